```python
import jax, jax.numpy as jnp
from jax import lax
import numpy as np

D_MODEL = 2048
BATCH = 4
SEQ = 4096
DEPTH = 1

MIX_WIDTH = D_MODEL
RWKV_WIDTH = MIX_WIDTH // 2
POOL_WIDTH = MIX_WIDTH - RWKV_WIDTH
RWKV_HEAD_SIZE = 64
RWKV_HEADS = RWKV_WIDTH // RWKV_HEAD_SIZE
DECAY_LORA = 64
ICLR_LORA = 64
GATE_LORA = 160
POOL_WINDOWS = (2, 4, 8, 16)
POOL_GROUPS = len(POOL_WINDOWS)
POOL_GROUP_WIDTH = POOL_WIDTH // POOL_GROUPS
SHIFT_WIDTH = 3 * RWKV_WIDTH + DECAY_LORA + ICLR_LORA + GATE_LORA
IN_COLS = SHIFT_WIDTH + POOL_WIDTH
N_EXPERTS = 32
TOP_K = 4
D_FF_EXPERT = D_MODEL
SWIGLU_LIMIT = 7.0
SWIGLU_ALPHA = 1.702
MOE_BLOCK = 128
LN_EPS = 1e-5
GN_EPS = 64e-5
DEEPNORM_ALPHA = (2 * DEPTH) ** 0.25
DEEPNORM_BETA = (8 * DEPTH) ** -0.25

kernel_name = 'hybrid_rwkv7_pool_moe_deepnorm'


def _layer_norm(x, w, b, eps=LN_EPS):
    xf = x.astype(jnp.float32)
    mu = xf.mean(-1, keepdims=True)
    var = jnp.square(xf - mu).mean(-1, keepdims=True)
    return (xf - mu) * lax.rsqrt(var + eps) * w + b


def _wkv7(r, decay, k, v, a, b):
    def step(S, inp):
        r_t, w_t, k_t, v_t, a_t, b_t = inp
        sa = jnp.einsum('bhij,bhj->bhi', S, a_t)
        S = (S * w_t[:, :, None, :] + sa[..., None] * b_t[:, :, None, :]
             + v_t[..., None] * k_t[:, :, None, :])
        return S, jnp.einsum('bhij,bhj->bhi', S, r_t)
    bsz, _, heads, n = r.shape
    s0 = jnp.zeros((bsz, heads, n, n), jnp.float32)
    seq_first = tuple(jnp.swapaxes(z, 0, 1) for z in (r, decay, k, v, a, b))
    _, y = lax.scan(step, s0, seq_first)
    return jnp.swapaxes(y, 0, 1)


def _rwkv7_group(p, mu_shift, w0, w2_decay, a0, a2_iclr, g2_gate, k_k, k_a, r_k, lnx_w, lnx_b):
    bsz, t, _ = p.shape
    prev = jnp.pad(p[:, :-1], ((0, 0), (1, 0), (0, 0)))
    p = p + (prev - p) * mu_shift
    r, k, v, xw, xa, xg = jnp.split(p, [RWKV_WIDTH, 2 * RWKV_WIDTH, 3 * RWKV_WIDTH,
                                        3 * RWKV_WIDTH + DECAY_LORA,
                                        3 * RWKV_WIDTH + DECAY_LORA + ICLR_LORA], axis=-1)
    w_logit = -jax.nn.softplus(-(w0 + jnp.tanh(xw) @ w2_decay)) - 0.5
    decay = jnp.exp(-jnp.exp(w_logit))
    a = jax.nn.sigmoid(a0 + xa @ a2_iclr)
    g = jax.nn.sigmoid(xg) @ g2_gate
    heads = lambda z: z.reshape(bsz, t, RWKV_HEADS, RWKV_HEAD_SIZE)
    kk = heads(k * k_k)
    kk = kk * lax.rsqrt(jnp.maximum(jnp.sum(kk * kk, -1, keepdims=True), 1e-24))
    k = k * (1.0 + (a - 1.0) * k_a)
    r_h, k_h, v_h, a_h = heads(r), heads(k), heads(v), heads(a)
    y = _wkv7(r_h, heads(decay), k_h, v_h, -kk, kk * a_h)
    y = _layer_norm(y, lnx_w, lnx_b, GN_EPS)
    y = y + jnp.sum(r_h * k_h * r_k, -1, keepdims=True) * v_h
    return y.reshape(bsz, t, RWKV_WIDTH) * g


def _pool_group(p, w_pool, pool_scale):
    bsz, t, _ = p.shape
    c = jnp.cumsum(p, axis=1)
    t_idx = jnp.arange(t)
    outs = []
    for gi, win in enumerate(POOL_WINDOWS):
        sl = slice(gi * POOL_GROUP_WIDTH, (gi + 1) * POOL_GROUP_WIDTH)
        cg = c[..., sl]
        c_lag = jnp.pad(cg[:, :t - win], ((0, 0), (win, 0), (0, 0)))
        count = jnp.minimum(t_idx + 1, win).astype(jnp.float32)[None, :, None]
        outs.append((cg - c_lag) / count - p[..., sl])
    pooled = jnp.stack(outs, axis=2)
    mixed = jnp.einsum('btgc,gcd->btgd', pooled, w_pool)
    return mixed.reshape(bsz, t, POOL_WIDTH) * pool_scale


def _moe(h, router_w, router_b, w1_exp, b1_exp, w2_exp, b2_exp):
    bsz, t, d = h.shape
    n_tok = bsz * t
    n_asg = n_tok * TOP_K
    n_blocks = -(-n_asg // MOE_BLOCK) + N_EXPERTS
    xf = h.reshape(n_tok, d)
    logits = (xf @ router_w + router_b).astype(jnp.float32)
    top_logits, top_idx = lax.top_k(logits, TOP_K)
    gates = jax.nn.softmax(top_logits, axis=-1)
    expert_of = top_idx.reshape(n_asg)
    order = jnp.argsort(expert_of, stable=True)
    sorted_expert = expert_of[order]
    sorted_token = order // TOP_K
    counts = jnp.bincount(expert_of, length=N_EXPERTS)
    padded = (counts + MOE_BLOCK - 1) // MOE_BLOCK * MOE_BLOCK
    start = jnp.cumsum(counts) - counts
    padded_end = jnp.cumsum(padded)
    padded_start = padded_end - padded
    dest = padded_start[sorted_expert] + jnp.arange(n_asg) - start[sorted_expert]
    slot_token = jnp.full((n_blocks * MOE_BLOCK,), n_tok, jnp.int32).at[dest].set(sorted_token)
    block_expert = jnp.minimum(
        jnp.searchsorted(padded_end, jnp.arange(n_blocks) * MOE_BLOCK, side='right'),
        N_EXPERTS - 1)
    x_slots = jnp.concatenate([xf, jnp.zeros((1, d), xf.dtype)])[slot_token]
    x_slots = x_slots.reshape(n_blocks, MOE_BLOCK, d)

    def expert_block(args):
        xb, e = args
        hb = xb @ w1_exp[e] + b1_exp[e]
        x_glu = jnp.minimum(hb[:, 0::2], SWIGLU_LIMIT)
        x_lin = jnp.clip(hb[:, 1::2], -SWIGLU_LIMIT, SWIGLU_LIMIT)
        act = x_glu * jax.nn.sigmoid(SWIGLU_ALPHA * x_glu) * (x_lin + 1.0)
        return act @ w2_exp[e] + b2_exp[e]

    y_slots = lax.map(expert_block, (x_slots, block_expert)).reshape(-1, d)
    y_asg = y_slots[dest].astype(jnp.float32) * gates.reshape(n_asg)[order][:, None]
    y = jax.ops.segment_sum(y_asg, sorted_token, num_segments=n_tok)
    return y.reshape(bsz, t, d).astype(h.dtype)


def setup_inputs(seed: int = 0) -> dict:
    key = jax.random.key(seed)
    ks = jax.random.split(key, 26)
    nrm = lambda k, shape, s: s * jax.random.normal(k, shape, jnp.float32)
    L = DEPTH
    return {
        'x': nrm(ks[0], (BATCH, SEQ, D_MODEL), 1.0),
        'w_in': nrm(ks[1], (L, D_MODEL, IN_COLS), D_MODEL ** -0.5),
        'mu_shift': jax.random.uniform(ks[2], (L, SHIFT_WIDTH), jnp.float32),
        'w0': jax.random.uniform(ks[3], (L, RWKV_WIDTH), jnp.float32, -6.0, -0.5),
        'w2_decay': nrm(ks[4], (L, DECAY_LORA, RWKV_WIDTH), 0.5 * DECAY_LORA ** -0.5),
        'a0': nrm(ks[5], (L, RWKV_WIDTH), 0.5),
        'a2_iclr': nrm(ks[6], (L, ICLR_LORA, RWKV_WIDTH), 0.5 * ICLR_LORA ** -0.5),
        'g2_gate': nrm(ks[7], (L, GATE_LORA, RWKV_WIDTH), GATE_LORA ** -0.5),
        'k_k': 0.85 + nrm(ks[8], (L, RWKV_WIDTH), 0.05),
        'k_a': 1.0 + nrm(ks[9], (L, RWKV_WIDTH), 0.05),
        'r_k': nrm(ks[10], (L, RWKV_HEADS, RWKV_HEAD_SIZE), 0.1),
        'lnx_w': 1.0 + nrm(ks[11], (L, RWKV_HEADS, RWKV_HEAD_SIZE), 0.05),
        'lnx_b': nrm(ks[12], (L, RWKV_HEADS, RWKV_HEAD_SIZE), 0.02),
        'w_pool': nrm(ks[13], (L, POOL_GROUPS, POOL_GROUP_WIDTH, POOL_GROUP_WIDTH), POOL_GROUP_WIDTH ** -0.5),
        'pool_scale': 1.0 + nrm(ks[14], (L, POOL_WIDTH), 0.05),
        'w_out': nrm(ks[15], (L, MIX_WIDTH, D_MODEL), DEEPNORM_BETA * MIX_WIDTH ** -0.5),
        'ln1_w': 1.0 + nrm(ks[16], (L, D_MODEL), 0.05),
        'ln1_b': nrm(ks[17], (L, D_MODEL), 0.02),
        'router_w': nrm(ks[18], (L, D_MODEL, N_EXPERTS), D_MODEL ** -0.5),
        'router_b': nrm(ks[19], (L, N_EXPERTS), 0.01),
        'w1_exp': nrm(ks[20], (L, N_EXPERTS, D_MODEL, 2 * D_FF_EXPERT), D_MODEL ** -0.5),
        'b1_exp': nrm(ks[21], (L, N_EXPERTS, 2 * D_FF_EXPERT), 0.01),
        'w2_exp': nrm(ks[22], (L, N_EXPERTS, D_FF_EXPERT, D_MODEL), DEEPNORM_BETA * D_FF_EXPERT ** -0.5),
        'b2_exp': nrm(ks[23], (L, N_EXPERTS, D_MODEL), 0.01),
        'ln2_w': 1.0 + nrm(ks[24], (L, D_MODEL), 0.05),
        'ln2_b': nrm(ks[25], (L, D_MODEL), 0.02),
    }


def reference(x, w_in, mu_shift, w0, w2_decay, a0, a2_iclr, g2_gate, k_k, k_a, r_k,
              lnx_w, lnx_b, w_pool, pool_scale, w_out, ln1_w, ln1_b, router_w, router_b,
              w1_exp, b1_exp, w2_exp, b2_exp, ln2_w, ln2_b):
    h = x
    for l in range(DEPTH):
        proj = (h @ w_in[l]).astype(jnp.float32)
        y_rwkv = _rwkv7_group(proj[..., :SHIFT_WIDTH], mu_shift[l], w0[l], w2_decay[l],
                              a0[l], a2_iclr[l], g2_gate[l], k_k[l], k_a[l], r_k[l],
                              lnx_w[l], lnx_b[l])
        y_pool = _pool_group(proj[..., SHIFT_WIDTH:], w_pool[l], pool_scale[l])
        mix = jnp.concatenate([y_rwkv, y_pool], axis=-1).astype(h.dtype) @ w_out[l]
        h = _layer_norm(DEEPNORM_ALPHA * h + mix, ln1_w[l], ln1_b[l]).astype(x.dtype)
        ffn = _moe(h, router_w[l], router_b[l], w1_exp[l], b1_exp[l], w2_exp[l], b2_exp[l])
        h = _layer_norm(DEEPNORM_ALPHA * h + ffn, ln2_w[l], ln2_b[l]).astype(x.dtype)
    return h
```

```python
import jax
import jax.numpy as jnp
from jax import lax
from jax.experimental import pallas as pl
from jax.experimental.pallas import tpu as pltpu

F32 = jnp.float32
BF16 = jnp.bfloat16

RWKV_WIDTH = 1024
HEAD = 64
DECAY_LORA = 64
ICLR_LORA = 64
GATE_LORA = 160
POOL_WINDOWS = (2, 4, 8, 16)
POOL_GROUP_WIDTH = 256
TOP_K = 4
SWIGLU_LIMIT = 7.0
SWIGLU_ALPHA = 1.702
LN_EPS = 1e-5
GN_EPS = 64e-5
DEEPNORM_ALPHA = 2.0 ** 0.25

LANES = 128
LORA_PAD = 512
LORA_XW, LORA_XA, LORA_XG = 0, 128, 256
RKV_COLS = 3 * RWKV_WIDTH
POOL_WIDTH = len(POOL_WINDOWS) * POOL_GROUP_WIDTH
POOL_COL0 = RKV_COLS
LORA_COL0 = POOL_COL0 + POOL_WIDTH
PROJ_COLS = LORA_COL0 + LORA_PAD
CHUNK = 64
SUB = 16
VMEM_LIMIT = 56 * 1024 * 1024


def _cparams(sem):
    return pltpu.CompilerParams(dimension_semantics=sem, vmem_limit_bytes=VMEM_LIMIT)


def _bdot(a, b):
    return jnp.dot(a.astype(BF16), b.astype(BF16), preferred_element_type=F32)


def _bdot_nt(a, b):
    return lax.dot_general(a.astype(BF16), b.astype(BF16), (((1,), (1,)), ((), ())),
                           preferred_element_type=F32)


def _bdot_tn(a, b):
    return lax.dot_general(a.astype(BF16), b.astype(BF16), (((0,), (0,)), ((), ())),
                           preferred_element_type=F32)


def _in_proj_kernel(x_ref, w_ref, o_ref, xb_ref):
    @pl.when(pl.program_id(1) == 0)
    def _():
        xb_ref[...] = x_ref[...].astype(BF16)

    o_ref[...] = jnp.dot(xb_ref[...], w_ref[...], preferred_element_type=F32)


def _in_proj(x2d, w_p, tm=512, tn=768):
    n, d = x2d.shape
    cols = w_p.shape[1]
    return pl.pallas_call(
        _in_proj_kernel,
        grid=(n // tm, cols // tn),
        in_specs=[pl.BlockSpec((tm, d), lambda i, j: (i, 0)),
                  pl.BlockSpec((d, tn), lambda i, j: (0, j))],
        out_specs=pl.BlockSpec((tm, tn), lambda i, j: (i, j)),
        out_shape=jax.ShapeDtypeStruct((n, cols), F32),
        scratch_shapes=[pltpu.VMEM((tm, d), BF16)],
        compiler_params=_cparams(("arbitrary", "arbitrary")),
        name="in_proj",
    )(x2d, w_p)


def _shifted(x_ref, prev_ref, mu_ref, first_row):
    x = x_ref[...]
    xp = jnp.where(first_row, prev_ref[...], pltpu.roll(x, 1, axis=0))
    prev_ref[...] = x_ref[x.shape[0] - 1:x.shape[0], :]
    return x + (xp - x) * mu_ref[...]


def _rwkv_kernel(r_ref, k_ref, v_ref, lo_ref, mur_ref, muk_ref, muv_ref, mul_ref,
                 w0_ref, w2d_ref, a0_ref, a2_ref, g2_ref, kk_ref, ka_ref, rk_ref, lnw_ref, lnb_ref,
                 o_ref,
                 pr_ref, pk_ref, pv_ref, plo_ref, s_ref,
                 rs_ref, lw_ref, ks_ref, vs_ref, kks_ref, as_ref, gs_ref):
    tb = r_ref.shape[0]
    c2 = 2 * CHUNK

    @pl.when(pl.program_id(2) == 0)
    def _():
        pr_ref[...] = jnp.zeros_like(pr_ref)
        pk_ref[...] = jnp.zeros_like(pk_ref)
        pv_ref[...] = jnp.zeros_like(pv_ref)
        plo_ref[...] = jnp.zeros_like(plo_ref)
        s_ref[...] = jnp.zeros_like(s_ref)

    first_row = lax.broadcasted_iota(jnp.int32, (tb, 1), 0) == 0
    r = _shifted(r_ref, pr_ref, mur_ref, first_row)
    k = _shifted(k_ref, pk_ref, muk_ref, first_row)
    v = _shifted(v_ref, pv_ref, muv_ref, first_row)
    lo = _shifted(lo_ref, plo_ref, mul_ref, first_row)

    xw = jnp.tanh(lo[:, LORA_XW:LORA_XW + LANES])
    xa = lo[:, LORA_XA:LORA_XA + LANES]
    xg = jax.nn.sigmoid(lo[:, LORA_XG:LORA_PAD])
    z = w0_ref[...] + _bdot(xw, w2d_ref[...])
    w_logit = -(jnp.maximum(-z, 0.0) + jnp.log(1.0 + jnp.exp(-jnp.abs(z)))) - 0.5
    a_ic = jax.nn.sigmoid(a0_ref[...] + _bdot(xa, a2_ref[...]))
    rs_ref[...] = r
    lw_ref[...] = -jnp.exp(w_logit)
    ks_ref[...] = k * (1.0 + (a_ic - 1.0) * ka_ref[...])
    vs_ref[...] = v
    kks_ref[...] = k * kk_ref[...]
    as_ref[...] = a_ic
    gs_ref[...] = _bdot(xg, g2_ref[...])

    lane = lax.broadcasted_iota(jnp.int32, (CHUNK, LANES), 1)
    head0 = lane < HEAD
    row = lax.broadcasted_iota(jnp.int32, (c2, c2), 0)
    col = lax.broadcasted_iota(jnp.int32, (c2, c2), 1)
    strict = row > col
    incl = row >= col
    same_sub = (row // SUB) == (col // SUB)
    eye = (row == col).astype(F32)
    tri = (lax.broadcasted_iota(jnp.int32, (CHUNK, CHUNK), 0)
           >= lax.broadcasted_iota(jnp.int32, (CHUNK, CHUNK), 1)).astype(F32)
    rk = rk_ref[...]
    lnw = lnw_ref[...]
    lnb = lnb_ref[...]

    def head_sum(x):
        s0 = jnp.sum(jnp.where(head0, x, 0.0), axis=-1, keepdims=True)
        s1 = jnp.sum(jnp.where(head0, 0.0, x), axis=-1, keepdims=True)
        return jnp.where(head0, s0, s1)

    def stack(x):
        return jnp.concatenate([jnp.where(head0, x, 0.0), jnp.where(head0, 0.0, x)], axis=0)

    def unstack(x):
        return x[:CHUNK, :] + x[CHUNK:, :]

    def chunk(c, carry):
        sl = pl.ds(pl.multiple_of(c * CHUNK, CHUNK), CHUNK)
        rc, lw, kc, vc, kk, ac = (ref[sl, :] for ref in (rs_ref, lw_ref, ks_ref, vs_ref, kks_ref, as_ref))
        kkn = kk * lax.rsqrt(jnp.maximum(head_sum(kk * kk), 1e-24))
        a_ = -kkn
        b_ = kkn * ac
        lwi = jnp.dot(tri, lw, preferred_element_type=F32, precision=lax.Precision.HIGHEST)
        tot = lwi[CHUNK - 1:CHUNK, :]
        e_in = jnp.exp(lwi)
        e_inv = jnp.exp(-lwi)
        e_out = jnp.exp(tot - lwi)
        at = stack(a_ * jnp.exp(lwi - lw))
        rt = stack(rc * e_in)
        bt = stack(b_ * e_inv)
        kt = stack(kc * e_inv)
        bh = stack(b_ * e_out)
        kh = stack(kc * e_out)
        vst = stack(vc)

        a_ab = jnp.where(strict, _bdot_nt(at, bt), 0.0)
        a_ak = jnp.where(strict, _bdot_nt(at, kt), 0.0)
        a_rb = jnp.where(incl, _bdot_nt(rt, bt), 0.0)
        a_rk = jnp.where(incl, _bdot_nt(rt, kt), 0.0)

        d1 = jnp.where(same_sub, a_ab, 0.0)
        l1 = a_ab - d1
        d2 = _bdot(d1, d1)
        d4 = _bdot(d2, d2)
        d8 = _bdot(d4, d4)
        t_d = _bdot(_bdot(eye + d1, eye + d2), _bdot(eye + d4, eye + d8))
        m1 = _bdot(t_d, l1)
        m2 = _bdot(m1, m1)
        t_inv = _bdot(_bdot(eye + m1, eye + m2), t_d)

        s_bd = s_ref[...]
        w_st = _bdot(t_inv, at)
        u_v = _bdot(t_inv, _bdot(a_ak, vst))
        u_st = _bdot_nt(w_st, s_bd) + u_v
        y_st = _bdot_nt(rt, s_bd) + _bdot(a_rb, u_st) + _bdot(a_rk, vst)
        s_ref[...] = s_bd * jnp.exp(tot) + _bdot_tn(u_st, bh) + _bdot_tn(vst, kh)

        y = unstack(y_st)
        mu = head_sum(y) * (1.0 / HEAD)
        yc = y - mu
        var = head_sum(yc * yc) * (1.0 / HEAD)
        yn = yc * lax.rsqrt(var + GN_EPS) * lnw + lnb
        yn = yn + head_sum(rc * kc * rk) * vc
        o_ref[sl, :] = (yn * gs_ref[sl, :]).astype(o_ref.dtype)
        return carry

    lax.fori_loop(0, tb // CHUNK, chunk, 0)


def _rwkv_group(proj, batch, seq, mu_rkv, mu_lora, w0, w2d, a0, a2, g2, k_k, k_a, r_k, lnx_w, lnx_b, tb=512):
    n_hp = RWKV_WIDTH // LANES
    n_tb = seq // tb
    row = lambda b, h, t: b * n_tb + t
    vec = lambda off: pl.BlockSpec((1, LANES), lambda b, h, t: (0, off + h))
    lora_blk = LORA_COL0 // LORA_PAD
    in_specs = [
        pl.BlockSpec((tb, LANES), lambda b, h, t: (row(b, h, t), h)),
        pl.BlockSpec((tb, LANES), lambda b, h, t: (row(b, h, t), n_hp + h)),
        pl.BlockSpec((tb, LANES), lambda b, h, t: (row(b, h, t), 2 * n_hp + h)),
        pl.BlockSpec((tb, LORA_PAD), lambda b, h, t: (row(b, h, t), lora_blk)),
        vec(0), vec(n_hp), vec(2 * n_hp),
        pl.BlockSpec((1, LORA_PAD), lambda b, h, t: (0, 0)),
        vec(0),
        pl.BlockSpec((LANES, LANES), lambda b, h, t: (0, h)),
        vec(0),
        pl.BlockSpec((LANES, LANES), lambda b, h, t: (0, h)),
        pl.BlockSpec((LORA_PAD - LORA_XG, LANES), lambda b, h, t: (0, h)),
        vec(0), vec(0), vec(0), vec(0), vec(0),
    ]
    blk = lambda: pltpu.VMEM((tb, LANES), F32)
    return pl.pallas_call(
        _rwkv_kernel,
        grid=(batch, n_hp, n_tb),
        in_specs=in_specs,
        out_specs=pl.BlockSpec((tb, LANES), lambda b, h, t: (row(b, h, t), h)),
        out_shape=jax.ShapeDtypeStruct((batch * seq, RWKV_WIDTH), BF16),
        scratch_shapes=[pltpu.VMEM((1, LANES), F32), pltpu.VMEM((1, LANES), F32), pltpu.VMEM((1, LANES), F32),
                        pltpu.VMEM((1, LORA_PAD), F32), pltpu.VMEM((2 * CHUNK, LANES), F32),
                        blk(), blk(), blk(), blk(), blk(), blk(), blk()],
        compiler_params=_cparams(("arbitrary", "arbitrary", "arbitrary")),
        name="rwkv7_group",
    )(proj, proj, proj, proj, mu_rkv, mu_rkv, mu_rkv, mu_lora, w0, w2d, a0, a2, g2, k_k, k_a, r_k, lnx_w, lnx_b)


HALO = 16


def _pool_kernel(p_ref, w_ref, sc_ref, o_ref, ext_ref):
    tb = p_ref.shape[0]
    t = pl.program_id(1)

    @pl.when(t == 0)
    def _():
        ext_ref[0:HALO, :] = jnp.zeros((HALO, ext_ref.shape[1]), F32)

    @pl.when(t > 0)
    def _():
        ext_ref[0:HALO, :] = ext_ref[tb:tb + HALO, :]

    ext_ref[HALO:HALO + tb, :] = p_ref[...]
    t_idx = t * tb + lax.broadcasted_iota(jnp.int32, (tb, 1), 0)
    for gi, win in enumerate(POOL_WINDOWS):
        cs = slice(gi * POOL_GROUP_WIDTH, (gi + 1) * POOL_GROUP_WIDTH)
        acc = ext_ref[HALO:HALO + tb, cs]
        for d in range(1, win):
            acc = acc + ext_ref[HALO - d:HALO - d + tb, cs]
        count = jnp.minimum(t_idx + 1, win).astype(F32)
        pooled = acc / count - ext_ref[HALO:HALO + tb, cs]
        mixed = _bdot(pooled, w_ref[gi])
        o_ref[:, cs] = (mixed * sc_ref[:, cs]).astype(o_ref.dtype)


def _pool_group(proj, batch, seq, w_pool_b, pool_scale, tb=512):
    n_tb = seq // tb
    col_blk = POOL_COL0 // POOL_WIDTH
    return pl.pallas_call(
        _pool_kernel,
        grid=(batch, n_tb),
        in_specs=[pl.BlockSpec((tb, POOL_WIDTH), lambda b, t: (b * n_tb + t, col_blk)),
                  pl.BlockSpec(w_pool_b.shape, lambda b, t: (0, 0, 0)),
                  pl.BlockSpec((1, POOL_WIDTH), lambda b, t: (0, 0))],
        out_specs=pl.BlockSpec((tb, POOL_WIDTH), lambda b, t: (b * n_tb + t, 0)),
        out_shape=jax.ShapeDtypeStruct((batch * seq, POOL_WIDTH), BF16),
        scratch_shapes=[pltpu.VMEM((tb + HALO, POOL_WIDTH), F32)],
        compiler_params=_cparams(("arbitrary", "arbitrary")),
        name="pool_group",
    )(proj, w_pool_b, pool_scale)


def _layer_norm(x, w, b, eps):
    mu = jnp.mean(x, axis=-1, keepdims=True)
    xc = x - mu
    var = jnp.mean(xc * xc, axis=-1, keepdims=True)
    return xc * lax.rsqrt(var + eps) * w + b


def _mix_kernel(yr_ref, yp_ref, x_ref, wo_ref, lw_ref, lb_ref, rw_ref, rb_ref,
                h_ref, ri_ref, g_ref, cnt_ref, carry_ref):
    tm = x_ref.shape[0]
    half = yr_ref.shape[1]

    @pl.when(pl.program_id(0) == 0)
    def _():
        carry_ref[...] = jnp.zeros_like(carry_ref)

    mix = (jnp.dot(yr_ref[...], wo_ref[0:half, :], preferred_element_type=F32)
           + jnp.dot(yp_ref[...], wo_ref[half:, :], preferred_element_type=F32))
    h = _layer_norm(DEEPNORM_ALPHA * x_ref[...] + mix, lw_ref[...], lb_ref[...], LN_EPS)
    h_ref[...] = h

    logits = jnp.dot(h, rw_ref[...], preferred_element_type=F32, precision=lax.Precision.HIGHEST) + rb_ref[...]
    lane = lax.broadcasted_iota(jnp.int32, (tm, LANES), 1).astype(F32)
    idxs, vals = [], []
    left = logits
    for _ in range(TOP_K):
        m = jnp.max(left, axis=-1, keepdims=True)
        idx = jnp.min(jnp.where(left == m, lane, float(LANES)), axis=-1, keepdims=True)
        idxs.append(idx)
        vals.append(m)
        left = jnp.where(lane == idx, -jnp.inf, left)
    exps = [jnp.exp(v - vals[0]) for v in vals]
    denom = exps[0] + exps[1] + exps[2] + exps[3]

    onehot = jnp.zeros((tm, LANES), F32)
    for idx in idxs:
        onehot = onehot + (lane == idx).astype(F32)
    tri = (lax.broadcasted_iota(jnp.int32, (tm, tm), 0)
           > lax.broadcasted_iota(jnp.int32, (tm, tm), 1)).astype(BF16)
    before = jnp.dot(tri, onehot.astype(BF16), preferred_element_type=F32) + carry_ref[...]
    carry_ref[...] = carry_ref[...] + jnp.sum(onehot, axis=0, keepdims=True)
    cnt_ref[...] = carry_ref[...]

    ri = jnp.zeros((tm, LANES), F32)
    gt = jnp.zeros((tm, LANES), F32)
    for k in range(TOP_K):
        rank = jnp.sum(jnp.where(lane == idxs[k], before, 0.0), axis=-1, keepdims=True)
        ri = jnp.where(lane == float(k), idxs[k], ri)
        ri = jnp.where(lane == float(TOP_K + k), rank, ri)
        gt = jnp.where(lane == float(k), exps[k] / denom, gt)
    ri_ref[...] = ri.astype(jnp.int32)
    g_ref[...] = gt


def _mix_ln_router(y_rwkv, y_pool, x2d, w_out_b, ln_w, ln_b, router_w_p, router_b_p, tm=512):
    n, d = x2d.shape
    half = y_rwkv.shape[1]
    const = lambda shape: pl.BlockSpec(shape, lambda i: (0, 0))
    return pl.pallas_call(
        _mix_kernel,
        grid=(n // tm,),
        in_specs=[pl.BlockSpec((tm, half), lambda i: (i, 0)),
                  pl.BlockSpec((tm, y_pool.shape[1]), lambda i: (i, 0)),
                  pl.BlockSpec((tm, d), lambda i: (i, 0)),
                  const(w_out_b.shape), const((1, d)), const((1, d)),
                  const(router_w_p.shape), const((1, LANES))],
        out_specs=[pl.BlockSpec((tm, d), lambda i: (i, 0)),
                   pl.BlockSpec((tm, LANES), lambda i: (i, 0)),
                   pl.BlockSpec((tm, LANES), lambda i: (i, 0)),
                   const((1, LANES))],
        out_shape=[jax.ShapeDtypeStruct((n, d), F32),
                   jax.ShapeDtypeStruct((n, LANES), jnp.int32),
                   jax.ShapeDtypeStruct((n, LANES), F32),
                   jax.ShapeDtypeStruct((1, LANES), F32)],
        scratch_shapes=[pltpu.VMEM((1, LANES), F32)],
        compiler_params=_cparams(("arbitrary",)),
        name="mix_ln_router",
    )(y_rwkv, y_pool, x2d, w_out_b, ln_w, ln_b, router_w_p, router_b_p)


MOE_BM = 512


def _gather_kernel(tok_ref, nu_ref, h_hbm, o_ref, buf, sem):
    m = pl.program_id(0)
    bm = o_ref.shape[0]

    @pl.when(m < nu_ref[0])
    def _():
        def issue(i, carry):
            tok = tok_ref[m * bm + i]
            pltpu.make_async_copy(h_hbm.at[pl.ds(tok, 1), :], buf.at[pl.ds(i, 1), :], sem).start()
            return carry

        lax.fori_loop(0, bm, issue, 0)
        pltpu.make_async_copy(h_hbm.at[pl.ds(0, bm), :], buf, sem).wait()
        o_ref[...] = buf[...].astype(o_ref.dtype)

    @pl.when(m >= nu_ref[0])
    def _():
        o_ref[...] = jnp.zeros_like(o_ref)


def _dispatch(h, slot_token, n_used, n_blocks, bm=MOE_BM):
    d = h.shape[1]
    return pl.pallas_call(
        _gather_kernel,
        grid_spec=pltpu.PrefetchScalarGridSpec(
            num_scalar_prefetch=2,
            grid=(n_blocks,),
            in_specs=[pl.BlockSpec(memory_space=pl.ANY)],
            out_specs=pl.BlockSpec((bm, d), lambda m, tok, nu: (m, 0)),
            scratch_shapes=[pltpu.VMEM((bm, d), F32), pltpu.SemaphoreType.DMA(())]),
        out_shape=jax.ShapeDtypeStruct((n_blocks * bm, d), BF16),
        compiler_params=_cparams(("arbitrary",)),
        name="moe_dispatch",
    )(slot_token, n_used, h)


def _ffn_kernel(be_ref, nu_ref, x_ref, w1g_ref, w1l_ref, b1g_ref, b1l_ref, w2_ref, b2_ref, o_ref):
    m = pl.program_id(0)
    f = pl.program_id(1)

    @pl.when(m < nu_ref[0])
    def _():
        x = x_ref[...]
        hg = jnp.dot(x, w1g_ref[0], preferred_element_type=F32) + b1g_ref[0]
        hl = jnp.dot(x, w1l_ref[0], preferred_element_type=F32) + b1l_ref[0]
        x_glu = jnp.minimum(hg, SWIGLU_LIMIT)
        x_lin = jnp.clip(hl, -SWIGLU_LIMIT, SWIGLU_LIMIT)
        act = x_glu * jax.nn.sigmoid(SWIGLU_ALPHA * x_glu) * (x_lin + 1.0)
        part = jnp.dot(act.astype(BF16), w2_ref[0], preferred_element_type=F32)

        @pl.when(f == 0)
        def _():
            o_ref[...] = part + b2_ref[0]

        @pl.when(f > 0)
        def _():
            o_ref[...] += part

    @pl.when((m >= nu_ref[0]) & (f == 0))
    def _():
        o_ref[...] = jnp.zeros_like(o_ref)


def _experts(x_slots, block_expert, n_used, w1g, w1l, b1g, b1l, w2, b2, bm=MOE_BM, tf=512):
    n_slots, d = x_slots.shape
    n_blocks = n_slots // bm
    ff = w1g.shape[2]
    n_f = ff // tf

    def m_eff(m, nu):
        return jnp.minimum(m, nu[0] - 1)

    def f_eff(m, f, nu):
        return jnp.where(m < nu[0], f, n_f - 1)

    return pl.pallas_call(
        _ffn_kernel,
        grid_spec=pltpu.PrefetchScalarGridSpec(
            num_scalar_prefetch=2,
            grid=(n_blocks, n_f),
            in_specs=[
                pl.BlockSpec((bm, d), lambda m, f, be, nu: (m_eff(m, nu), 0)),
                pl.BlockSpec((1, d, tf), lambda m, f, be, nu: (be[m], 0, f_eff(m, f, nu))),
                pl.BlockSpec((1, d, tf), lambda m, f, be, nu: (be[m], 0, f_eff(m, f, nu))),
                pl.BlockSpec((1, 1, tf), lambda m, f, be, nu: (be[m], 0, f_eff(m, f, nu))),
                pl.BlockSpec((1, 1, tf), lambda m, f, be, nu: (be[m], 0, f_eff(m, f, nu))),
                pl.BlockSpec((1, tf, d), lambda m, f, be, nu: (be[m], f_eff(m, f, nu), 0)),
                pl.BlockSpec((1, 1, d), lambda m, f, be, nu: (be[m], 0, 0)),
            ],
            out_specs=pl.BlockSpec((bm, d), lambda m, f, be, nu: (m, 0))),
        out_shape=jax.ShapeDtypeStruct((n_slots, d), F32),
        compiler_params=_cparams(("arbitrary", "arbitrary")),
        name="moe_experts",
    )(block_expert, n_used, x_slots, w1g, w1l, b1g, b1l, w2, b2)


def _combine_kernel(pos_ref, h_ref, g_ref, lw_ref, lb_ref, y_hbm, o_ref, buf, sem):
    i = pl.program_id(0)
    tm = h_ref.shape[0]

    def issue(n, carry):
        for k in range(TOP_K):
            p = pos_ref[(i * tm + n) * TOP_K + k]
            pltpu.make_async_copy(y_hbm.at[pl.ds(p, 1), :], buf.at[pl.ds(k * tm + n, 1), :], sem).start()
        return carry

    lax.fori_loop(0, tm, issue, 0)
    pltpu.make_async_copy(y_hbm.at[pl.ds(0, TOP_K * tm), :], buf, sem).wait()
    g = g_ref[...]
    ffn = g[:, 0:1] * buf[0:tm, :]
    for k in range(1, TOP_K):
        ffn = ffn + g[:, k:k + 1] * buf[k * tm:(k + 1) * tm, :]
    o_ref[...] = _layer_norm(DEEPNORM_ALPHA * h_ref[...] + ffn, lw_ref[...], lb_ref[...], LN_EPS)


def _combine(pos_flat, h, gates, ln_w, ln_b, y_slots, tm=256):
    n, d = h.shape
    return pl.pallas_call(
        _combine_kernel,
        grid_spec=pltpu.PrefetchScalarGridSpec(
            num_scalar_prefetch=1,
            grid=(n // tm,),
            in_specs=[pl.BlockSpec((tm, d), lambda i, pos: (i, 0)),
                      pl.BlockSpec((tm, LANES), lambda i, pos: (i, 0)),
                      pl.BlockSpec((1, d), lambda i, pos: (0, 0)),
                      pl.BlockSpec((1, d), lambda i, pos: (0, 0)),
                      pl.BlockSpec(memory_space=pl.ANY)],
            out_specs=pl.BlockSpec((tm, d), lambda i, pos: (i, 0)),
            scratch_shapes=[pltpu.VMEM((TOP_K * tm, d), F32), pltpu.SemaphoreType.DMA(())]),
        out_shape=jax.ShapeDtypeStruct((n, d), F32),
        compiler_params=_cparams(("arbitrary",)),
        name="moe_combine_ln",
    )(pos_flat, h, gates, ln_w, ln_b, y_slots)


def _moe(h, route_i, gates, counts_f, w1, b1, w2, b2, ln_w, ln_b):
    n_tok, d = h.shape
    n_exp = w1.shape[0]
    n_asg = n_tok * TOP_K
    n_blocks = -(-n_asg // MOE_BM) + n_exp

    counts = counts_f[0, :n_exp].astype(jnp.int32)
    padded = (counts + MOE_BM - 1) // MOE_BM * MOE_BM
    pend = jnp.cumsum(padded)
    pstart = pend - padded
    pos = pstart[route_i[:, 0:TOP_K]] + route_i[:, TOP_K:2 * TOP_K]
    pos_flat = pos.reshape(n_asg)
    token_of = jnp.arange(n_asg, dtype=jnp.int32) // TOP_K
    slot_token = jnp.zeros((n_blocks * MOE_BM,), jnp.int32).at[pos_flat].set(token_of)
    n_used = (pend[-1] // MOE_BM).astype(jnp.int32).reshape(1)
    blk = jnp.minimum(jnp.arange(n_blocks, dtype=jnp.int32), n_used[0] - 1)
    block_expert = jnp.minimum(jnp.searchsorted(pend, blk * MOE_BM, side='right'), n_exp - 1).astype(jnp.int32)

    w1g = w1[:, :, 0::2].astype(BF16)
    w1l = w1[:, :, 1::2].astype(BF16)
    b1g = b1[:, None, 0::2]
    b1l = b1[:, None, 1::2]
    w2b = w2.astype(BF16)

    x_slots = _dispatch(h, slot_token, n_used, n_blocks)
    y_slots = _experts(x_slots, block_expert, n_used, w1g, w1l, b1g, b1l, w2b, b2[:, None, :])
    return _combine(pos_flat, h, gates, ln_w, ln_b, y_slots)


def _pad_to(a, axis, size):
    pad = [(0, 0)] * a.ndim
    pad[axis] = (0, size - a.shape[axis])
    return jnp.pad(a, pad)


def kernel(x, w_in, mu_shift, w0, w2_decay, a0, a2_iclr, g2_gate, k_k, k_a, r_k, lnx_w, lnx_b, w_pool, pool_scale,
           w_out, ln1_w, ln1_b, router_w, router_b, w1_exp, b1_exp, w2_exp, b2_exp, ln2_w, ln2_b):
    batch, seq, d = x.shape
    assert w_in.shape[0] == 1, "one layer"
    x2d = x.reshape(batch * seq, d)
    c_xw = RKV_COLS
    c_xa = c_xw + DECAY_LORA
    c_xg = c_xa + ICLR_LORA
    c_pool = c_xg + GATE_LORA

    def lora_layout(a):
        return jnp.concatenate([_pad_to(a[:, c_xw:c_xa], 1, LORA_XA - LORA_XW),
                                _pad_to(a[:, c_xa:c_xg], 1, LORA_XG - LORA_XA),
                                _pad_to(a[:, c_xg:c_pool], 1, LORA_PAD - LORA_XG)], axis=1)

    wi = w_in[0]
    w_p = jnp.concatenate([wi[:, :RKV_COLS], wi[:, c_pool:], lora_layout(wi)], axis=1).astype(BF16)
    mu = mu_shift[0][None, :]
    row = lambda a: a.reshape(1, -1)

    proj = _in_proj(x2d, w_p)
    y_rwkv = _rwkv_group(
        proj, batch, seq, mu[:, :RKV_COLS], lora_layout(mu), row(w0[0]),
        _pad_to(w2_decay[0], 0, LANES).astype(BF16), row(a0[0]), _pad_to(a2_iclr[0], 0, LANES).astype(BF16),
        _pad_to(g2_gate[0], 0, LORA_PAD - LORA_XG).astype(BF16), row(k_k[0]), row(k_a[0]), row(r_k[0]),
        row(lnx_w[0]), row(lnx_b[0]))
    y_pool = _pool_group(proj, batch, seq, w_pool[0].astype(BF16), row(pool_scale[0]))

    n_exp = router_w.shape[2]
    router_w_p = _pad_to(router_w[0], 1, LANES)
    router_b_p = jnp.concatenate([router_b[0], jnp.full((LANES - n_exp,), -1e30, F32)])[None, :]
    h1, route_i, gates, counts = _mix_ln_router(y_rwkv, y_pool, x2d, w_out[0].astype(BF16), row(ln1_w[0]),
                                                row(ln1_b[0]), router_w_p, router_b_p)
    out = _moe(h1, route_i, gates, counts, w1_exp[0], b1_exp[0], w2_exp[0], b2_exp[0], row(ln2_w[0]), row(ln2_b[0]))
    return out.reshape(batch, seq, d)
```

```python
import jax
import jax.numpy as jnp
from jax import lax
from jax.experimental import pallas as pl
from jax.experimental.pallas import tpu as pltpu

F32 = jnp.float32
BF16 = jnp.bfloat16

RWKV_WIDTH = 1024
HEAD = 64
DECAY_LORA = 64
ICLR_LORA = 64
GATE_LORA = 160
POOL_WINDOWS = (2, 4, 8, 16)
POOL_GROUP_WIDTH = 256
TOP_K = 4
SWIGLU_LIMIT = 7.0
SWIGLU_ALPHA = 1.702
LN_EPS = 1e-5
GN_EPS = 64e-5
DEEPNORM_ALPHA = 2.0 ** 0.25

LANES = 128
LORA_PAD = 512
LORA_XW, LORA_XA, LORA_XG = 0, 128, 256
RKV_COLS = 3 * RWKV_WIDTH
POOL_WIDTH = len(POOL_WINDOWS) * POOL_GROUP_WIDTH
POOL_COL0 = RKV_COLS
LORA_COL0 = POOL_COL0 + POOL_WIDTH
PROJ_COLS = LORA_COL0 + LORA_PAD
CHUNK = 64
SUB = 16
VMEM_LIMIT = 56 * 1024 * 1024


def _cparams(sem):
    return pltpu.CompilerParams(dimension_semantics=sem, vmem_limit_bytes=VMEM_LIMIT)


def _bdot(a, b):
    return jnp.dot(a.astype(BF16), b.astype(BF16), preferred_element_type=F32)


def _bdot_nt(a, b):
    return lax.dot_general(a.astype(BF16), b.astype(BF16), (((1,), (1,)), ((), ())),
                           preferred_element_type=F32)


def _bdot_tn(a, b):
    return lax.dot_general(a.astype(BF16), b.astype(BF16), (((0,), (0,)), ((), ())),
                           preferred_element_type=F32)


def _in_proj_kernel(x_ref, w_ref, o_ref, xb_ref):
    @pl.when(pl.program_id(1) == 0)
    def _():
        xb_ref[...] = x_ref[...].astype(BF16)

    o_ref[...] = jnp.dot(xb_ref[...], w_ref[...], preferred_element_type=F32)


def _in_proj(x2d, w_p, tm=512, tn=768):
    n, d = x2d.shape
    cols = w_p.shape[1]
    return pl.pallas_call(
        _in_proj_kernel,
        grid=(n // tm, cols // tn),
        in_specs=[pl.BlockSpec((tm, d), lambda i, j: (i, 0)),
                  pl.BlockSpec((d, tn), lambda i, j: (0, j))],
        out_specs=pl.BlockSpec((tm, tn), lambda i, j: (i, j)),
        out_shape=jax.ShapeDtypeStruct((n, cols), F32),
        scratch_shapes=[pltpu.VMEM((tm, d), BF16)],
        compiler_params=_cparams(("arbitrary", "arbitrary")),
        name="in_proj",
    )(x2d, w_p)


def _shifted(x_ref, prev_ref, mu_ref, first_row):
    x = x_ref[...]
    xp = jnp.where(first_row, prev_ref[...], pltpu.roll(x, 1, axis=0))
    prev_ref[...] = x_ref[x.shape[0] - 1:x.shape[0], :]
    return x + (xp - x) * mu_ref[...]


def _rwkv_kernel(r_ref, k_ref, v_ref, lo_ref, mur_ref, muk_ref, muv_ref, mul_ref,
                 w0_ref, w2d_ref, a0_ref, a2_ref, g2_ref, kk_ref, ka_ref, rk_ref, lnw_ref, lnb_ref,
                 o_ref,
                 pr_ref, pk_ref, pv_ref, plo_ref, s_ref,
                 rs_ref, lw_ref, ks_ref, vs_ref, kks_ref, as_ref, gs_ref):
    tb, width = r_ref.shape
    n_pairs = width // LANES
    c2 = 2 * CHUNK

    @pl.when(pl.program_id(2) == 0)
    def _():
        pr_ref[...] = jnp.zeros_like(pr_ref)
        pk_ref[...] = jnp.zeros_like(pk_ref)
        pv_ref[...] = jnp.zeros_like(pv_ref)
        plo_ref[...] = jnp.zeros_like(plo_ref)
        s_ref[...] = jnp.zeros_like(s_ref)

    first_row = lax.broadcasted_iota(jnp.int32, (tb, 1), 0) == 0
    r = _shifted(r_ref, pr_ref, mur_ref, first_row)
    k = _shifted(k_ref, pk_ref, muk_ref, first_row)
    v = _shifted(v_ref, pv_ref, muv_ref, first_row)
    lo = _shifted(lo_ref, plo_ref, mul_ref, first_row)

    xw = jnp.tanh(lo[:, LORA_XW:LORA_XW + LANES])
    xa = lo[:, LORA_XA:LORA_XA + LANES]
    xg = jax.nn.sigmoid(lo[:, LORA_XG:LORA_PAD])
    z = w0_ref[...] + _bdot(xw, w2d_ref[...])
    w_logit = -(jnp.maximum(-z, 0.0) + jnp.log(1.0 + jnp.exp(-jnp.abs(z)))) - 0.5
    a_ic = jax.nn.sigmoid(a0_ref[...] + _bdot(xa, a2_ref[...]))
    rs_ref[...] = r
    lw_ref[...] = -jnp.exp(w_logit)
    ks_ref[...] = k * (1.0 + (a_ic - 1.0) * ka_ref[...])
    vs_ref[...] = v
    kks_ref[...] = k * kk_ref[...]
    as_ref[...] = a_ic
    gs_ref[...] = _bdot(xg, g2_ref[...])

    lane = lax.broadcasted_iota(jnp.int32, (CHUNK, LANES), 1)
    head0 = lane < HEAD
    row = lax.broadcasted_iota(jnp.int32, (c2, c2), 0)
    col = lax.broadcasted_iota(jnp.int32, (c2, c2), 1)
    strict = row > col
    incl = row >= col
    same_sub = (row // SUB) == (col // SUB)
    eye = (row == col).astype(F32)
    tri = (lax.broadcasted_iota(jnp.int32, (CHUNK, CHUNK), 0)
           >= lax.broadcasted_iota(jnp.int32, (CHUNK, CHUNK), 1)).astype(F32)
    def head_sum(x):
        s0 = jnp.sum(jnp.where(head0, x, 0.0), axis=-1, keepdims=True)
        s1 = jnp.sum(jnp.where(head0, 0.0, x), axis=-1, keepdims=True)
        return jnp.where(head0, s0, s1)

    def stack(x):
        return jnp.concatenate([jnp.where(head0, x, 0.0), jnp.where(head0, 0.0, x)], axis=0)

    def operands(rc, lw, lwi, kc, vc, kk, ac):
        kkn = kk * lax.rsqrt(jnp.maximum(head_sum(kk * kk), 1e-24))
        b_ = kkn * ac
        tot = lwi[CHUNK - 1:CHUNK, :]
        e_inv = jnp.exp(-lwi)
        e_out = jnp.exp(tot - lwi)
        at = stack(-kkn * jnp.exp(lwi - lw)).astype(BF16)
        rt = stack(rc * jnp.exp(lwi)).astype(BF16)
        bk = jnp.concatenate([stack(b_ * e_inv), stack(kc * e_inv)], axis=0).astype(BF16)
        bkh = jnp.concatenate([stack(b_ * e_out), stack(kc * e_out)], axis=0).astype(BF16)
        return at, rt, bk, bkh, stack(vc).astype(BF16), jnp.exp(tot)

    def finish(y_st, rc, kc, vc, gate, rk, lnw, lnb):
        y = y_st[:CHUNK, :] + y_st[CHUNK:, :]
        mu = head_sum(y) * (1.0 / HEAD)
        yc = y - mu
        var = head_sum(yc * yc) * (1.0 / HEAD)
        yn = yc * lax.rsqrt(var + GN_EPS) * lnw + lnb
        yn = yn + head_sum(rc * kc * rk) * vc
        return (yn * gate).astype(o_ref.dtype)

    def chunk(c, carry):
        sl = pl.ds(pl.multiple_of(c * CHUNK, CHUNK), CHUNK)
        lw = lw_ref[sl, :]
        lwi = jnp.dot(tri, lw, preferred_element_type=F32, precision=lax.Precision.HIGHEST)
        pairs = range(n_pairs)
        ls = [slice(p * LANES, (p + 1) * LANES) for p in pairs]
        ops = [operands(rs_ref[sl, l], lw[:, l], lwi[:, l], ks_ref[sl, l], vs_ref[sl, l], kks_ref[sl, l],
                        as_ref[sl, l]) for l in ls]
        at, rt, bk, bkh, vst, dec = (list(z) for z in zip(*ops))
        a_all = [_bdot_nt(jnp.concatenate([at[p], rt[p]], axis=0), bk[p]) for p in pairs]
        a_ab = [jnp.where(strict, a[:c2, :c2], 0.0) for a in a_all]
        a_ak = [jnp.where(strict, a[:c2, c2:], 0.0) for a in a_all]
        a_r = [jnp.where(jnp.concatenate([incl, incl], axis=1), a[c2:, :], 0.0).astype(BF16) for a in a_all]
        akv = [_bdot(a_ak[p], vst[p]).astype(BF16) for p in pairs]

        d1 = [jnp.where(same_sub, a, 0.0) for a in a_ab]
        l1 = [a_ab[p] - d1[p] for p in pairs]
        d2 = [_bdot(d, d) for d in d1]
        d4 = [_bdot(d, d) for d in d2]
        p12 = [_bdot(eye + d1[p], eye + d2[p]) for p in pairs]
        d8 = [_bdot(d, d) for d in d4]
        p48 = [_bdot(eye + d4[p], eye + d8[p]) for p in pairs]
        t_d = [_bdot(p12[p], p48[p]).astype(BF16) for p in pairs]
        m1 = [_bdot(t_d[p], l1[p]) for p in pairs]
        m2 = [_bdot(m, m) for m in m1]
        q = [_bdot(eye + m1[p], eye + m2[p]) for p in pairs]
        t_inv = [_bdot(q[p], t_d[p]) for p in pairs]
        tx = [_bdot(t_inv[p], jnp.concatenate([at[p], akv[p]], axis=1)) for p in pairs]

        s_bd = [s_ref[p] for p in pairs]
        wr = [_bdot_nt(jnp.concatenate([tx[p][:, :LANES].astype(BF16), rt[p]], axis=0), s_bd[p]) for p in pairs]
        uv = [jnp.concatenate([(wr[p][:c2, :] + tx[p][:, LANES:]).astype(BF16), vst[p]], axis=0) for p in pairs]
        y_st = [wr[p][c2:, :] + _bdot(a_r[p], uv[p]) for p in pairs]
        for p in pairs:
            s_ref[p] = s_bd[p] * dec[p] + _bdot_tn(uv[p], bkh[p])
        for p, l in zip(pairs, ls):
            o_ref[sl, l] = finish(y_st[p], rs_ref[sl, l], ks_ref[sl, l], vs_ref[sl, l], gs_ref[sl, l],
                                  rk_ref[:, l], lnw_ref[:, l], lnb_ref[:, l])
        return carry

    lax.fori_loop(0, tb // CHUNK, chunk, 0)


RWKV_PAIRS_PER_STEP = 8


def _rwkv_group(proj, batch, seq, mu_rkv, mu_lora, w0, w2d, a0, a2, g2, k_k, k_a, r_k, lnx_w, lnx_b, tb=512,
                pairs=RWKV_PAIRS_PER_STEP):
    width = pairs * LANES
    n_g = RWKV_WIDTH // width
    n_tb = seq // tb
    row = lambda b, h, t: b * n_tb + t
    vec = lambda off: pl.BlockSpec((1, width), lambda b, h, t: (0, off + h))
    lora_blk = LORA_COL0 // LORA_PAD
    in_specs = [
        pl.BlockSpec((tb, width), lambda b, h, t: (row(b, h, t), h)),
        pl.BlockSpec((tb, width), lambda b, h, t: (row(b, h, t), n_g + h)),
        pl.BlockSpec((tb, width), lambda b, h, t: (row(b, h, t), 2 * n_g + h)),
        pl.BlockSpec((tb, LORA_PAD), lambda b, h, t: (row(b, h, t), lora_blk)),
        vec(0), vec(n_g), vec(2 * n_g),
        pl.BlockSpec((1, LORA_PAD), lambda b, h, t: (0, 0)),
        vec(0),
        pl.BlockSpec((LANES, width), lambda b, h, t: (0, h)),
        vec(0),
        pl.BlockSpec((LANES, width), lambda b, h, t: (0, h)),
        pl.BlockSpec((LORA_PAD - LORA_XG, width), lambda b, h, t: (0, h)),
        vec(0), vec(0), vec(0), vec(0), vec(0),
    ]
    blk = lambda: pltpu.VMEM((tb, width), F32)
    return pl.pallas_call(
        _rwkv_kernel,
        grid=(batch, n_g, n_tb),
        in_specs=in_specs,
        out_specs=pl.BlockSpec((tb, width), lambda b, h, t: (row(b, h, t), h)),
        out_shape=jax.ShapeDtypeStruct((batch * seq, RWKV_WIDTH), BF16),
        scratch_shapes=[pltpu.VMEM((1, width), F32), pltpu.VMEM((1, width), F32), pltpu.VMEM((1, width), F32),
                        pltpu.VMEM((1, LORA_PAD), F32), pltpu.VMEM((pairs, 2 * CHUNK, LANES), F32),
                        blk(), blk(), blk(), blk(), blk(), blk(), blk()],
        compiler_params=_cparams(("arbitrary", "arbitrary", "arbitrary")),
        name="rwkv7_group",
    )(proj, proj, proj, proj, mu_rkv, mu_rkv, mu_rkv, mu_lora, w0, w2d, a0, a2, g2, k_k, k_a, r_k, lnx_w, lnx_b)


HALO = 16


def _pool_kernel(p_ref, w_ref, sc_ref, o_ref, ext_ref):
    tb = p_ref.shape[0]
    t = pl.program_id(1)

    @pl.when(t == 0)
    def _():
        ext_ref[0:HALO, :] = jnp.zeros((HALO, ext_ref.shape[1]), F32)

    @pl.when(t > 0)
    def _():
        ext_ref[0:HALO, :] = ext_ref[tb:tb + HALO, :]

    ext_ref[HALO:HALO + tb, :] = p_ref[...]
    t_idx = t * tb + lax.broadcasted_iota(jnp.int32, (tb, 1), 0)
    for gi, win in enumerate(POOL_WINDOWS):
        cs = slice(gi * POOL_GROUP_WIDTH, (gi + 1) * POOL_GROUP_WIDTH)
        acc = ext_ref[HALO:HALO + tb, cs]
        for d in range(1, win):
            acc = acc + ext_ref[HALO - d:HALO - d + tb, cs]
        count = jnp.minimum(t_idx + 1, win).astype(F32)
        pooled = acc / count - ext_ref[HALO:HALO + tb, cs]
        mixed = _bdot(pooled, w_ref[gi])
        o_ref[:, cs] = (mixed * sc_ref[:, cs]).astype(o_ref.dtype)


def _pool_group(proj, batch, seq, w_pool_b, pool_scale, tb=512):
    n_tb = seq // tb
    col_blk = POOL_COL0 // POOL_WIDTH
    return pl.pallas_call(
        _pool_kernel,
        grid=(batch, n_tb),
        in_specs=[pl.BlockSpec((tb, POOL_WIDTH), lambda b, t: (b * n_tb + t, col_blk)),
                  pl.BlockSpec(w_pool_b.shape, lambda b, t: (0, 0, 0)),
                  pl.BlockSpec((1, POOL_WIDTH), lambda b, t: (0, 0))],
        out_specs=pl.BlockSpec((tb, POOL_WIDTH), lambda b, t: (b * n_tb + t, 0)),
        out_shape=jax.ShapeDtypeStruct((batch * seq, POOL_WIDTH), BF16),
        scratch_shapes=[pltpu.VMEM((tb + HALO, POOL_WIDTH), F32)],
        compiler_params=_cparams(("arbitrary", "arbitrary")),
        name="pool_group",
    )(proj, w_pool_b, pool_scale)


def _layer_norm(x, w, b, eps):
    mu = jnp.mean(x, axis=-1, keepdims=True)
    xc = x - mu
    var = jnp.mean(xc * xc, axis=-1, keepdims=True)
    return xc * lax.rsqrt(var + eps) * w + b


def _mix_kernel(yr_ref, yp_ref, x_ref, wo_ref, lw_ref, lb_ref, rw_ref, rb_ref,
                h_ref, ri_ref, g_ref, cnt_ref, carry_ref):
    tm = x_ref.shape[0]
    half = yr_ref.shape[1]

    @pl.when(pl.program_id(0) == 0)
    def _():
        carry_ref[...] = jnp.zeros_like(carry_ref)

    mix = (jnp.dot(yr_ref[...], wo_ref[0:half, :], preferred_element_type=F32)
           + jnp.dot(yp_ref[...], wo_ref[half:, :], preferred_element_type=F32))
    h = _layer_norm(DEEPNORM_ALPHA * x_ref[...] + mix, lw_ref[...], lb_ref[...], LN_EPS)
    h_ref[...] = h

    logits = jnp.dot(h, rw_ref[...], preferred_element_type=F32, precision=lax.Precision.HIGHEST) + rb_ref[...]
    lane = lax.broadcasted_iota(jnp.int32, (tm, LANES), 1).astype(F32)
    idxs, vals = [], []
    left = logits
    for _ in range(TOP_K):
        m = jnp.max(left, axis=-1, keepdims=True)
        idx = jnp.min(jnp.where(left == m, lane, float(LANES)), axis=-1, keepdims=True)
        idxs.append(idx)
        vals.append(m)
        left = jnp.where(lane == idx, -jnp.inf, left)
    exps = [jnp.exp(v - vals[0]) for v in vals]
    denom = exps[0] + exps[1] + exps[2] + exps[3]

    onehot = jnp.zeros((tm, LANES), F32)
    for idx in idxs:
        onehot = onehot + (lane == idx).astype(F32)
    tri = (lax.broadcasted_iota(jnp.int32, (tm, tm), 0)
           > lax.broadcasted_iota(jnp.int32, (tm, tm), 1)).astype(BF16)
    before = jnp.dot(tri, onehot.astype(BF16), preferred_element_type=F32) + carry_ref[...]
    carry_ref[...] = carry_ref[...] + jnp.sum(onehot, axis=0, keepdims=True)
    cnt_ref[...] = carry_ref[...]

    ri = jnp.zeros((tm, LANES), F32)
    gt = jnp.zeros((tm, LANES), F32)
    for k in range(TOP_K):
        rank = jnp.sum(jnp.where(lane == idxs[k], before, 0.0), axis=-1, keepdims=True)
        ri = jnp.where(lane == float(k), idxs[k], ri)
        ri = jnp.where(lane == float(TOP_K + k), rank, ri)
        gt = jnp.where(lane == float(k), exps[k] / denom, gt)
    ri_ref[...] = ri.astype(jnp.int32)
    g_ref[...] = gt


def _mix_ln_router(y_rwkv, y_pool, x2d, w_out_b, ln_w, ln_b, router_w_p, router_b_p, tm=512):
    n, d = x2d.shape
    half = y_rwkv.shape[1]
    const = lambda shape: pl.BlockSpec(shape, lambda i: (0, 0))
    return pl.pallas_call(
        _mix_kernel,
        grid=(n // tm,),
        in_specs=[pl.BlockSpec((tm, half), lambda i: (i, 0)),
                  pl.BlockSpec((tm, y_pool.shape[1]), lambda i: (i, 0)),
                  pl.BlockSpec((tm, d), lambda i: (i, 0)),
                  const(w_out_b.shape), const((1, d)), const((1, d)),
                  const(router_w_p.shape), const((1, LANES))],
        out_specs=[pl.BlockSpec((tm, d), lambda i: (i, 0)),
                   pl.BlockSpec((tm, LANES), lambda i: (i, 0)),
                   pl.BlockSpec((tm, LANES), lambda i: (i, 0)),
                   const((1, LANES))],
        out_shape=[jax.ShapeDtypeStruct((n, d), F32),
                   jax.ShapeDtypeStruct((n, LANES), jnp.int32),
                   jax.ShapeDtypeStruct((n, LANES), F32),
                   jax.ShapeDtypeStruct((1, LANES), F32)],
        scratch_shapes=[pltpu.VMEM((1, LANES), F32)],
        compiler_params=_cparams(("arbitrary",)),
        name="mix_ln_router",
    )(y_rwkv, y_pool, x2d, w_out_b, ln_w, ln_b, router_w_p, router_b_p)


MOE_BM = 512


MXU_DIM = 256


def _split_w1_kernel(w_ref, g_ref, l_ref):
    half = MXU_DIM // 2
    src = lax.broadcasted_iota(jnp.int32, (MXU_DIM, MXU_DIM), 0)
    dst = lax.broadcasted_iota(jnp.int32, (MXU_DIM, MXU_DIM), 1)
    perm = (src == jnp.where(dst < half, 2 * dst, 2 * (dst - half) + 1)).astype(BF16)
    for c in range(w_ref.shape[2] // MXU_DIM):
        w = w_ref[0, :, c * MXU_DIM:(c + 1) * MXU_DIM].astype(BF16)
        o = jnp.dot(w, perm, preferred_element_type=F32)
        g_ref[0, :, c * half:(c + 1) * half] = o[:, :half].astype(BF16)
        l_ref[0, :, c * half:(c + 1) * half] = o[:, half:].astype(BF16)


def _split_w1(w1, tr=512, tc=2048):
    n_exp, d, f2 = w1.shape
    out = jax.ShapeDtypeStruct((n_exp, d, f2 // 2), BF16)
    return pl.pallas_call(
        _split_w1_kernel,
        grid=(n_exp, d // tr, f2 // tc),
        in_specs=[pl.BlockSpec((1, tr, tc), lambda e, i, j: (e, i, j))],
        out_specs=[pl.BlockSpec((1, tr, tc // 2), lambda e, i, j: (e, i, j)),
                   pl.BlockSpec((1, tr, tc // 2), lambda e, i, j: (e, i, j))],
        out_shape=[out, out],
        compiler_params=_cparams(("arbitrary", "arbitrary", "arbitrary")),
        name="split_w1",
    )(w1)


def _cast_kernel(w_ref, o_ref):
    o_ref[...] = w_ref[...].astype(o_ref.dtype)


def _cast_bf16(w, tr=512):
    n_exp, r, c = w.shape
    return pl.pallas_call(
        _cast_kernel,
        grid=(n_exp, r // tr),
        in_specs=[pl.BlockSpec((1, tr, c), lambda e, i: (e, i, 0))],
        out_specs=pl.BlockSpec((1, tr, c), lambda e, i: (e, i, 0)),
        out_shape=jax.ShapeDtypeStruct(w.shape, BF16),
        compiler_params=_cparams(("arbitrary", "arbitrary")),
        name="cast_w2",
    )(w)


def _gather_kernel(tok_ref, nu_ref, h_hbm, o_ref, buf, sem):
    m = pl.program_id(0)
    bm = o_ref.shape[0]

    @pl.when(m < nu_ref[0])
    def _():
        def issue(i, carry):
            tok = tok_ref[m * bm + i]
            pltpu.make_async_copy(h_hbm.at[pl.ds(tok, 1), :], buf.at[pl.ds(i, 1), :], sem).start()
            return carry

        lax.fori_loop(0, bm, issue, 0)
        pltpu.make_async_copy(h_hbm.at[pl.ds(0, bm), :], buf, sem).wait()
        o_ref[...] = buf[...].astype(o_ref.dtype)

    @pl.when(m >= nu_ref[0])
    def _():
        o_ref[...] = jnp.zeros_like(o_ref)


def _dispatch(h, slot_token, n_used, n_blocks, bm=MOE_BM):
    d = h.shape[1]
    return pl.pallas_call(
        _gather_kernel,
        grid_spec=pltpu.PrefetchScalarGridSpec(
            num_scalar_prefetch=2,
            grid=(n_blocks,),
            in_specs=[pl.BlockSpec(memory_space=pl.ANY)],
            out_specs=pl.BlockSpec((bm, d), lambda m, tok, nu: (m, 0)),
            scratch_shapes=[pltpu.VMEM((bm, d), F32), pltpu.SemaphoreType.DMA(())]),
        out_shape=jax.ShapeDtypeStruct((n_blocks * bm, d), BF16),
        compiler_params=_cparams(("arbitrary",)),
        name="moe_dispatch",
    )(slot_token, n_used, h)


def _ffn_kernel(be_ref, nu_ref, x_ref, w1g_ref, w1l_ref, b1g_ref, b1l_ref, w2_ref, b2_ref, o_ref):
    m = pl.program_id(0)
    f = pl.program_id(1)

    @pl.when(m < nu_ref[0])
    def _():
        x = x_ref[...]
        hg = jnp.dot(x, w1g_ref[0], preferred_element_type=F32) + b1g_ref[0]
        hl = jnp.dot(x, w1l_ref[0], preferred_element_type=F32) + b1l_ref[0]
        x_glu = jnp.minimum(hg, SWIGLU_LIMIT)
        x_lin = jnp.clip(hl, -SWIGLU_LIMIT, SWIGLU_LIMIT)
        act = x_glu * jax.nn.sigmoid(SWIGLU_ALPHA * x_glu) * (x_lin + 1.0)
        part = jnp.dot(act.astype(BF16), w2_ref[0], preferred_element_type=F32)

        @pl.when(f == 0)
        def _():
            o_ref[...] = part + b2_ref[0]

        @pl.when(f > 0)
        def _():
            o_ref[...] += part

    @pl.when((m >= nu_ref[0]) & (f == 0))
    def _():
        o_ref[...] = jnp.zeros_like(o_ref)


def _experts(x_slots, block_expert, n_used, w1g, w1l, b1g, b1l, w2, b2, bm=MOE_BM, tf=512):
    n_slots, d = x_slots.shape
    n_blocks = n_slots // bm
    ff = w1g.shape[2]
    n_f = ff // tf

    def m_eff(m, nu):
        return jnp.minimum(m, nu[0] - 1)

    def f_eff(m, f, nu):
        return jnp.where(m < nu[0], f, n_f - 1)

    return pl.pallas_call(
        _ffn_kernel,
        grid_spec=pltpu.PrefetchScalarGridSpec(
            num_scalar_prefetch=2,
            grid=(n_blocks, n_f),
            in_specs=[
                pl.BlockSpec((bm, d), lambda m, f, be, nu: (m_eff(m, nu), 0)),
                pl.BlockSpec((1, d, tf), lambda m, f, be, nu: (be[m], 0, f_eff(m, f, nu))),
                pl.BlockSpec((1, d, tf), lambda m, f, be, nu: (be[m], 0, f_eff(m, f, nu))),
                pl.BlockSpec((1, 1, tf), lambda m, f, be, nu: (be[m], 0, f_eff(m, f, nu))),
                pl.BlockSpec((1, 1, tf), lambda m, f, be, nu: (be[m], 0, f_eff(m, f, nu))),
                pl.BlockSpec((1, tf, d), lambda m, f, be, nu: (be[m], f_eff(m, f, nu), 0)),
                pl.BlockSpec((1, 1, d), lambda m, f, be, nu: (be[m], 0, 0)),
            ],
            out_specs=pl.BlockSpec((bm, d), lambda m, f, be, nu: (m, 0))),
        out_shape=jax.ShapeDtypeStruct((n_slots, d), F32),
        compiler_params=_cparams(("arbitrary", "arbitrary")),
        name="moe_experts",
    )(block_expert, n_used, x_slots, w1g, w1l, b1g, b1l, w2, b2)


def _combine_kernel(pos_ref, h_ref, g_ref, lw_ref, lb_ref, y_hbm, o_ref, buf, sem):
    i = pl.program_id(0)
    tm = h_ref.shape[0]

    def issue(n, carry):
        for k in range(TOP_K):
            p = pos_ref[(i * tm + n) * TOP_K + k]
            pltpu.make_async_copy(y_hbm.at[pl.ds(p, 1), :], buf.at[pl.ds(k * tm + n, 1), :], sem).start()
        return carry

    lax.fori_loop(0, tm, issue, 0)
    pltpu.make_async_copy(y_hbm.at[pl.ds(0, TOP_K * tm), :], buf, sem).wait()
    g = g_ref[...]
    ffn = g[:, 0:1] * buf[0:tm, :]
    for k in range(1, TOP_K):
        ffn = ffn + g[:, k:k + 1] * buf[k * tm:(k + 1) * tm, :]
    o_ref[...] = _layer_norm(DEEPNORM_ALPHA * h_ref[...] + ffn, lw_ref[...], lb_ref[...], LN_EPS)


def _combine(pos_flat, h, gates, ln_w, ln_b, y_slots, tm=256):
    n, d = h.shape
    return pl.pallas_call(
        _combine_kernel,
        grid_spec=pltpu.PrefetchScalarGridSpec(
            num_scalar_prefetch=1,
            grid=(n // tm,),
            in_specs=[pl.BlockSpec((tm, d), lambda i, pos: (i, 0)),
                      pl.BlockSpec((tm, LANES), lambda i, pos: (i, 0)),
                      pl.BlockSpec((1, d), lambda i, pos: (0, 0)),
                      pl.BlockSpec((1, d), lambda i, pos: (0, 0)),
                      pl.BlockSpec(memory_space=pl.ANY)],
            out_specs=pl.BlockSpec((tm, d), lambda i, pos: (i, 0)),
            scratch_shapes=[pltpu.VMEM((TOP_K * tm, d), F32), pltpu.SemaphoreType.DMA(())]),
        out_shape=jax.ShapeDtypeStruct((n, d), F32),
        compiler_params=_cparams(("arbitrary",)),
        name="moe_combine_ln",
    )(pos_flat, h, gates, ln_w, ln_b, y_slots)


def _moe(h, route_i, gates, counts_f, w1, b1, w2, b2, ln_w, ln_b):
    n_tok, d = h.shape
    n_exp = w1.shape[0]
    n_asg = n_tok * TOP_K
    n_blocks = -(-n_asg // MOE_BM) + n_exp

    counts = counts_f[0, :n_exp].astype(jnp.int32)
    padded = (counts + MOE_BM - 1) // MOE_BM * MOE_BM
    pend = jnp.cumsum(padded)
    pstart = pend - padded
    pos = pstart[route_i[:, 0:TOP_K]] + route_i[:, TOP_K:2 * TOP_K]
    pos_flat = pos.reshape(n_asg)
    token_of = jnp.arange(n_asg, dtype=jnp.int32) // TOP_K
    slot_token = jnp.zeros((n_blocks * MOE_BM,), jnp.int32).at[pos_flat].set(token_of)
    n_used = (pend[-1] // MOE_BM).astype(jnp.int32).reshape(1)
    blk = jnp.minimum(jnp.arange(n_blocks, dtype=jnp.int32), n_used[0] - 1)
    block_expert = jnp.minimum(jnp.searchsorted(pend, blk * MOE_BM, side='right'), n_exp - 1).astype(jnp.int32)

    w1g, w1l = _split_w1(w1)
    b1g = b1[:, None, 0::2]
    b1l = b1[:, None, 1::2]
    w2b = _cast_bf16(w2)

    x_slots = _dispatch(h, slot_token, n_used, n_blocks)
    y_slots = _experts(x_slots, block_expert, n_used, w1g, w1l, b1g, b1l, w2b, b2[:, None, :])
    return _combine(pos_flat, h, gates, ln_w, ln_b, y_slots)


def _pad_to(a, axis, size):
    pad = [(0, 0)] * a.ndim
    pad[axis] = (0, size - a.shape[axis])
    return jnp.pad(a, pad)


def kernel(x, w_in, mu_shift, w0, w2_decay, a0, a2_iclr, g2_gate, k_k, k_a, r_k, lnx_w, lnx_b, w_pool, pool_scale,
           w_out, ln1_w, ln1_b, router_w, router_b, w1_exp, b1_exp, w2_exp, b2_exp, ln2_w, ln2_b):
    batch, seq, d = x.shape
    assert w_in.shape[0] == 1, "one layer"
    x2d = x.reshape(batch * seq, d)
    c_xw = RKV_COLS
    c_xa = c_xw + DECAY_LORA
    c_xg = c_xa + ICLR_LORA
    c_pool = c_xg + GATE_LORA

    def lora_layout(a):
        return jnp.concatenate([_pad_to(a[:, c_xw:c_xa], 1, LORA_XA - LORA_XW),
                                _pad_to(a[:, c_xa:c_xg], 1, LORA_XG - LORA_XA),
                                _pad_to(a[:, c_xg:c_pool], 1, LORA_PAD - LORA_XG)], axis=1)

    wi = w_in[0]
    w_p = jnp.concatenate([wi[:, :RKV_COLS], wi[:, c_pool:], lora_layout(wi)], axis=1).astype(BF16)
    mu = mu_shift[0][None, :]
    row = lambda a: a.reshape(1, -1)

    proj = _in_proj(x2d, w_p)
    y_rwkv = _rwkv_group(
        proj, batch, seq, mu[:, :RKV_COLS], lora_layout(mu), row(w0[0]),
        _pad_to(w2_decay[0], 0, LANES).astype(BF16), row(a0[0]), _pad_to(a2_iclr[0], 0, LANES).astype(BF16),
        _pad_to(g2_gate[0], 0, LORA_PAD - LORA_XG).astype(BF16), row(k_k[0]), row(k_a[0]), row(r_k[0]),
        row(lnx_w[0]), row(lnx_b[0]))
    y_pool = _pool_group(proj, batch, seq, w_pool[0].astype(BF16), row(pool_scale[0]))

    n_exp = router_w.shape[2]
    router_w_p = _pad_to(router_w[0], 1, LANES)
    router_b_p = jnp.concatenate([router_b[0], jnp.full((LANES - n_exp,), -1e30, F32)])[None, :]
    h1, route_i, gates, counts = _mix_ln_router(y_rwkv, y_pool, x2d, w_out[0].astype(BF16), row(ln1_w[0]),
                                                row(ln1_b[0]), router_w_p, router_b_p)
    out = _moe(h1, route_i, gates, counts, w1_exp[0], b1_exp[0], w2_exp[0], b2_exp[0], row(ln2_w[0]), row(ln2_b[0]))
    return out.reshape(batch, seq, d)
```

```python
import functools

import jax
import jax.numpy as jnp
from jax import lax
from jax.experimental import pallas as pl
from jax.experimental.pallas import tpu as pltpu

F32 = jnp.float32
BF16 = jnp.bfloat16

RWKV_WIDTH = 1024
HEAD = 64
DECAY_LORA = 64
ICLR_LORA = 64
GATE_LORA = 160
POOL_WINDOWS = (2, 4, 8, 16)
POOL_GROUP_WIDTH = 256
TOP_K = 4
SWIGLU_LIMIT = 7.0
SWIGLU_ALPHA = 1.702
LN_EPS = 1e-5
GN_EPS = 64e-5
DEEPNORM_ALPHA = 2.0 ** 0.25

LANES = 128
LORA_PAD = 512
LORA_XW, LORA_XA, LORA_XG = 0, 128, 256
RKV_COLS = 3 * RWKV_WIDTH
POOL_WIDTH = len(POOL_WINDOWS) * POOL_GROUP_WIDTH
POOL_COL0 = RKV_COLS
LORA_COL0 = POOL_COL0 + POOL_WIDTH
PROJ_COLS = LORA_COL0 + LORA_PAD
CHUNK = 64
SUB = 16
VMEM_LIMIT = 56 * 1024 * 1024


def _cparams(sem):
    return pltpu.CompilerParams(dimension_semantics=sem, vmem_limit_bytes=VMEM_LIMIT)


def _bdot(a, b):
    return jnp.dot(a.astype(BF16), b.astype(BF16), preferred_element_type=F32)


def _bdot_nt(a, b):
    return lax.dot_general(a.astype(BF16), b.astype(BF16), (((1,), (1,)), ((), ())),
                           preferred_element_type=F32)


def _bdot_tn(a, b):
    return lax.dot_general(a.astype(BF16), b.astype(BF16), (((0,), (0,)), ((), ())),
                           preferred_element_type=F32)


def _in_proj_kernel(x_ref, w_ref, o_ref, xb_ref):
    @pl.when(pl.program_id(1) == 0)
    def _():
        xb_ref[...] = x_ref[...].astype(BF16)

    o_ref[...] = jnp.dot(xb_ref[...], w_ref[...], preferred_element_type=F32)


def _in_proj(x2d, w_p, tm=512, tn=768):
    n, d = x2d.shape
    cols = w_p.shape[1]
    return pl.pallas_call(
        _in_proj_kernel,
        grid=(n // tm, cols // tn),
        in_specs=[pl.BlockSpec((tm, d), lambda i, j: (i, 0)),
                  pl.BlockSpec((d, tn), lambda i, j: (0, j))],
        out_specs=pl.BlockSpec((tm, tn), lambda i, j: (i, j)),
        out_shape=jax.ShapeDtypeStruct((n, cols), F32),
        scratch_shapes=[pltpu.VMEM((tm, d), BF16)],
        compiler_params=_cparams(("arbitrary", "arbitrary")),
        name="in_proj",
    )(x2d, w_p)


def _shifted(x_ref, prev_ref, mu_ref, first_row):
    x = x_ref[...]
    xp = jnp.where(first_row, prev_ref[...], pltpu.roll(x, 1, axis=0))
    prev_ref[...] = x_ref[x.shape[0] - 1:x.shape[0], :]
    return x + (xp - x) * mu_ref[...]


def _rwkv_kernel(r_ref, k_ref, v_ref, lo_ref, mur_ref, muk_ref, muv_ref, mul_ref,
                 w0_ref, w2d_ref, a0_ref, a2_ref, g2_ref, kk_ref, ka_ref, rk_ref, lnw_ref, lnb_ref,
                 o_ref,
                 pr_ref, pk_ref, pv_ref, plo_ref, s_ref,
                 rs_ref, lw_ref, ks_ref, vs_ref, kks_ref, as_ref, gs_ref):
    tb, width = r_ref.shape
    n_pairs = width // LANES
    c2 = 2 * CHUNK

    @pl.when(pl.program_id(2) == 0)
    def _():
        pr_ref[...] = jnp.zeros_like(pr_ref)
        pk_ref[...] = jnp.zeros_like(pk_ref)
        pv_ref[...] = jnp.zeros_like(pv_ref)
        plo_ref[...] = jnp.zeros_like(plo_ref)
        s_ref[...] = jnp.zeros_like(s_ref)

    first_row = lax.broadcasted_iota(jnp.int32, (tb, 1), 0) == 0
    r = _shifted(r_ref, pr_ref, mur_ref, first_row)
    k = _shifted(k_ref, pk_ref, muk_ref, first_row)
    v = _shifted(v_ref, pv_ref, muv_ref, first_row)
    lo = _shifted(lo_ref, plo_ref, mul_ref, first_row)

    xw = jnp.tanh(lo[:, LORA_XW:LORA_XW + LANES])
    xa = lo[:, LORA_XA:LORA_XA + LANES]
    xg = jax.nn.sigmoid(lo[:, LORA_XG:LORA_PAD])
    z = w0_ref[...] + _bdot(xw, w2d_ref[...])
    w_logit = -(jnp.maximum(-z, 0.0) + jnp.log(1.0 + jnp.exp(-jnp.abs(z)))) - 0.5
    a_ic = jax.nn.sigmoid(a0_ref[...] + _bdot(xa, a2_ref[...]))
    rs_ref[...] = r
    lw_ref[...] = -jnp.exp(w_logit)
    ks_ref[...] = k * (1.0 + (a_ic - 1.0) * ka_ref[...])
    vs_ref[...] = v
    kks_ref[...] = k * kk_ref[...]
    as_ref[...] = a_ic
    gs_ref[...] = _bdot(xg, g2_ref[...])

    lane = lax.broadcasted_iota(jnp.int32, (CHUNK, LANES), 1)
    head0 = lane < HEAD
    row = lax.broadcasted_iota(jnp.int32, (c2, c2), 0)
    col = lax.broadcasted_iota(jnp.int32, (c2, c2), 1)
    strict = row > col
    incl = row >= col
    same_sub = (row // SUB) == (col // SUB)
    eye = (row == col).astype(F32)
    tri = (lax.broadcasted_iota(jnp.int32, (CHUNK, CHUNK), 0)
           >= lax.broadcasted_iota(jnp.int32, (CHUNK, CHUNK), 1)).astype(F32)
    def head_sum(x):
        s0 = jnp.sum(jnp.where(head0, x, 0.0), axis=-1, keepdims=True)
        s1 = jnp.sum(jnp.where(head0, 0.0, x), axis=-1, keepdims=True)
        return jnp.where(head0, s0, s1)

    def stack(x):
        return jnp.concatenate([jnp.where(head0, x, 0.0), jnp.where(head0, 0.0, x)], axis=0)

    def operands(rc, lw, lwi, kc, vc, kk, ac):
        kkn = kk * lax.rsqrt(jnp.maximum(head_sum(kk * kk), 1e-24))
        b_ = kkn * ac
        tot = lwi[CHUNK - 1:CHUNK, :]
        e_inv = jnp.exp(-lwi)
        e_out = jnp.exp(tot - lwi)
        at = stack(-kkn * jnp.exp(lwi - lw)).astype(BF16)
        rt = stack(rc * jnp.exp(lwi)).astype(BF16)
        bk = jnp.concatenate([stack(b_ * e_inv), stack(kc * e_inv)], axis=0).astype(BF16)
        bkh = jnp.concatenate([stack(b_ * e_out), stack(kc * e_out)], axis=0).astype(BF16)
        return at, rt, bk, bkh, stack(vc).astype(BF16), jnp.exp(tot)

    def finish(y_st, rc, kc, vc, gate, rk, lnw, lnb):
        y = y_st[:CHUNK, :] + y_st[CHUNK:, :]
        mu = head_sum(y) * (1.0 / HEAD)
        yc = y - mu
        var = head_sum(yc * yc) * (1.0 / HEAD)
        yn = yc * lax.rsqrt(var + GN_EPS) * lnw + lnb
        yn = yn + head_sum(rc * kc * rk) * vc
        return (yn * gate).astype(o_ref.dtype)

    def chunk(c, carry):
        sl = pl.ds(pl.multiple_of(c * CHUNK, CHUNK), CHUNK)
        lw = lw_ref[sl, :]
        lwi = jnp.dot(tri, lw, preferred_element_type=F32, precision=lax.Precision.HIGHEST)
        pairs = range(n_pairs)
        ls = [slice(p * LANES, (p + 1) * LANES) for p in pairs]
        ops = [operands(rs_ref[sl, l], lw[:, l], lwi[:, l], ks_ref[sl, l], vs_ref[sl, l], kks_ref[sl, l],
                        as_ref[sl, l]) for l in ls]
        at, rt, bk, bkh, vst, dec = (list(z) for z in zip(*ops))
        a_all = [_bdot_nt(jnp.concatenate([at[p], rt[p]], axis=0), bk[p]) for p in pairs]
        a_ab = [jnp.where(strict, a[:c2, :c2], 0.0) for a in a_all]
        a_ak = [jnp.where(strict, a[:c2, c2:], 0.0) for a in a_all]
        a_r = [jnp.where(jnp.concatenate([incl, incl], axis=1), a[c2:, :], 0.0).astype(BF16) for a in a_all]
        akv = [_bdot(a_ak[p], vst[p]).astype(BF16) for p in pairs]

        d1 = [jnp.where(same_sub, a, 0.0) for a in a_ab]
        l1 = [a_ab[p] - d1[p] for p in pairs]
        d2 = [_bdot(d, d) for d in d1]
        d4 = [_bdot(d, d) for d in d2]
        p12 = [_bdot(eye + d1[p], eye + d2[p]) for p in pairs]
        d8 = [_bdot(d, d) for d in d4]
        p48 = [_bdot(eye + d4[p], eye + d8[p]) for p in pairs]
        t_d = [_bdot(p12[p], p48[p]).astype(BF16) for p in pairs]
        m1 = [_bdot(t_d[p], l1[p]) for p in pairs]
        m2 = [_bdot(m, m) for m in m1]
        q = [_bdot(eye + m1[p], eye + m2[p]) for p in pairs]
        t_inv = [_bdot(q[p], t_d[p]) for p in pairs]
        tx = [_bdot(t_inv[p], jnp.concatenate([at[p], akv[p]], axis=1)) for p in pairs]

        s_bd = [s_ref[p] for p in pairs]
        wr = [_bdot_nt(jnp.concatenate([tx[p][:, :LANES].astype(BF16), rt[p]], axis=0), s_bd[p]) for p in pairs]
        uv = [jnp.concatenate([(wr[p][:c2, :] + tx[p][:, LANES:]).astype(BF16), vst[p]], axis=0) for p in pairs]
        y_st = [wr[p][c2:, :] + _bdot(a_r[p], uv[p]) for p in pairs]
        for p in pairs:
            s_ref[p] = s_bd[p] * dec[p] + _bdot_tn(uv[p], bkh[p])
        for p, l in zip(pairs, ls):
            o_ref[sl, l] = finish(y_st[p], rs_ref[sl, l], ks_ref[sl, l], vs_ref[sl, l], gs_ref[sl, l],
                                  rk_ref[:, l], lnw_ref[:, l], lnb_ref[:, l])
        return carry

    lax.fori_loop(0, tb // CHUNK, chunk, 0)


RWKV_PAIRS_PER_STEP = 8


def _rwkv_group(proj, batch, seq, mu_rkv, mu_lora, w0, w2d, a0, a2, g2, k_k, k_a, r_k, lnx_w, lnx_b, tb=512,
                pairs=RWKV_PAIRS_PER_STEP):
    width = pairs * LANES
    n_g = RWKV_WIDTH // width
    n_tb = seq // tb
    row = lambda b, h, t: b * n_tb + t
    vec = lambda off: pl.BlockSpec((1, width), lambda b, h, t: (0, off + h))
    lora_blk = LORA_COL0 // LORA_PAD
    in_specs = [
        pl.BlockSpec((tb, width), lambda b, h, t: (row(b, h, t), h)),
        pl.BlockSpec((tb, width), lambda b, h, t: (row(b, h, t), n_g + h)),
        pl.BlockSpec((tb, width), lambda b, h, t: (row(b, h, t), 2 * n_g + h)),
        pl.BlockSpec((tb, LORA_PAD), lambda b, h, t: (row(b, h, t), lora_blk)),
        vec(0), vec(n_g), vec(2 * n_g),
        pl.BlockSpec((1, LORA_PAD), lambda b, h, t: (0, 0)),
        vec(0),
        pl.BlockSpec((LANES, width), lambda b, h, t: (0, h)),
        vec(0),
        pl.BlockSpec((LANES, width), lambda b, h, t: (0, h)),
        pl.BlockSpec((LORA_PAD - LORA_XG, width), lambda b, h, t: (0, h)),
        vec(0), vec(0), vec(0), vec(0), vec(0),
    ]
    blk = lambda: pltpu.VMEM((tb, width), F32)
    return pl.pallas_call(
        _rwkv_kernel,
        grid=(batch, n_g, n_tb),
        in_specs=in_specs,
        out_specs=pl.BlockSpec((tb, width), lambda b, h, t: (row(b, h, t), h)),
        out_shape=jax.ShapeDtypeStruct((batch * seq, RWKV_WIDTH), BF16),
        scratch_shapes=[pltpu.VMEM((1, width), F32), pltpu.VMEM((1, width), F32), pltpu.VMEM((1, width), F32),
                        pltpu.VMEM((1, LORA_PAD), F32), pltpu.VMEM((pairs, 2 * CHUNK, LANES), F32),
                        blk(), blk(), blk(), blk(), blk(), blk(), blk()],
        compiler_params=_cparams(("arbitrary", "arbitrary", "arbitrary")),
        name="rwkv7_group",
    )(proj, proj, proj, proj, mu_rkv, mu_rkv, mu_rkv, mu_lora, w0, w2d, a0, a2, g2, k_k, k_a, r_k, lnx_w, lnx_b)


HALO = 16


def _pool_kernel(p_ref, w_ref, sc_ref, o_ref, ext_ref):
    tb = p_ref.shape[0]
    t = pl.program_id(1)

    @pl.when(t == 0)
    def _():
        ext_ref[0:HALO, :] = jnp.zeros((HALO, ext_ref.shape[1]), F32)

    @pl.when(t > 0)
    def _():
        ext_ref[0:HALO, :] = ext_ref[tb:tb + HALO, :]

    ext_ref[HALO:HALO + tb, :] = p_ref[...]
    t_idx = t * tb + lax.broadcasted_iota(jnp.int32, (tb, 1), 0)
    for gi, win in enumerate(POOL_WINDOWS):
        cs = slice(gi * POOL_GROUP_WIDTH, (gi + 1) * POOL_GROUP_WIDTH)
        acc = ext_ref[HALO:HALO + tb, cs]
        for d in range(1, win):
            acc = acc + ext_ref[HALO - d:HALO - d + tb, cs]
        count = jnp.minimum(t_idx + 1, win).astype(F32)
        pooled = acc / count - ext_ref[HALO:HALO + tb, cs]
        mixed = _bdot(pooled, w_ref[gi])
        o_ref[:, cs] = (mixed * sc_ref[:, cs]).astype(o_ref.dtype)


def _pool_group(proj, batch, seq, w_pool_b, pool_scale, tb=512):
    n_tb = seq // tb
    col_blk = POOL_COL0 // POOL_WIDTH
    return pl.pallas_call(
        _pool_kernel,
        grid=(batch, n_tb),
        in_specs=[pl.BlockSpec((tb, POOL_WIDTH), lambda b, t: (b * n_tb + t, col_blk)),
                  pl.BlockSpec(w_pool_b.shape, lambda b, t: (0, 0, 0)),
                  pl.BlockSpec((1, POOL_WIDTH), lambda b, t: (0, 0))],
        out_specs=pl.BlockSpec((tb, POOL_WIDTH), lambda b, t: (b * n_tb + t, 0)),
        out_shape=jax.ShapeDtypeStruct((batch * seq, POOL_WIDTH), BF16),
        scratch_shapes=[pltpu.VMEM((tb + HALO, POOL_WIDTH), F32)],
        compiler_params=_cparams(("arbitrary", "arbitrary")),
        name="pool_group",
    )(proj, w_pool_b, pool_scale)


def _layer_norm(x, w, b, eps):
    mu = jnp.mean(x, axis=-1, keepdims=True)
    xc = x - mu
    var = jnp.mean(xc * xc, axis=-1, keepdims=True)
    return xc * lax.rsqrt(var + eps) * w + b


def _mix_kernel(yr_ref, yp_ref, x_ref, wo_ref, lw_ref, lb_ref, rwh_ref, rwl_ref, rb_ref,
                h_ref, ri_ref, g_ref, cnt_ref, carry_ref):
    tm = x_ref.shape[0]
    half = yr_ref.shape[1]

    @pl.when(pl.program_id(0) == 0)
    def _():
        carry_ref[...] = jnp.zeros_like(carry_ref)

    mix = (jnp.dot(yr_ref[...], wo_ref[0:half, :], preferred_element_type=F32)
           + jnp.dot(yp_ref[...], wo_ref[half:, :], preferred_element_type=F32))
    h = _layer_norm(DEEPNORM_ALPHA * x_ref[...] + mix, lw_ref[...], lb_ref[...], LN_EPS)
    h_ref[...] = h

    h_hi = h.astype(BF16)
    h_lo = (h - h_hi.astype(F32)).astype(BF16)
    logits = (jnp.dot(h_hi, rwh_ref[...], preferred_element_type=F32)
              + jnp.dot(h_lo, rwh_ref[...], preferred_element_type=F32)
              + jnp.dot(h_hi, rwl_ref[...], preferred_element_type=F32)) + rb_ref[...]
    lane = lax.broadcasted_iota(jnp.int32, (tm, LANES), 1).astype(F32)
    idxs, vals = [], []
    left = logits
    for _ in range(TOP_K):
        m = jnp.max(left, axis=-1, keepdims=True)
        idx = jnp.min(jnp.where(left == m, lane, float(LANES)), axis=-1, keepdims=True)
        idxs.append(idx)
        vals.append(m)
        left = jnp.where(lane == idx, -jnp.inf, left)
    exps = [jnp.exp(v - vals[0]) for v in vals]
    denom = exps[0] + exps[1] + exps[2] + exps[3]

    onehot = jnp.zeros((tm, LANES), F32)
    for idx in idxs:
        onehot = onehot + (lane == idx).astype(F32)
    tri = (lax.broadcasted_iota(jnp.int32, (tm, tm), 0)
           > lax.broadcasted_iota(jnp.int32, (tm, tm), 1)).astype(BF16)
    before = jnp.dot(tri, onehot.astype(BF16), preferred_element_type=F32) + carry_ref[...]
    carry_ref[...] = carry_ref[...] + jnp.sum(onehot, axis=0, keepdims=True)
    cnt_ref[...] = carry_ref[...]

    ri = jnp.zeros((tm, LANES), F32)
    gt = jnp.zeros((tm, LANES), F32)
    for k in range(TOP_K):
        rank = jnp.sum(jnp.where(lane == idxs[k], before, 0.0), axis=-1, keepdims=True)
        ri = jnp.where(lane == float(k), idxs[k], ri)
        ri = jnp.where(lane == float(TOP_K + k), rank, ri)
        gt = jnp.where(lane == float(k), exps[k] / denom, gt)
    ri_ref[...] = ri.astype(jnp.int32)
    g_ref[...] = gt


def _mix_ln_router(y_rwkv, y_pool, x2d, w_out_b, ln_w, ln_b, router_w_p, router_b_p, tm=512):
    n, d = x2d.shape
    rw_hi = router_w_p.astype(BF16)
    rw_lo = (router_w_p - rw_hi.astype(F32)).astype(BF16)
    half = y_rwkv.shape[1]
    const = lambda shape: pl.BlockSpec(shape, lambda i: (0, 0))
    return pl.pallas_call(
        _mix_kernel,
        grid=(n // tm,),
        in_specs=[pl.BlockSpec((tm, half), lambda i: (i, 0)),
                  pl.BlockSpec((tm, y_pool.shape[1]), lambda i: (i, 0)),
                  pl.BlockSpec((tm, d), lambda i: (i, 0)),
                  const(w_out_b.shape), const((1, d)), const((1, d)),
                  const(router_w_p.shape), const(router_w_p.shape), const((1, LANES))],
        out_specs=[pl.BlockSpec((tm, d), lambda i: (i, 0)),
                   pl.BlockSpec((tm, LANES), lambda i: (i, 0)),
                   pl.BlockSpec((tm, LANES), lambda i: (i, 0)),
                   const((1, LANES))],
        out_shape=[jax.ShapeDtypeStruct((n, d), F32),
                   jax.ShapeDtypeStruct((n, LANES), jnp.int32),
                   jax.ShapeDtypeStruct((n, LANES), F32),
                   jax.ShapeDtypeStruct((1, LANES), F32)],
        scratch_shapes=[pltpu.VMEM((1, LANES), F32)],
        compiler_params=_cparams(("arbitrary",)),
        name="mix_ln_router",
    )(y_rwkv, y_pool, x2d, w_out_b, ln_w, ln_b, rw_hi, rw_lo, router_b_p)


MOE_BM = 512
GATHER_UNROLL = 8


MXU_DIM = 256
OUT_CHUNK = 512


def _split_w1_kernel(w_ref, g_ref, l_ref):
    half = MXU_DIM // 2
    src = lax.broadcasted_iota(jnp.int32, (MXU_DIM, MXU_DIM), 0)
    dst = lax.broadcasted_iota(jnp.int32, (MXU_DIM, MXU_DIM), 1)
    perm = (src == jnp.where(dst < half, 2 * dst, 2 * (dst - half) + 1)).astype(BF16)
    for c in range(w_ref.shape[2] // MXU_DIM):
        w = w_ref[0, :, c * MXU_DIM:(c + 1) * MXU_DIM].astype(BF16)
        o = jnp.dot(w, perm, preferred_element_type=F32)
        g_ref[0, :, c * half:(c + 1) * half] = o[:, :half].astype(BF16)
        l_ref[0, :, c * half:(c + 1) * half] = o[:, half:].astype(BF16)


def _split_w1(w1, tr=512, tc=2048):
    n_exp, d, f2 = w1.shape
    out = jax.ShapeDtypeStruct((n_exp, d, f2 // 2), BF16)
    return pl.pallas_call(
        _split_w1_kernel,
        grid=(n_exp, d // tr, f2 // tc),
        in_specs=[pl.BlockSpec((1, tr, tc), lambda e, i, j: (e, i, j))],
        out_specs=[pl.BlockSpec((1, tr, tc // 2), lambda e, i, j: (e, i, j)),
                   pl.BlockSpec((1, tr, tc // 2), lambda e, i, j: (e, i, j))],
        out_shape=[out, out],
        compiler_params=_cparams(("arbitrary", "arbitrary", "arbitrary")),
        name="split_w1",
    )(w1)


def _cast_kernel(w_ref, o_ref):
    o_ref[...] = w_ref[...].astype(o_ref.dtype)


def _cast_bf16(w, tr=512):
    n_exp, r, c = w.shape
    return pl.pallas_call(
        _cast_kernel,
        grid=(n_exp, r // tr),
        in_specs=[pl.BlockSpec((1, tr, c), lambda e, i: (e, i, 0))],
        out_specs=pl.BlockSpec((1, tr, c), lambda e, i: (e, i, 0)),
        out_shape=jax.ShapeDtypeStruct(w.shape, BF16),
        compiler_params=_cparams(("arbitrary", "arbitrary")),
        name="cast_w2",
    )(w)


def _ffn_kernel(be_ref, nu_ref, tok_ref, h_hbm, w1g_ref, w1l_ref, b1g_ref, b1l_ref, w2_ref, b2_ref, o_ref,
                xf_ref, xb_ref, sem, *, n_f):
    m = pl.program_id(0)
    f = pl.program_id(1)
    n_blocks = pl.num_programs(0)
    bm = o_ref.shape[0]
    rows = bm // n_f

    def row_copy(blk, i):
        tok = tok_ref[blk * bm + i]
        return pltpu.make_async_copy(h_hbm.at[pl.ds(tok, 1), :], xf_ref.at[pl.ds(i, 1), :], sem)

    def all_rows():
        return pltpu.make_async_copy(h_hbm.at[pl.ds(0, bm), :], xf_ref, sem)

    @pl.when((m == 0) & (f == 0))
    def _():
        def issue(i, carry):
            row_copy(0, i).start()
            return carry

        lax.fori_loop(0, bm, issue, 0)

    @pl.when(f == 0)
    def _():
        all_rows().wait()
        xb_ref[...] = xf_ref[...].astype(BF16)
        o_ref[...] = jnp.broadcast_to(b2_ref[0], o_ref.shape)

    nxt = jnp.minimum(m + 1, n_blocks - 1)

    @pl.when(m < nu_ref[0])
    def _():
        for i in range(rows):
            row_copy(nxt, f * rows + i).start()
        x = xb_ref[...]
        hg = jnp.dot(x, w1g_ref[0], preferred_element_type=F32) + b1g_ref[0]
        hl = jnp.dot(x, w1l_ref[0], preferred_element_type=F32) + b1l_ref[0]
        x_glu = jnp.minimum(hg, SWIGLU_LIMIT)
        x_lin = jnp.clip(hl, -SWIGLU_LIMIT, SWIGLU_LIMIT)
        act = x_glu * jax.nn.sigmoid(SWIGLU_ALPHA * x_glu) * (x_lin + 1.0)
        act = act.astype(BF16)
        for c in range(0, o_ref.shape[1], OUT_CHUNK):
            o_ref[:, c:c + OUT_CHUNK] += jnp.dot(act, w2_ref[0, :, c:c + OUT_CHUNK], preferred_element_type=F32)

    @pl.when(m >= nu_ref[0])
    def _():
        def issue(i, carry):
            row_copy(nxt, f * rows + i).start()
            return carry

        lax.fori_loop(0, rows, issue, 0)

    @pl.when((m == n_blocks - 1) & (f == n_f - 1))
    def _():
        all_rows().wait()


def _experts(h, slot_token, block_expert, n_used, w1g, w1l, b1g, b1l, w2, b2, bm=MOE_BM, tf=1024):
    d = h.shape[1]
    n_slots = slot_token.shape[0]
    n_blocks = n_slots // bm
    ff = w1g.shape[2]
    n_f = ff // tf

    def f_eff(m, f, nu):
        return jnp.where(m < nu[0], f, n_f - 1)

    return pl.pallas_call(
        functools.partial(_ffn_kernel, n_f=n_f),
        grid_spec=pltpu.PrefetchScalarGridSpec(
            num_scalar_prefetch=3,
            grid=(n_blocks, n_f),
            in_specs=[
                pl.BlockSpec(memory_space=pl.ANY),
                pl.BlockSpec((1, d, tf), lambda m, f, be, nu, tok: (be[m], 0, f_eff(m, f, nu))),
                pl.BlockSpec((1, d, tf), lambda m, f, be, nu, tok: (be[m], 0, f_eff(m, f, nu))),
                pl.BlockSpec((1, 1, tf), lambda m, f, be, nu, tok: (be[m], 0, f_eff(m, f, nu))),
                pl.BlockSpec((1, 1, tf), lambda m, f, be, nu, tok: (be[m], 0, f_eff(m, f, nu))),
                pl.BlockSpec((1, tf, d), lambda m, f, be, nu, tok: (be[m], f_eff(m, f, nu), 0)),
                pl.BlockSpec((1, 1, d), lambda m, f, be, nu, tok: (be[m], 0, 0)),
            ],
            out_specs=pl.BlockSpec((bm, d), lambda m, f, be, nu, tok: (m, 0)),
            scratch_shapes=[pltpu.VMEM((bm, d), F32), pltpu.VMEM((bm, d), BF16), pltpu.SemaphoreType.DMA(())]),
        out_shape=jax.ShapeDtypeStruct((n_slots, d), F32),
        compiler_params=_cparams(("arbitrary", "arbitrary")),
        name="moe_experts",
    )(block_expert, n_used, slot_token, h, w1g, w1l, b1g, b1l, w2, b2)


def _combine_kernel(pos_ref, h_ref, g_ref, lw_ref, lb_ref, y_hbm, o_ref, buf, sem):
    i = pl.program_id(0)
    tm = h_ref.shape[0]

    def issue_step(step, slot):
        def group(gi, carry):
            n0 = pl.multiple_of(gi * GATHER_UNROLL, GATHER_UNROLL)
            for j in range(GATHER_UNROLL):
                for k in range(TOP_K):
                    p = pos_ref[(step * tm + n0 + j) * TOP_K + k]
                    pltpu.make_async_copy(y_hbm.at[pl.ds(p, 1), :], buf.at[slot, pl.ds(k * tm + n0 + j, 1), :],
                                          sem.at[slot]).start()
            return carry

        lax.fori_loop(0, tm // GATHER_UNROLL, group, 0)

    @pl.when(i == 0)
    def _():
        issue_step(0, 0)

    @pl.when(i + 1 < pl.num_programs(0))
    def _():
        issue_step(i + 1, (i + 1) % 2)

    slot = i % 2
    pltpu.make_async_copy(y_hbm.at[pl.ds(0, TOP_K * tm), :], buf.at[slot], sem.at[slot]).wait()
    g = g_ref[...]
    ffn = g[:, 0:1] * buf[slot, 0:tm, :]
    for k in range(1, TOP_K):
        ffn = ffn + g[:, k:k + 1] * buf[slot, k * tm:(k + 1) * tm, :]
    o_ref[...] = _layer_norm(DEEPNORM_ALPHA * h_ref[...] + ffn, lw_ref[...], lb_ref[...], LN_EPS)


def _combine(pos_flat, h, gates, ln_w, ln_b, y_slots, tm=256):
    n, d = h.shape
    return pl.pallas_call(
        _combine_kernel,
        grid_spec=pltpu.PrefetchScalarGridSpec(
            num_scalar_prefetch=1,
            grid=(n // tm,),
            in_specs=[pl.BlockSpec((tm, d), lambda i, pos: (i, 0)),
                      pl.BlockSpec((tm, LANES), lambda i, pos: (i, 0)),
                      pl.BlockSpec((1, d), lambda i, pos: (0, 0)),
                      pl.BlockSpec((1, d), lambda i, pos: (0, 0)),
                      pl.BlockSpec(memory_space=pl.ANY)],
            out_specs=pl.BlockSpec((tm, d), lambda i, pos: (i, 0)),
            scratch_shapes=[pltpu.VMEM((2, TOP_K * tm, d), F32), pltpu.SemaphoreType.DMA((2,))]),
        out_shape=jax.ShapeDtypeStruct((n, d), F32),
        compiler_params=_cparams(("arbitrary",)),
        name="moe_combine_ln",
    )(pos_flat, h, gates, ln_w, ln_b, y_slots)


def _moe(h, route_i, gates, counts_f, w1, b1, w2, b2, ln_w, ln_b):
    n_tok, d = h.shape
    n_exp = w1.shape[0]
    n_asg = n_tok * TOP_K
    n_blocks = -(-n_asg // MOE_BM) + n_exp

    counts = counts_f[0, :n_exp].astype(jnp.int32)
    padded = (counts + MOE_BM - 1) // MOE_BM * MOE_BM
    pend = jnp.cumsum(padded)
    pstart = pend - padded
    pos = pstart[route_i[:, 0:TOP_K]] + route_i[:, TOP_K:2 * TOP_K]
    pos_flat = pos.reshape(n_asg)
    token_of = jnp.arange(n_asg, dtype=jnp.int32) // TOP_K
    slot_token = jnp.zeros((n_blocks * MOE_BM,), jnp.int32).at[pos_flat].set(token_of)
    n_used = (pend[-1] // MOE_BM).astype(jnp.int32).reshape(1)
    blk = jnp.minimum(jnp.arange(n_blocks, dtype=jnp.int32), n_used[0] - 1)
    block_expert = jnp.minimum(jnp.sum(pend[None, :] <= (blk * MOE_BM)[:, None], axis=1), n_exp - 1).astype(jnp.int32)

    w1g, w1l = _split_w1(w1)
    b1g = b1[:, None, 0::2]
    b1l = b1[:, None, 1::2]
    w2b = _cast_bf16(w2)

    y_slots = _experts(h, slot_token, block_expert, n_used, w1g, w1l, b1g, b1l, w2b, b2[:, None, :])
    return _combine(pos_flat, h, gates, ln_w, ln_b, y_slots)


def _pad_to(a, axis, size):
    pad = [(0, 0)] * a.ndim
    pad[axis] = (0, size - a.shape[axis])
    return jnp.pad(a, pad)


def kernel(x, w_in, mu_shift, w0, w2_decay, a0, a2_iclr, g2_gate, k_k, k_a, r_k, lnx_w, lnx_b, w_pool, pool_scale,
           w_out, ln1_w, ln1_b, router_w, router_b, w1_exp, b1_exp, w2_exp, b2_exp, ln2_w, ln2_b):
    batch, seq, d = x.shape
    assert w_in.shape[0] == 1, "one layer"
    x2d = x.reshape(batch * seq, d)
    c_xw = RKV_COLS
    c_xa = c_xw + DECAY_LORA
    c_xg = c_xa + ICLR_LORA
    c_pool = c_xg + GATE_LORA

    def lora_layout(a):
        return jnp.concatenate([_pad_to(a[:, c_xw:c_xa], 1, LORA_XA - LORA_XW),
                                _pad_to(a[:, c_xa:c_xg], 1, LORA_XG - LORA_XA),
                                _pad_to(a[:, c_xg:c_pool], 1, LORA_PAD - LORA_XG)], axis=1)

    wi = w_in[0]
    w_p = jnp.concatenate([wi[:, :RKV_COLS], wi[:, c_pool:], lora_layout(wi)], axis=1).astype(BF16)
    mu = mu_shift[0][None, :]
    row = lambda a: a.reshape(1, -1)

    proj = _in_proj(x2d, w_p)
    y_rwkv = _rwkv_group(
        proj, batch, seq, mu[:, :RKV_COLS], lora_layout(mu), row(w0[0]),
        _pad_to(w2_decay[0], 0, LANES).astype(BF16), row(a0[0]), _pad_to(a2_iclr[0], 0, LANES).astype(BF16),
        _pad_to(g2_gate[0], 0, LORA_PAD - LORA_XG).astype(BF16), row(k_k[0]), row(k_a[0]), row(r_k[0]),
        row(lnx_w[0]), row(lnx_b[0]))
    y_pool = _pool_group(proj, batch, seq, w_pool[0].astype(BF16), row(pool_scale[0]))

    n_exp = router_w.shape[2]
    router_w_p = _pad_to(router_w[0], 1, LANES)
    router_b_p = jnp.concatenate([router_b[0], jnp.full((LANES - n_exp,), -1e30, F32)])[None, :]
    h1, route_i, gates, counts = _mix_ln_router(y_rwkv, y_pool, x2d, w_out[0].astype(BF16), row(ln1_w[0]),
                                                row(ln1_b[0]), router_w_p, router_b_p)
    out = _moe(h1, route_i, gates, counts, w1_exp[0], b1_exp[0], w2_exp[0], b2_exp[0], row(ln2_w[0]), row(ln2_b[0]))
    return out.reshape(batch, seq, d)
```

```python
import jax
import jax.numpy as jnp
from jax import lax
from jax.experimental import pallas as pl
from jax.experimental.pallas import tpu as pltpu

F32 = jnp.float32
BF16 = jnp.bfloat16

RWKV_WIDTH = 1024
HEAD = 64
DECAY_LORA = 64
ICLR_LORA = 64
GATE_LORA = 160
POOL_WINDOWS = (2, 4, 8, 16)
POOL_GROUP_WIDTH = 256
TOP_K = 4
SWIGLU_LIMIT = 7.0
SWIGLU_ALPHA = 1.702
LN_EPS = 1e-5
GN_EPS = 64e-5
DEEPNORM_ALPHA = 2.0 ** 0.25

LANES = 128
LORA_PAD = 512
LORA_XW, LORA_XA, LORA_XG = 0, 128, 256
RKV_COLS = 3 * RWKV_WIDTH
POOL_WIDTH = len(POOL_WINDOWS) * POOL_GROUP_WIDTH
POOL_COL0 = RKV_COLS
LORA_COL0 = POOL_COL0 + POOL_WIDTH
PROJ_COLS = LORA_COL0 + LORA_PAD
CHUNK = 64
SUB = 16
VMEM_LIMIT = 56 * 1024 * 1024


def _cparams(sem):
    return pltpu.CompilerParams(dimension_semantics=sem, vmem_limit_bytes=VMEM_LIMIT)


def _bdot(a, b):
    return jnp.dot(a.astype(BF16), b.astype(BF16), preferred_element_type=F32)


def _bdot_nt(a, b):
    return lax.dot_general(a.astype(BF16), b.astype(BF16), (((1,), (1,)), ((), ())),
                           preferred_element_type=F32)


def _bdot_tn(a, b):
    return lax.dot_general(a.astype(BF16), b.astype(BF16), (((0,), (0,)), ((), ())),
                           preferred_element_type=F32)


def _in_proj_kernel(x_ref, w_ref, o_ref, xb_ref):
    @pl.when(pl.program_id(1) == 0)
    def _():
        xb_ref[...] = x_ref[...].astype(BF16)

    o_ref[...] = jnp.dot(xb_ref[...], w_ref[...], preferred_element_type=F32)


def _in_proj(x2d, w_p, tm=1024, tn=768):
    n, d = x2d.shape
    cols = w_p.shape[1]
    return pl.pallas_call(
        _in_proj_kernel,
        grid=(n // tm, cols // tn),
        in_specs=[pl.BlockSpec((tm, d), lambda i, j: (i, 0)),
                  pl.BlockSpec((d, tn), lambda i, j: (0, j))],
        out_specs=pl.BlockSpec((tm, tn), lambda i, j: (i, j)),
        out_shape=jax.ShapeDtypeStruct((n, cols), F32),
        scratch_shapes=[pltpu.VMEM((tm, d), BF16)],
        compiler_params=_cparams(("arbitrary", "arbitrary")),
        name="in_proj",
    )(x2d, w_p)


def _shifted(x_ref, prev_ref, mu_ref, first_row):
    x = x_ref[...]
    xp = jnp.where(first_row, prev_ref[...], pltpu.roll(x, 1, axis=0))
    prev_ref[...] = x_ref[x.shape[0] - 1:x.shape[0], :]
    return x + (xp - x) * mu_ref[...]


def _rwkv_kernel(r_ref, k_ref, v_ref, lo_ref, mur_ref, muk_ref, muv_ref, mul_ref,
                 w0_ref, w2d_ref, a0_ref, a2_ref, g2_ref, kk_ref, ka_ref, rk_ref, lnw_ref, lnb_ref,
                 o_ref,
                 pr_ref, pk_ref, pv_ref, plo_ref, s_ref,
                 rs_ref, lw_ref, ks_ref, vs_ref, kks_ref, as_ref, gs_ref):
    tb, width = r_ref.shape
    n_pairs = width // LANES
    c2 = 2 * CHUNK

    @pl.when(pl.program_id(2) == 0)
    def _():
        pr_ref[...] = jnp.zeros_like(pr_ref)
        pk_ref[...] = jnp.zeros_like(pk_ref)
        pv_ref[...] = jnp.zeros_like(pv_ref)
        plo_ref[...] = jnp.zeros_like(plo_ref)
        s_ref[...] = jnp.zeros_like(s_ref)

    first_row = lax.broadcasted_iota(jnp.int32, (tb, 1), 0) == 0
    r = _shifted(r_ref, pr_ref, mur_ref, first_row)
    k = _shifted(k_ref, pk_ref, muk_ref, first_row)
    v = _shifted(v_ref, pv_ref, muv_ref, first_row)
    lo = _shifted(lo_ref, plo_ref, mul_ref, first_row)

    xw = jnp.tanh(lo[:, LORA_XW:LORA_XW + LANES])
    xa = lo[:, LORA_XA:LORA_XA + LANES]
    xg = jax.nn.sigmoid(lo[:, LORA_XG:LORA_PAD])
    z = w0_ref[...] + _bdot(xw, w2d_ref[...])
    w_logit = -(jnp.maximum(-z, 0.0) + jnp.log(1.0 + jnp.exp(-jnp.abs(z)))) - 0.5
    a_ic = jax.nn.sigmoid(a0_ref[...] + _bdot(xa, a2_ref[...]))
    rs_ref[...] = r
    lw_ref[...] = -jnp.exp(w_logit)
    ks_ref[...] = k * (1.0 + (a_ic - 1.0) * ka_ref[...])
    vs_ref[...] = v
    kks_ref[...] = k * kk_ref[...]
    as_ref[...] = a_ic
    gs_ref[...] = _bdot(xg, g2_ref[...])

    lane = lax.broadcasted_iota(jnp.int32, (CHUNK, LANES), 1)
    head0 = lane < HEAD
    row = lax.broadcasted_iota(jnp.int32, (c2, c2), 0)
    col = lax.broadcasted_iota(jnp.int32, (c2, c2), 1)
    strict = row > col
    incl = row >= col
    same_sub = (row // SUB) == (col // SUB)
    eye = (row == col).astype(F32)
    tri = (lax.broadcasted_iota(jnp.int32, (CHUNK, CHUNK), 0)
           >= lax.broadcasted_iota(jnp.int32, (CHUNK, CHUNK), 1)).astype(F32)
    def head_sum(x):
        s0 = jnp.sum(jnp.where(head0, x, 0.0), axis=-1, keepdims=True)
        s1 = jnp.sum(jnp.where(head0, 0.0, x), axis=-1, keepdims=True)
        return jnp.where(head0, s0, s1)

    def stack(x):
        return jnp.concatenate([jnp.where(head0, x, 0.0), jnp.where(head0, 0.0, x)], axis=0)

    def operands(rc, lw, lwi, kc, vc, kk, ac):
        kkn = kk * lax.rsqrt(jnp.maximum(head_sum(kk * kk), 1e-24))
        b_ = kkn * ac
        tot = lwi[CHUNK - 1:CHUNK, :]
        e_inv = jnp.exp(-lwi)
        e_out = jnp.exp(tot - lwi)
        at = stack(-kkn * jnp.exp(lwi - lw)).astype(BF16)
        rt = stack(rc * jnp.exp(lwi)).astype(BF16)
        bk = jnp.concatenate([stack(b_ * e_inv), stack(kc * e_inv)], axis=0).astype(BF16)
        bkh = jnp.concatenate([stack(b_ * e_out), stack(kc * e_out)], axis=0).astype(BF16)
        return at, rt, bk, bkh, stack(vc).astype(BF16), jnp.exp(tot)

    def finish(y_st, rc, kc, vc, gate, rk, lnw, lnb):
        y = y_st[:CHUNK, :] + y_st[CHUNK:, :]
        mu = head_sum(y) * (1.0 / HEAD)
        yc = y - mu
        var = head_sum(yc * yc) * (1.0 / HEAD)
        yn = yc * lax.rsqrt(var + GN_EPS) * lnw + lnb
        yn = yn + head_sum(rc * kc * rk) * vc
        return (yn * gate).astype(o_ref.dtype)

    def chunk(c, carry):
        sl = pl.ds(pl.multiple_of(c * CHUNK, CHUNK), CHUNK)
        lw = lw_ref[sl, :]
        lwi = jnp.dot(tri, lw, preferred_element_type=F32, precision=lax.Precision.HIGHEST)
        pairs = range(n_pairs)
        ls = [slice(p * LANES, (p + 1) * LANES) for p in pairs]
        ops = [operands(rs_ref[sl, l], lw[:, l], lwi[:, l], ks_ref[sl, l], vs_ref[sl, l], kks_ref[sl, l],
                        as_ref[sl, l]) for l in ls]
        at, rt, bk, bkh, vst, dec = (list(z) for z in zip(*ops))
        a_all = [_bdot_nt(jnp.concatenate([at[p], rt[p]], axis=0), bk[p]) for p in pairs]
        a_ab = [jnp.where(strict, a[:c2, :c2], 0.0) for a in a_all]
        a_ak = [jnp.where(strict, a[:c2, c2:], 0.0) for a in a_all]
        a_r = [jnp.where(jnp.concatenate([incl, incl], axis=1), a[c2:, :], 0.0).astype(BF16) for a in a_all]
        akv = [_bdot(a_ak[p], vst[p]).astype(BF16) for p in pairs]

        d1 = [jnp.where(same_sub, a, 0.0) for a in a_ab]
        l1 = [a_ab[p] - d1[p] for p in pairs]
        d2 = [_bdot(d, d) for d in d1]
        d4 = [_bdot(d, d) for d in d2]
        p12 = [_bdot(eye + d1[p], eye + d2[p]) for p in pairs]
        d8 = [_bdot(d, d) for d in d4]
        p48 = [_bdot(eye + d4[p], eye + d8[p]) for p in pairs]
        t_d = [_bdot(p12[p], p48[p]).astype(BF16) for p in pairs]
        m1 = [_bdot(t_d[p], l1[p]) for p in pairs]
        m2 = [_bdot(m, m) for m in m1]
        q = [_bdot(eye + m1[p], eye + m2[p]) for p in pairs]
        t_inv = [_bdot(q[p], t_d[p]) for p in pairs]
        tx = [_bdot(t_inv[p], jnp.concatenate([at[p], akv[p]], axis=1)) for p in pairs]

        s_bd = [s_ref[p] for p in pairs]
        wr = [_bdot_nt(jnp.concatenate([tx[p][:, :LANES].astype(BF16), rt[p]], axis=0), s_bd[p]) for p in pairs]
        uv = [jnp.concatenate([(wr[p][:c2, :] + tx[p][:, LANES:]).astype(BF16), vst[p]], axis=0) for p in pairs]
        y_st = [wr[p][c2:, :] + _bdot(a_r[p], uv[p]) for p in pairs]
        for p in pairs:
            s_ref[p] = s_bd[p] * dec[p] + _bdot_tn(uv[p], bkh[p])
        for p, l in zip(pairs, ls):
            o_ref[sl, l] = finish(y_st[p], rs_ref[sl, l], ks_ref[sl, l], vs_ref[sl, l], gs_ref[sl, l],
                                  rk_ref[:, l], lnw_ref[:, l], lnb_ref[:, l])
        return carry

    lax.fori_loop(0, tb // CHUNK, chunk, 0)


RWKV_PAIRS_PER_STEP = 8


def _rwkv_group(proj, batch, seq, mu_rkv, mu_lora, w0, w2d, a0, a2, g2, k_k, k_a, r_k, lnx_w, lnx_b, tb=512,
                pairs=RWKV_PAIRS_PER_STEP):
    width = pairs * LANES
    n_g = RWKV_WIDTH // width
    n_tb = seq // tb
    row = lambda b, h, t: b * n_tb + t
    vec = lambda off: pl.BlockSpec((1, width), lambda b, h, t: (0, off + h))
    lora_blk = LORA_COL0 // LORA_PAD
    in_specs = [
        pl.BlockSpec((tb, width), lambda b, h, t: (row(b, h, t), h)),
        pl.BlockSpec((tb, width), lambda b, h, t: (row(b, h, t), n_g + h)),
        pl.BlockSpec((tb, width), lambda b, h, t: (row(b, h, t), 2 * n_g + h)),
        pl.BlockSpec((tb, LORA_PAD), lambda b, h, t: (row(b, h, t), lora_blk)),
        vec(0), vec(n_g), vec(2 * n_g),
        pl.BlockSpec((1, LORA_PAD), lambda b, h, t: (0, 0)),
        vec(0),
        pl.BlockSpec((LANES, width), lambda b, h, t: (0, h)),
        vec(0),
        pl.BlockSpec((LANES, width), lambda b, h, t: (0, h)),
        pl.BlockSpec((LORA_PAD - LORA_XG, width), lambda b, h, t: (0, h)),
        vec(0), vec(0), vec(0), vec(0), vec(0),
    ]
    blk = lambda: pltpu.VMEM((tb, width), F32)
    return pl.pallas_call(
        _rwkv_kernel,
        grid=(batch, n_g, n_tb),
        in_specs=in_specs,
        out_specs=pl.BlockSpec((tb, width), lambda b, h, t: (row(b, h, t), h)),
        out_shape=jax.ShapeDtypeStruct((batch * seq, RWKV_WIDTH), BF16),
        scratch_shapes=[pltpu.VMEM((1, width), F32), pltpu.VMEM((1, width), F32), pltpu.VMEM((1, width), F32),
                        pltpu.VMEM((1, LORA_PAD), F32), pltpu.VMEM((pairs, 2 * CHUNK, LANES), F32),
                        blk(), blk(), blk(), blk(), blk(), blk(), blk()],
        compiler_params=_cparams(("arbitrary", "arbitrary", "arbitrary")),
        name="rwkv7_group",
    )(proj, proj, proj, proj, mu_rkv, mu_rkv, mu_rkv, mu_lora, w0, w2d, a0, a2, g2, k_k, k_a, r_k, lnx_w, lnx_b)


HALO = 16


def _pool_kernel(p_ref, w_ref, sc_ref, o_ref, ext_ref):
    tb = p_ref.shape[0]
    t = pl.program_id(1)

    @pl.when(t == 0)
    def _():
        ext_ref[0:HALO, :] = jnp.zeros((HALO, ext_ref.shape[1]), F32)

    @pl.when(t > 0)
    def _():
        ext_ref[0:HALO, :] = ext_ref[tb:tb + HALO, :]

    ext_ref[HALO:HALO + tb, :] = p_ref[...]
    t_idx = t * tb + lax.broadcasted_iota(jnp.int32, (tb, 1), 0)
    for gi, win in enumerate(POOL_WINDOWS):
        cs = slice(gi * POOL_GROUP_WIDTH, (gi + 1) * POOL_GROUP_WIDTH)
        acc = ext_ref[HALO:HALO + tb, cs]
        for d in range(1, win):
            acc = acc + ext_ref[HALO - d:HALO - d + tb, cs]
        count = jnp.minimum(t_idx + 1, win).astype(F32)
        pooled = acc / count - ext_ref[HALO:HALO + tb, cs]
        mixed = _bdot(pooled, w_ref[gi])
        o_ref[:, cs] = (mixed * sc_ref[:, cs]).astype(o_ref.dtype)


def _pool_group(proj, batch, seq, w_pool_b, pool_scale, tb=512):
    n_tb = seq // tb
    col_blk = POOL_COL0 // POOL_WIDTH
    return pl.pallas_call(
        _pool_kernel,
        grid=(batch, n_tb),
        in_specs=[pl.BlockSpec((tb, POOL_WIDTH), lambda b, t: (b * n_tb + t, col_blk)),
                  pl.BlockSpec(w_pool_b.shape, lambda b, t: (0, 0, 0)),
                  pl.BlockSpec((1, POOL_WIDTH), lambda b, t: (0, 0))],
        out_specs=pl.BlockSpec((tb, POOL_WIDTH), lambda b, t: (b * n_tb + t, 0)),
        out_shape=jax.ShapeDtypeStruct((batch * seq, POOL_WIDTH), BF16),
        scratch_shapes=[pltpu.VMEM((tb + HALO, POOL_WIDTH), F32)],
        compiler_params=_cparams(("arbitrary", "arbitrary")),
        name="pool_group",
    )(proj, w_pool_b, pool_scale)


def _layer_norm(x, w, b, eps):
    mu = jnp.mean(x, axis=-1, keepdims=True)
    xc = x - mu
    var = jnp.mean(xc * xc, axis=-1, keepdims=True)
    return xc * lax.rsqrt(var + eps) * w + b


def _mix_kernel(yr_ref, yp_ref, x_ref, wo_ref, lw_ref, lb_ref, rwh_ref, rwl_ref, rb_ref,
                h_ref, ri_ref, g_ref, cnt_ref, carry_ref):
    tm = x_ref.shape[0]
    half = yr_ref.shape[1]

    @pl.when(pl.program_id(0) == 0)
    def _():
        carry_ref[...] = jnp.zeros_like(carry_ref)

    mix = (jnp.dot(yr_ref[...], wo_ref[0:half, :], preferred_element_type=F32)
           + jnp.dot(yp_ref[...], wo_ref[half:, :], preferred_element_type=F32))
    h = _layer_norm(DEEPNORM_ALPHA * x_ref[...] + mix, lw_ref[...], lb_ref[...], LN_EPS)
    h_ref[...] = h

    h_hi = h.astype(BF16)
    h_lo = (h - h_hi.astype(F32)).astype(BF16)
    logits = (jnp.dot(h_hi, rwh_ref[...], preferred_element_type=F32)
              + jnp.dot(h_lo, rwh_ref[...], preferred_element_type=F32)
              + jnp.dot(h_hi, rwl_ref[...], preferred_element_type=F32)) + rb_ref[...]
    lane = lax.broadcasted_iota(jnp.int32, (tm, LANES), 1).astype(F32)
    idxs, vals = [], []
    left = logits
    for _ in range(TOP_K):
        m = jnp.max(left, axis=-1, keepdims=True)
        idx = jnp.min(jnp.where(left == m, lane, float(LANES)), axis=-1, keepdims=True)
        idxs.append(idx)
        vals.append(m)
        left = jnp.where(lane == idx, -jnp.inf, left)
    exps = [jnp.exp(v - vals[0]) for v in vals]
    denom = exps[0] + exps[1] + exps[2] + exps[3]

    onehot = jnp.zeros((tm, LANES), F32)
    for idx in idxs:
        onehot = onehot + (lane == idx).astype(F32)
    tri = (lax.broadcasted_iota(jnp.int32, (tm, tm), 0)
           > lax.broadcasted_iota(jnp.int32, (tm, tm), 1)).astype(BF16)
    before = jnp.dot(tri, onehot.astype(BF16), preferred_element_type=F32) + carry_ref[...]
    carry_ref[...] = carry_ref[...] + jnp.sum(onehot, axis=0, keepdims=True)
    cnt_ref[...] = carry_ref[...]

    ri = jnp.zeros((tm, LANES), F32)
    gt = jnp.zeros((tm, LANES), F32)
    for k in range(TOP_K):
        rank = jnp.sum(jnp.where(lane == idxs[k], before, 0.0), axis=-1, keepdims=True)
        ri = jnp.where(lane == float(k), idxs[k], ri)
        ri = jnp.where(lane == float(TOP_K + k), rank, ri)
        gt = jnp.where(lane == float(k), exps[k] / denom, gt)
    ri_ref[...] = ri.astype(jnp.int32)
    g_ref[...] = gt


def _mix_ln_router(y_rwkv, y_pool, x2d, w_out_b, ln_w, ln_b, router_w_p, router_b_p, tm=512):
    n, d = x2d.shape
    rw_hi = router_w_p.astype(BF16)
    rw_lo = (router_w_p - rw_hi.astype(F32)).astype(BF16)
    half = y_rwkv.shape[1]
    const = lambda shape: pl.BlockSpec(shape, lambda i: (0, 0))
    return pl.pallas_call(
        _mix_kernel,
        grid=(n // tm,),
        in_specs=[pl.BlockSpec((tm, half), lambda i: (i, 0)),
                  pl.BlockSpec((tm, y_pool.shape[1]), lambda i: (i, 0)),
                  pl.BlockSpec((tm, d), lambda i: (i, 0)),
                  const(w_out_b.shape), const((1, d)), const((1, d)),
                  const(router_w_p.shape), const(router_w_p.shape), const((1, LANES))],
        out_specs=[pl.BlockSpec((tm, d), lambda i: (i, 0)),
                   pl.BlockSpec((tm, LANES), lambda i: (i, 0)),
                   pl.BlockSpec((tm, LANES), lambda i: (i, 0)),
                   const((1, LANES))],
        out_shape=[jax.ShapeDtypeStruct((n, d), F32),
                   jax.ShapeDtypeStruct((n, LANES), jnp.int32),
                   jax.ShapeDtypeStruct((n, LANES), F32),
                   jax.ShapeDtypeStruct((1, LANES), F32)],
        scratch_shapes=[pltpu.VMEM((1, LANES), F32)],
        compiler_params=_cparams(("arbitrary",)),
        name="mix_ln_router",
    )(y_rwkv, y_pool, x2d, w_out_b, ln_w, ln_b, rw_hi, rw_lo, router_b_p)


MOE_BM = 512
GATHER_UNROLL = 8


MXU_DIM = 256
OUT_CHUNK = 512


def _split_w1_kernel(w_ref, g_ref, l_ref):
    half = MXU_DIM // 2
    src = lax.broadcasted_iota(jnp.int32, (MXU_DIM, MXU_DIM), 0)
    dst = lax.broadcasted_iota(jnp.int32, (MXU_DIM, MXU_DIM), 1)
    perm = (src == jnp.where(dst < half, 2 * dst, 2 * (dst - half) + 1)).astype(BF16)
    for c in range(w_ref.shape[2] // MXU_DIM):
        w = w_ref[0, :, c * MXU_DIM:(c + 1) * MXU_DIM].astype(BF16)
        o = jnp.dot(w, perm, preferred_element_type=F32)
        g_ref[0, :, c * half:(c + 1) * half] = o[:, :half].astype(BF16)
        l_ref[0, :, c * half:(c + 1) * half] = o[:, half:].astype(BF16)


def _split_w1(w1, tr=512, tc=2048):
    n_exp, d, f2 = w1.shape
    out = jax.ShapeDtypeStruct((n_exp, d, f2 // 2), BF16)
    return pl.pallas_call(
        _split_w1_kernel,
        grid=(n_exp, d // tr, f2 // tc),
        in_specs=[pl.BlockSpec((1, tr, tc), lambda e, i, j: (e, i, j))],
        out_specs=[pl.BlockSpec((1, tr, tc // 2), lambda e, i, j: (e, i, j)),
                   pl.BlockSpec((1, tr, tc // 2), lambda e, i, j: (e, i, j))],
        out_shape=[out, out],
        compiler_params=_cparams(("arbitrary", "arbitrary", "arbitrary")),
        name="split_w1",
    )(w1)


def _cast_kernel(w_ref, o_ref):
    o_ref[...] = w_ref[...].astype(o_ref.dtype)


def _cast_bf16(w, tr=512):
    n_exp, r, c = w.shape
    return pl.pallas_call(
        _cast_kernel,
        grid=(n_exp, r // tr),
        in_specs=[pl.BlockSpec((1, tr, c), lambda e, i: (e, i, 0))],
        out_specs=pl.BlockSpec((1, tr, c), lambda e, i: (e, i, 0)),
        out_shape=jax.ShapeDtypeStruct(w.shape, BF16),
        compiler_params=_cparams(("arbitrary", "arbitrary")),
        name="cast_w2",
    )(w)


def _ffn_kernel(be_ref, nu_ref, tok_ref, h_hbm, w1g_ref, w1l_ref, b1g_ref, b1l_ref, w2_ref, b2_ref, o_ref,
                xf_ref, xb_ref, sem):
    m = pl.program_id(0)
    f = pl.program_id(1)
    bm = o_ref.shape[0]

    def row_copy(blk, i):
        tok = tok_ref[blk * bm + i]
        return pltpu.make_async_copy(h_hbm.at[pl.ds(tok, 1), :], xf_ref.at[pl.ds(i, 1), :], sem)

    @pl.when((m == 0) & (f == 0))
    def _():
        def issue(i, carry):
            row_copy(0, i).start()
            return carry

        lax.fori_loop(0, bm, issue, 0)

    @pl.when(f == 0)
    def _():
        pltpu.make_async_copy(h_hbm.at[pl.ds(0, bm), :], xf_ref, sem).wait()
        xb_ref[...] = xf_ref[...].astype(BF16)
        o_ref[...] = jnp.broadcast_to(b2_ref[0], o_ref.shape)

        @pl.when(m + 1 < pl.num_programs(0))
        def _():
            for i in range(bm):
                row_copy(m + 1, i).start()

    @pl.when(m < nu_ref[0])
    def _():
        x = xb_ref[...]
        hg = jnp.dot(x, w1g_ref[0], preferred_element_type=F32) + b1g_ref[0]
        hl = jnp.dot(x, w1l_ref[0], preferred_element_type=F32) + b1l_ref[0]
        x_glu = jnp.minimum(hg, SWIGLU_LIMIT)
        x_lin = jnp.clip(hl, -SWIGLU_LIMIT, SWIGLU_LIMIT)
        act = x_glu * jax.nn.sigmoid(SWIGLU_ALPHA * x_glu) * (x_lin + 1.0)
        act = act.astype(BF16)
        for c in range(0, o_ref.shape[1], OUT_CHUNK):
            o_ref[:, c:c + OUT_CHUNK] += jnp.dot(act, w2_ref[0, :, c:c + OUT_CHUNK], preferred_element_type=F32)


def _experts(h, slot_token, block_expert, n_used, w1g, w1l, b1g, b1l, w2, b2, bm=MOE_BM, tf=1024):
    d = h.shape[1]
    n_slots = slot_token.shape[0]
    n_blocks = n_slots // bm
    ff = w1g.shape[2]
    n_f = ff // tf

    def f_eff(m, f, nu):
        return jnp.where(m < nu[0], f, n_f - 1)

    return pl.pallas_call(
        _ffn_kernel,
        grid_spec=pltpu.PrefetchScalarGridSpec(
            num_scalar_prefetch=3,
            grid=(n_blocks, n_f),
            in_specs=[
                pl.BlockSpec(memory_space=pl.ANY),
                pl.BlockSpec((1, d, tf), lambda m, f, be, nu, tok: (be[m], 0, f_eff(m, f, nu))),
                pl.BlockSpec((1, d, tf), lambda m, f, be, nu, tok: (be[m], 0, f_eff(m, f, nu))),
                pl.BlockSpec((1, 1, tf), lambda m, f, be, nu, tok: (be[m], 0, f_eff(m, f, nu))),
                pl.BlockSpec((1, 1, tf), lambda m, f, be, nu, tok: (be[m], 0, f_eff(m, f, nu))),
                pl.BlockSpec((1, tf, d), lambda m, f, be, nu, tok: (be[m], f_eff(m, f, nu), 0)),
                pl.BlockSpec((1, 1, d), lambda m, f, be, nu, tok: (be[m], 0, 0)),
            ],
            out_specs=pl.BlockSpec((bm, d), lambda m, f, be, nu, tok: (m, 0)),
            scratch_shapes=[pltpu.VMEM((bm, d), F32), pltpu.VMEM((bm, d), BF16), pltpu.SemaphoreType.DMA(())]),
        out_shape=jax.ShapeDtypeStruct((n_slots, d), F32),
        compiler_params=_cparams(("arbitrary", "arbitrary")),
        name="moe_experts",
    )(block_expert, n_used, slot_token, h, w1g, w1l, b1g, b1l, w2, b2)


def _combine_kernel(pos_ref, h_ref, g_ref, lw_ref, lb_ref, y_hbm, o_ref, buf, sem):
    i = pl.program_id(0)
    tm = h_ref.shape[0]

    def issue_step(step, slot):
        def group(gi, carry):
            n0 = pl.multiple_of(gi * GATHER_UNROLL, GATHER_UNROLL)
            for j in range(GATHER_UNROLL):
                for k in range(TOP_K):
                    p = pos_ref[(step * tm + n0 + j) * TOP_K + k]
                    pltpu.make_async_copy(y_hbm.at[pl.ds(p, 1), :], buf.at[slot, pl.ds(k * tm + n0 + j, 1), :],
                                          sem.at[slot]).start()
            return carry

        lax.fori_loop(0, tm // GATHER_UNROLL, group, 0)

    @pl.when(i == 0)
    def _():
        issue_step(0, 0)

    @pl.when(i + 1 < pl.num_programs(0))
    def _():
        issue_step(i + 1, (i + 1) % 2)

    slot = i % 2
    pltpu.make_async_copy(y_hbm.at[pl.ds(0, TOP_K * tm), :], buf.at[slot], sem.at[slot]).wait()
    g = g_ref[...]
    ffn = g[:, 0:1] * buf[slot, 0:tm, :]
    for k in range(1, TOP_K):
        ffn = ffn + g[:, k:k + 1] * buf[slot, k * tm:(k + 1) * tm, :]
    o_ref[...] = _layer_norm(DEEPNORM_ALPHA * h_ref[...] + ffn, lw_ref[...], lb_ref[...], LN_EPS)


def _combine(pos_flat, h, gates, ln_w, ln_b, y_slots, tm=256):
    n, d = h.shape
    return pl.pallas_call(
        _combine_kernel,
        grid_spec=pltpu.PrefetchScalarGridSpec(
            num_scalar_prefetch=1,
            grid=(n // tm,),
            in_specs=[pl.BlockSpec((tm, d), lambda i, pos: (i, 0)),
                      pl.BlockSpec((tm, LANES), lambda i, pos: (i, 0)),
                      pl.BlockSpec((1, d), lambda i, pos: (0, 0)),
                      pl.BlockSpec((1, d), lambda i, pos: (0, 0)),
                      pl.BlockSpec(memory_space=pl.ANY)],
            out_specs=pl.BlockSpec((tm, d), lambda i, pos: (i, 0)),
            scratch_shapes=[pltpu.VMEM((2, TOP_K * tm, d), F32), pltpu.SemaphoreType.DMA((2,))]),
        out_shape=jax.ShapeDtypeStruct((n, d), F32),
        compiler_params=_cparams(("arbitrary",)),
        name="moe_combine_ln",
    )(pos_flat, h, gates, ln_w, ln_b, y_slots)


def _moe(h, route_i, gates, counts_f, w1, b1, w2, b2, ln_w, ln_b):
    n_tok, d = h.shape
    n_exp = w1.shape[0]
    n_asg = n_tok * TOP_K
    n_blocks = -(-n_asg // MOE_BM) + n_exp

    counts = counts_f[0, :n_exp].astype(jnp.int32)
    padded = (counts + MOE_BM - 1) // MOE_BM * MOE_BM
    pend = jnp.cumsum(padded)
    pstart = pend - padded
    pos = pstart[route_i[:, 0:TOP_K]] + route_i[:, TOP_K:2 * TOP_K]
    pos_flat = pos.reshape(n_asg)
    token_of = jnp.arange(n_asg, dtype=jnp.int32) // TOP_K
    slot_token = jnp.zeros((n_blocks * MOE_BM,), jnp.int32).at[pos_flat].set(token_of)
    n_used = (pend[-1] // MOE_BM).astype(jnp.int32).reshape(1)
    blk = jnp.minimum(jnp.arange(n_blocks, dtype=jnp.int32), n_used[0] - 1)
    block_expert = jnp.minimum(jnp.sum(pend[None, :] <= (blk * MOE_BM)[:, None], axis=1), n_exp - 1).astype(jnp.int32)

    w1g, w1l = _split_w1(w1)
    b1g = b1[:, None, 0::2]
    b1l = b1[:, None, 1::2]
    w2b = _cast_bf16(w2)

    y_slots = _experts(h, slot_token, block_expert, n_used, w1g, w1l, b1g, b1l, w2b, b2[:, None, :])
    return _combine(pos_flat, h, gates, ln_w, ln_b, y_slots)


def _pad_to(a, axis, size):
    pad = [(0, 0)] * a.ndim
    pad[axis] = (0, size - a.shape[axis])
    return jnp.pad(a, pad)


def kernel(x, w_in, mu_shift, w0, w2_decay, a0, a2_iclr, g2_gate, k_k, k_a, r_k, lnx_w, lnx_b, w_pool, pool_scale,
           w_out, ln1_w, ln1_b, router_w, router_b, w1_exp, b1_exp, w2_exp, b2_exp, ln2_w, ln2_b):
    batch, seq, d = x.shape
    assert w_in.shape[0] == 1, "one layer"
    x2d = x.reshape(batch * seq, d)
    c_xw = RKV_COLS
    c_xa = c_xw + DECAY_LORA
    c_xg = c_xa + ICLR_LORA
    c_pool = c_xg + GATE_LORA

    def lora_layout(a):
        return jnp.concatenate([_pad_to(a[:, c_xw:c_xa], 1, LORA_XA - LORA_XW),
                                _pad_to(a[:, c_xa:c_xg], 1, LORA_XG - LORA_XA),
                                _pad_to(a[:, c_xg:c_pool], 1, LORA_PAD - LORA_XG)], axis=1)

    wi = w_in[0]
    w_p = jnp.concatenate([wi[:, :RKV_COLS], wi[:, c_pool:], lora_layout(wi)], axis=1).astype(BF16)
    mu = mu_shift[0][None, :]
    row = lambda a: a.reshape(1, -1)

    proj = _in_proj(x2d, w_p)
    y_rwkv = _rwkv_group(
        proj, batch, seq, mu[:, :RKV_COLS], lora_layout(mu), row(w0[0]),
        _pad_to(w2_decay[0], 0, LANES).astype(BF16), row(a0[0]), _pad_to(a2_iclr[0], 0, LANES).astype(BF16),
        _pad_to(g2_gate[0], 0, LORA_PAD - LORA_XG).astype(BF16), row(k_k[0]), row(k_a[0]), row(r_k[0]),
        row(lnx_w[0]), row(lnx_b[0]))
    y_pool = _pool_group(proj, batch, seq, w_pool[0].astype(BF16), row(pool_scale[0]))

    n_exp = router_w.shape[2]
    router_w_p = _pad_to(router_w[0], 1, LANES)
    router_b_p = jnp.concatenate([router_b[0], jnp.full((LANES - n_exp,), -1e30, F32)])[None, :]
    h1, route_i, gates, counts = _mix_ln_router(y_rwkv, y_pool, x2d, w_out[0].astype(BF16), row(ln1_w[0]),
                                                row(ln1_b[0]), router_w_p, router_b_p)
    out = _moe(h1, route_i, gates, counts, w1_exp[0], b1_exp[0], w2_exp[0], b2_exp[0], row(ln2_w[0]), row(ln2_b[0]))
    return out.reshape(batch, seq, d)
```

```python
import jax
import jax.numpy as jnp
from jax import lax
from jax.experimental import pallas as pl
from jax.experimental.pallas import tpu as pltpu

F32 = jnp.float32
BF16 = jnp.bfloat16

RWKV_WIDTH = 1024
HEAD = 64
DECAY_LORA = 64
ICLR_LORA = 64
GATE_LORA = 160
POOL_WINDOWS = (2, 4, 8, 16)
POOL_GROUP_WIDTH = 256
TOP_K = 4
SWIGLU_LIMIT = 7.0
SWIGLU_ALPHA = 1.702
LN_EPS = 1e-5
GN_EPS = 64e-5
DEEPNORM_ALPHA = 2.0 ** 0.25

LANES = 128
LORA_PAD = 512
LORA_XW, LORA_XA, LORA_XG = 0, 128, 256
RKV_COLS = 3 * RWKV_WIDTH
POOL_WIDTH = len(POOL_WINDOWS) * POOL_GROUP_WIDTH
POOL_COL0 = RKV_COLS
LORA_COL0 = POOL_COL0 + POOL_WIDTH
PROJ_COLS = LORA_COL0 + LORA_PAD
CHUNK = 64
SUB = 16
VMEM_LIMIT = 56 * 1024 * 1024


def _cparams(sem):
    return pltpu.CompilerParams(dimension_semantics=sem, vmem_limit_bytes=VMEM_LIMIT)


def _bdot(a, b):
    return jnp.dot(a.astype(BF16), b.astype(BF16), preferred_element_type=F32)


def _bdot_nt(a, b):
    return lax.dot_general(a.astype(BF16), b.astype(BF16), (((1,), (1,)), ((), ())),
                           preferred_element_type=F32)


def _bdot_tn(a, b):
    return lax.dot_general(a.astype(BF16), b.astype(BF16), (((0,), (0,)), ((), ())),
                           preferred_element_type=F32)


def _in_proj_kernel(x_ref, w_ref, o_ref, xb_ref):
    @pl.when(pl.program_id(1) == 0)
    def _():
        xb_ref[...] = x_ref[...].astype(BF16)

    o_ref[...] = jnp.dot(xb_ref[...], w_ref[...], preferred_element_type=F32)


def _in_proj(x2d, w_p, tm=1024, tn=768):
    n, d = x2d.shape
    cols = w_p.shape[1]
    return pl.pallas_call(
        _in_proj_kernel,
        grid=(n // tm, cols // tn),
        in_specs=[pl.BlockSpec((tm, d), lambda i, j: (i, 0)),
                  pl.BlockSpec((d, tn), lambda i, j: (0, j))],
        out_specs=pl.BlockSpec((tm, tn), lambda i, j: (i, j)),
        out_shape=jax.ShapeDtypeStruct((n, cols), F32),
        scratch_shapes=[pltpu.VMEM((tm, d), BF16)],
        compiler_params=_cparams(("arbitrary", "arbitrary")),
        name="in_proj",
    )(x2d, w_p)


def _shifted(x_ref, prev_ref, mu_ref, first_row):
    x = x_ref[...]
    xp = jnp.where(first_row, prev_ref[...], pltpu.roll(x, 1, axis=0))
    prev_ref[...] = x_ref[x.shape[0] - 1:x.shape[0], :]
    return x + (xp - x) * mu_ref[...]


def _rwkv_kernel(r_ref, k_ref, v_ref, lo_ref, mur_ref, muk_ref, muv_ref, mul_ref,
                 w0_ref, w2d_ref, a0_ref, a2_ref, g2_ref, kk_ref, ka_ref, rk_ref, lnw_ref, lnb_ref,
                 o_ref,
                 pr_ref, pk_ref, pv_ref, plo_ref, s_ref,
                 rs_ref, lw_ref, ks_ref, vs_ref, kks_ref, as_ref, gs_ref):
    tb, width = r_ref.shape
    n_pairs = width // LANES
    c2 = 2 * CHUNK

    @pl.when(pl.program_id(2) == 0)
    def _():
        pr_ref[...] = jnp.zeros_like(pr_ref)
        pk_ref[...] = jnp.zeros_like(pk_ref)
        pv_ref[...] = jnp.zeros_like(pv_ref)
        plo_ref[...] = jnp.zeros_like(plo_ref)
        s_ref[...] = jnp.zeros_like(s_ref)

    first_row = lax.broadcasted_iota(jnp.int32, (tb, 1), 0) == 0
    r = _shifted(r_ref, pr_ref, mur_ref, first_row)
    k = _shifted(k_ref, pk_ref, muk_ref, first_row)
    v = _shifted(v_ref, pv_ref, muv_ref, first_row)
    lo = _shifted(lo_ref, plo_ref, mul_ref, first_row)

    xw = jnp.tanh(lo[:, LORA_XW:LORA_XW + LANES])
    xa = lo[:, LORA_XA:LORA_XA + LANES]
    xg = jax.nn.sigmoid(lo[:, LORA_XG:LORA_PAD])
    z = w0_ref[...] + _bdot(xw, w2d_ref[...])
    w_logit = -(jnp.maximum(-z, 0.0) + jnp.log(1.0 + jnp.exp(-jnp.abs(z)))) - 0.5
    a_ic = jax.nn.sigmoid(a0_ref[...] + _bdot(xa, a2_ref[...]))
    rs_ref[...] = r
    lw_ref[...] = -jnp.exp(w_logit)
    ks_ref[...] = k * (1.0 + (a_ic - 1.0) * ka_ref[...])
    vs_ref[...] = v
    kks_ref[...] = k * kk_ref[...]
    as_ref[...] = a_ic
    gs_ref[...] = _bdot(xg, g2_ref[...])

    lane = lax.broadcasted_iota(jnp.int32, (CHUNK, LANES), 1)
    head0 = lane < HEAD
    row = lax.broadcasted_iota(jnp.int32, (c2, c2), 0)
    col = lax.broadcasted_iota(jnp.int32, (c2, c2), 1)
    strict = row > col
    incl = row >= col
    same_sub = (row // SUB) == (col // SUB)
    eye = (row == col).astype(F32)
    tri = (lax.broadcasted_iota(jnp.int32, (CHUNK, CHUNK), 0)
           >= lax.broadcasted_iota(jnp.int32, (CHUNK, CHUNK), 1)).astype(F32)
    def head_sum(x):
        s0 = jnp.sum(jnp.where(head0, x, 0.0), axis=-1, keepdims=True)
        s1 = jnp.sum(jnp.where(head0, 0.0, x), axis=-1, keepdims=True)
        return jnp.where(head0, s0, s1)

    def stack(x):
        return jnp.concatenate([jnp.where(head0, x, 0.0), jnp.where(head0, 0.0, x)], axis=0)

    def operands(rc, lw, lwi, kc, vc, kk, ac):
        kkn = kk * lax.rsqrt(jnp.maximum(head_sum(kk * kk), 1e-24))
        b_ = kkn * ac
        tot = lwi[CHUNK - 1:CHUNK, :]
        e_inv = jnp.exp(-lwi)
        e_out = jnp.exp(tot - lwi)
        at = stack(-kkn * jnp.exp(lwi - lw)).astype(BF16)
        rt = stack(rc * jnp.exp(lwi)).astype(BF16)
        bk = jnp.concatenate([stack(b_ * e_inv), stack(kc * e_inv)], axis=0).astype(BF16)
        bkh = jnp.concatenate([stack(b_ * e_out), stack(kc * e_out)], axis=0).astype(BF16)
        return at, rt, bk, bkh, stack(vc).astype(BF16), jnp.exp(tot)

    def finish(y_st, rc, kc, vc, gate, rk, lnw, lnb):
        y = y_st[:CHUNK, :] + y_st[CHUNK:, :]
        mu = head_sum(y) * (1.0 / HEAD)
        yc = y - mu
        var = head_sum(yc * yc) * (1.0 / HEAD)
        yn = yc * lax.rsqrt(var + GN_EPS) * lnw + lnb
        yn = yn + head_sum(rc * kc * rk) * vc
        return (yn * gate).astype(o_ref.dtype)

    def chunk(c, carry):
        sl = pl.ds(pl.multiple_of(c * CHUNK, CHUNK), CHUNK)
        lw = lw_ref[sl, :]
        lwi = jnp.dot(tri, lw, preferred_element_type=F32, precision=lax.Precision.HIGHEST)
        pairs = range(n_pairs)
        ls = [slice(p * LANES, (p + 1) * LANES) for p in pairs]
        ops = [operands(rs_ref[sl, l], lw[:, l], lwi[:, l], ks_ref[sl, l], vs_ref[sl, l], kks_ref[sl, l],
                        as_ref[sl, l]) for l in ls]
        at, rt, bk, bkh, vst, dec = (list(z) for z in zip(*ops))
        a_all = [_bdot_nt(jnp.concatenate([at[p], rt[p]], axis=0), bk[p]) for p in pairs]
        a_ab = [jnp.where(strict, a[:c2, :c2], 0.0) for a in a_all]
        a_ak = [jnp.where(strict, a[:c2, c2:], 0.0) for a in a_all]
        a_r = [jnp.where(jnp.concatenate([incl, incl], axis=1), a[c2:, :], 0.0).astype(BF16) for a in a_all]
        akv = [_bdot(a_ak[p], vst[p]).astype(BF16) for p in pairs]

        d1 = [jnp.where(same_sub, a, 0.0) for a in a_ab]
        l1 = [a_ab[p] - d1[p] for p in pairs]
        d2 = [_bdot(d, d) for d in d1]
        d4 = [_bdot(d, d) for d in d2]
        p12 = [_bdot(eye + d1[p], eye + d2[p]) for p in pairs]
        d8 = [_bdot(d, d) for d in d4]
        p48 = [_bdot(eye + d4[p], eye + d8[p]) for p in pairs]
        t_d = [_bdot(p12[p], p48[p]).astype(BF16) for p in pairs]
        m1 = [_bdot(t_d[p], l1[p]) for p in pairs]
        m2 = [_bdot(m, m) for m in m1]
        q = [_bdot(eye + m1[p], eye + m2[p]) for p in pairs]
        t_inv = [_bdot(q[p], t_d[p]) for p in pairs]
        tx = [_bdot(t_inv[p], jnp.concatenate([at[p], akv[p]], axis=1)) for p in pairs]

        s_bd = [s_ref[p] for p in pairs]
        wr = [_bdot_nt(jnp.concatenate([tx[p][:, :LANES].astype(BF16), rt[p]], axis=0), s_bd[p]) for p in pairs]
        uv = [jnp.concatenate([(wr[p][:c2, :] + tx[p][:, LANES:]).astype(BF16), vst[p]], axis=0) for p in pairs]
        y_st = [wr[p][c2:, :] + _bdot(a_r[p], uv[p]) for p in pairs]
        for p in pairs:
            s_ref[p] = s_bd[p] * dec[p] + _bdot_tn(uv[p], bkh[p])
        for p, l in zip(pairs, ls):
            o_ref[sl, l] = finish(y_st[p], rs_ref[sl, l], ks_ref[sl, l], vs_ref[sl, l], gs_ref[sl, l],
                                  rk_ref[:, l], lnw_ref[:, l], lnb_ref[:, l])
        return carry

    lax.fori_loop(0, tb // CHUNK, chunk, 0)


RWKV_PAIRS_PER_STEP = 8


def _rwkv_group(proj, batch, seq, mu_rkv, mu_lora, w0, w2d, a0, a2, g2, k_k, k_a, r_k, lnx_w, lnx_b, tb=512,
                pairs=RWKV_PAIRS_PER_STEP):
    width = pairs * LANES
    n_g = RWKV_WIDTH // width
    n_tb = seq // tb
    row = lambda b, h, t: b * n_tb + t
    vec = lambda off: pl.BlockSpec((1, width), lambda b, h, t: (0, off + h))
    lora_blk = LORA_COL0 // LORA_PAD
    in_specs = [
        pl.BlockSpec((tb, width), lambda b, h, t: (row(b, h, t), h)),
        pl.BlockSpec((tb, width), lambda b, h, t: (row(b, h, t), n_g + h)),
        pl.BlockSpec((tb, width), lambda b, h, t: (row(b, h, t), 2 * n_g + h)),
        pl.BlockSpec((tb, LORA_PAD), lambda b, h, t: (row(b, h, t), lora_blk)),
        vec(0), vec(n_g), vec(2 * n_g),
        pl.BlockSpec((1, LORA_PAD), lambda b, h, t: (0, 0)),
        vec(0),
        pl.BlockSpec((LANES, width), lambda b, h, t: (0, h)),
        vec(0),
        pl.BlockSpec((LANES, width), lambda b, h, t: (0, h)),
        pl.BlockSpec((LORA_PAD - LORA_XG, width), lambda b, h, t: (0, h)),
        vec(0), vec(0), vec(0), vec(0), vec(0),
    ]
    blk = lambda: pltpu.VMEM((tb, width), F32)
    return pl.pallas_call(
        _rwkv_kernel,
        grid=(batch, n_g, n_tb),
        in_specs=in_specs,
        out_specs=pl.BlockSpec((tb, width), lambda b, h, t: (row(b, h, t), h)),
        out_shape=jax.ShapeDtypeStruct((batch * seq, RWKV_WIDTH), BF16),
        scratch_shapes=[pltpu.VMEM((1, width), F32), pltpu.VMEM((1, width), F32), pltpu.VMEM((1, width), F32),
                        pltpu.VMEM((1, LORA_PAD), F32), pltpu.VMEM((pairs, 2 * CHUNK, LANES), F32),
                        blk(), blk(), blk(), blk(), blk(), blk(), blk()],
        compiler_params=_cparams(("arbitrary", "arbitrary", "arbitrary")),
        name="rwkv7_group",
    )(proj, proj, proj, proj, mu_rkv, mu_rkv, mu_rkv, mu_lora, w0, w2d, a0, a2, g2, k_k, k_a, r_k, lnx_w, lnx_b)


HALO = 16


def _pool_kernel(p_ref, w_ref, sc_ref, o_ref, ext_ref):
    tb = p_ref.shape[0]
    t = pl.program_id(1)

    @pl.when(t == 0)
    def _():
        ext_ref[0:HALO, :] = jnp.zeros((HALO, ext_ref.shape[1]), F32)

    @pl.when(t > 0)
    def _():
        ext_ref[0:HALO, :] = ext_ref[tb:tb + HALO, :]

    ext_ref[HALO:HALO + tb, :] = p_ref[...]
    t_idx = t * tb + lax.broadcasted_iota(jnp.int32, (tb, 1), 0)
    for gi, win in enumerate(POOL_WINDOWS):
        cs = slice(gi * POOL_GROUP_WIDTH, (gi + 1) * POOL_GROUP_WIDTH)
        acc = ext_ref[HALO:HALO + tb, cs]
        for d in range(1, win):
            acc = acc + ext_ref[HALO - d:HALO - d + tb, cs]
        count = jnp.minimum(t_idx + 1, win).astype(F32)
        pooled = acc / count - ext_ref[HALO:HALO + tb, cs]
        mixed = _bdot(pooled, w_ref[gi])
        o_ref[:, cs] = (mixed * sc_ref[:, cs]).astype(o_ref.dtype)


def _pool_group(proj, batch, seq, w_pool_b, pool_scale, tb=512):
    n_tb = seq // tb
    col_blk = POOL_COL0 // POOL_WIDTH
    return pl.pallas_call(
        _pool_kernel,
        grid=(batch, n_tb),
        in_specs=[pl.BlockSpec((tb, POOL_WIDTH), lambda b, t: (b * n_tb + t, col_blk)),
                  pl.BlockSpec(w_pool_b.shape, lambda b, t: (0, 0, 0)),
                  pl.BlockSpec((1, POOL_WIDTH), lambda b, t: (0, 0))],
        out_specs=pl.BlockSpec((tb, POOL_WIDTH), lambda b, t: (b * n_tb + t, 0)),
        out_shape=jax.ShapeDtypeStruct((batch * seq, POOL_WIDTH), BF16),
        scratch_shapes=[pltpu.VMEM((tb + HALO, POOL_WIDTH), F32)],
        compiler_params=_cparams(("arbitrary", "arbitrary")),
        name="pool_group",
    )(proj, w_pool_b, pool_scale)


def _layer_norm(x, w, b, eps):
    mu = jnp.mean(x, axis=-1, keepdims=True)
    xc = x - mu
    var = jnp.mean(xc * xc, axis=-1, keepdims=True)
    return xc * lax.rsqrt(var + eps) * w + b


def _mix_kernel(yr_ref, yp_ref, x_ref, wo_ref, lw_ref, lb_ref, rwh_ref, rwl_ref, rb_ref,
                h_ref, hp_ref, ri_ref, g_ref, cnt_ref, carry_ref):
    tm = x_ref.shape[0]
    half = yr_ref.shape[1]

    @pl.when(pl.program_id(0) == 0)
    def _():
        carry_ref[...] = jnp.zeros_like(carry_ref)

    mix = (jnp.dot(yr_ref[...], wo_ref[0:half, :], preferred_element_type=F32)
           + jnp.dot(yp_ref[...], wo_ref[half:, :], preferred_element_type=F32))
    h = _layer_norm(DEEPNORM_ALPHA * x_ref[...] + mix, lw_ref[...], lb_ref[...], LN_EPS)
    h_ref[...] = h
    half_d = h.shape[1] // 2
    lo = pltpu.bitcast(h[:, :half_d].astype(BF16).astype(F32), jnp.uint32) >> 16
    hi = pltpu.bitcast(h[:, half_d:].astype(BF16).astype(F32), jnp.uint32)
    words = lo | hi
    chunks = jnp.stack([words[:, c * LANES:(c + 1) * LANES] for c in range(hp_ref.shape[1])], axis=0)
    hp_ref[...] = pltpu.einshape("ctl->tcl", chunks)

    h_hi = h.astype(BF16)
    h_lo = (h - h_hi.astype(F32)).astype(BF16)
    logits = (jnp.dot(h_hi, rwh_ref[...], preferred_element_type=F32)
              + jnp.dot(h_lo, rwh_ref[...], preferred_element_type=F32)
              + jnp.dot(h_hi, rwl_ref[...], preferred_element_type=F32)) + rb_ref[...]
    lane = lax.broadcasted_iota(jnp.int32, (tm, LANES), 1).astype(F32)
    idxs, vals = [], []
    left = logits
    for _ in range(TOP_K):
        m = jnp.max(left, axis=-1, keepdims=True)
        idx = jnp.min(jnp.where(left == m, lane, float(LANES)), axis=-1, keepdims=True)
        idxs.append(idx)
        vals.append(m)
        left = jnp.where(lane == idx, -jnp.inf, left)
    exps = [jnp.exp(v - vals[0]) for v in vals]
    denom = exps[0] + exps[1] + exps[2] + exps[3]

    onehot = jnp.zeros((tm, LANES), F32)
    for idx in idxs:
        onehot = onehot + (lane == idx).astype(F32)
    tri = (lax.broadcasted_iota(jnp.int32, (tm, tm), 0)
           > lax.broadcasted_iota(jnp.int32, (tm, tm), 1)).astype(BF16)
    before = jnp.dot(tri, onehot.astype(BF16), preferred_element_type=F32) + carry_ref[...]
    carry_ref[...] = carry_ref[...] + jnp.sum(onehot, axis=0, keepdims=True)
    cnt_ref[...] = carry_ref[...]

    ri = jnp.zeros((tm, LANES), F32)
    gt = jnp.zeros((tm, LANES), F32)
    for k in range(TOP_K):
        rank = jnp.sum(jnp.where(lane == idxs[k], before, 0.0), axis=-1, keepdims=True)
        ri = jnp.where(lane == float(k), idxs[k], ri)
        ri = jnp.where(lane == float(TOP_K + k), rank, ri)
        gt = jnp.where(lane == float(k), exps[k] / denom, gt)
    ri_ref[...] = ri.astype(jnp.int32)
    g_ref[...] = gt


def _mix_ln_router(y_rwkv, y_pool, x2d, w_out_b, ln_w, ln_b, router_w_p, router_b_p, tm=512):
    n, d = x2d.shape
    rw_hi = router_w_p.astype(BF16)
    rw_lo = (router_w_p - rw_hi.astype(F32)).astype(BF16)
    half = y_rwkv.shape[1]
    const = lambda shape: pl.BlockSpec(shape, lambda i: (0, 0))
    return pl.pallas_call(
        _mix_kernel,
        grid=(n // tm,),
        in_specs=[pl.BlockSpec((tm, half), lambda i: (i, 0)),
                  pl.BlockSpec((tm, y_pool.shape[1]), lambda i: (i, 0)),
                  pl.BlockSpec((tm, d), lambda i: (i, 0)),
                  const(w_out_b.shape), const((1, d)), const((1, d)),
                  const(router_w_p.shape), const(router_w_p.shape), const((1, LANES))],
        out_specs=[pl.BlockSpec((tm, d), lambda i: (i, 0)),
                   pl.BlockSpec((tm, d // 2 // LANES, LANES), lambda i: (i, 0, 0)),
                   pl.BlockSpec((tm, LANES), lambda i: (i, 0)),
                   pl.BlockSpec((tm, LANES), lambda i: (i, 0)),
                   const((1, LANES))],
        out_shape=[jax.ShapeDtypeStruct((n, d), F32),
                   jax.ShapeDtypeStruct((n, d // 2 // LANES, LANES), jnp.uint32),
                   jax.ShapeDtypeStruct((n, LANES), jnp.int32),
                   jax.ShapeDtypeStruct((n, LANES), F32),
                   jax.ShapeDtypeStruct((1, LANES), F32)],
        scratch_shapes=[pltpu.VMEM((1, LANES), F32)],
        compiler_params=_cparams(("arbitrary",)),
        name="mix_ln_router",
    )(y_rwkv, y_pool, x2d, w_out_b, ln_w, ln_b, rw_hi, rw_lo, router_b_p)


MOE_BM = 512
GATHER_UNROLL = 8


MXU_DIM = 256
OUT_CHUNK = 512


def _split_w1_kernel(w_ref, g_ref, l_ref):
    half = MXU_DIM // 2
    src = lax.broadcasted_iota(jnp.int32, (MXU_DIM, MXU_DIM), 0)
    dst = lax.broadcasted_iota(jnp.int32, (MXU_DIM, MXU_DIM), 1)
    perm = (src == jnp.where(dst < half, 2 * dst, 2 * (dst - half) + 1)).astype(BF16)
    for c in range(w_ref.shape[2] // MXU_DIM):
        w = w_ref[0, :, c * MXU_DIM:(c + 1) * MXU_DIM].astype(BF16)
        o = jnp.dot(w, perm, preferred_element_type=F32)
        g_ref[0, :, c * half:(c + 1) * half] = o[:, :half].astype(BF16)
        l_ref[0, :, c * half:(c + 1) * half] = o[:, half:].astype(BF16)


def _split_w1(w1, tr=512, tc=2048):
    n_exp, d, f2 = w1.shape
    out = jax.ShapeDtypeStruct((n_exp, d, f2 // 2), BF16)
    return pl.pallas_call(
        _split_w1_kernel,
        grid=(n_exp, d // tr, f2 // tc),
        in_specs=[pl.BlockSpec((1, tr, tc), lambda e, i, j: (e, i, j))],
        out_specs=[pl.BlockSpec((1, tr, tc // 2), lambda e, i, j: (e, i, j)),
                   pl.BlockSpec((1, tr, tc // 2), lambda e, i, j: (e, i, j))],
        out_shape=[out, out],
        compiler_params=_cparams(("arbitrary", "arbitrary", "arbitrary")),
        name="split_w1",
    )(w1)


def _cast_kernel(w_ref, o_ref):
    o_ref[...] = w_ref[...].astype(o_ref.dtype)


def _cast_bf16(w, tr=512):
    n_exp, r, c = w.shape
    return pl.pallas_call(
        _cast_kernel,
        grid=(n_exp, r // tr),
        in_specs=[pl.BlockSpec((1, tr, c), lambda e, i: (e, i, 0))],
        out_specs=pl.BlockSpec((1, tr, c), lambda e, i: (e, i, 0)),
        out_shape=jax.ShapeDtypeStruct(w.shape, BF16),
        compiler_params=_cparams(("arbitrary", "arbitrary")),
        name="cast_w2",
    )(w)


def _ffn_kernel(be_ref, nu_ref, tok_ref, h_hbm, w1g_ref, w1l_ref, b1g_ref, b1l_ref, w2_ref, b2_ref, o_ref,
                xp_ref, xb_ref, sem):
    m = pl.program_id(0)
    f = pl.program_id(1)
    bm = o_ref.shape[0]

    def row_copy(blk, i):
        return pltpu.make_async_copy(h_hbm.at[tok_ref[blk * bm + i]], xp_ref.at[i], sem)

    @pl.when((m == 0) & (f == 0))
    def _():
        def issue(i, carry):
            row_copy(0, i).start()
            return carry

        lax.fori_loop(0, bm, issue, 0)

    @pl.when(f == 0)
    def _():
        pltpu.make_async_copy(h_hbm.at[pl.ds(0, bm)], xp_ref, sem).wait()
        half_d = xb_ref.shape[1] // 2
        packed = pltpu.einshape("tcl->ctl", xp_ref[...])
        for c in range(xp_ref.shape[1]):
            words = packed[c]
            cols = slice(c * LANES, (c + 1) * LANES)
            xb_ref[:, cols] = pltpu.bitcast(words << 16, F32).astype(BF16)
            xb_ref[:, half_d + c * LANES:half_d + (c + 1) * LANES] = pltpu.bitcast(
                words & jnp.uint32(0xFFFF0000), F32).astype(BF16)
        o_ref[...] = jnp.broadcast_to(b2_ref[0], o_ref.shape)

        @pl.when(m + 1 < pl.num_programs(0))
        def _():
            for i in range(bm):
                row_copy(m + 1, i).start()

    @pl.when(m < nu_ref[0])
    def _():
        x = xb_ref[...]
        hg = jnp.dot(x, w1g_ref[0], preferred_element_type=F32) + b1g_ref[0]
        hl = jnp.dot(x, w1l_ref[0], preferred_element_type=F32) + b1l_ref[0]
        x_glu = jnp.minimum(hg, SWIGLU_LIMIT)
        x_lin = jnp.clip(hl, -SWIGLU_LIMIT, SWIGLU_LIMIT)
        act = x_glu * jax.nn.sigmoid(SWIGLU_ALPHA * x_glu) * (x_lin + 1.0)
        act = act.astype(BF16)
        for c in range(0, o_ref.shape[1], OUT_CHUNK):
            o_ref[:, c:c + OUT_CHUNK] += jnp.dot(act, w2_ref[0, :, c:c + OUT_CHUNK], preferred_element_type=F32)


def _experts(h_packed, slot_token, block_expert, n_used, w1g, w1l, b1g, b1l, w2, b2, bm=MOE_BM, tf=1024):
    d = w1g.shape[1]
    n_slots = slot_token.shape[0]
    n_blocks = n_slots // bm
    ff = w1g.shape[2]
    n_f = ff // tf

    def f_eff(m, f, nu):
        return jnp.where(m < nu[0], f, n_f - 1)

    return pl.pallas_call(
        _ffn_kernel,
        grid_spec=pltpu.PrefetchScalarGridSpec(
            num_scalar_prefetch=3,
            grid=(n_blocks, n_f),
            in_specs=[
                pl.BlockSpec(memory_space=pl.ANY),
                pl.BlockSpec((1, d, tf), lambda m, f, be, nu, tok: (be[m], 0, f_eff(m, f, nu))),
                pl.BlockSpec((1, d, tf), lambda m, f, be, nu, tok: (be[m], 0, f_eff(m, f, nu))),
                pl.BlockSpec((1, 1, tf), lambda m, f, be, nu, tok: (be[m], 0, f_eff(m, f, nu))),
                pl.BlockSpec((1, 1, tf), lambda m, f, be, nu, tok: (be[m], 0, f_eff(m, f, nu))),
                pl.BlockSpec((1, tf, d), lambda m, f, be, nu, tok: (be[m], f_eff(m, f, nu), 0)),
                pl.BlockSpec((1, 1, d), lambda m, f, be, nu, tok: (be[m], 0, 0)),
            ],
            out_specs=pl.BlockSpec((bm, d), lambda m, f, be, nu, tok: (m, 0)),
            scratch_shapes=[pltpu.VMEM((bm,) + h_packed.shape[1:], jnp.uint32), pltpu.VMEM((bm, d), BF16),
                            pltpu.SemaphoreType.DMA(())]),
        out_shape=jax.ShapeDtypeStruct((n_slots, d), F32),
        compiler_params=_cparams(("arbitrary", "arbitrary")),
        name="moe_experts",
    )(block_expert, n_used, slot_token, h_packed, w1g, w1l, b1g, b1l, w2, b2)


def _combine_kernel(pos_ref, h_ref, g_ref, lw_ref, lb_ref, y_hbm, o_ref, buf, sem):
    i = pl.program_id(0)
    tm = h_ref.shape[0]

    def issue_step(step, slot):
        def group(gi, carry):
            n0 = pl.multiple_of(gi * GATHER_UNROLL, GATHER_UNROLL)
            for j in range(GATHER_UNROLL):
                for k in range(TOP_K):
                    p = pos_ref[(step * tm + n0 + j) * TOP_K + k]
                    pltpu.make_async_copy(y_hbm.at[pl.ds(p, 1), :], buf.at[slot, pl.ds(k * tm + n0 + j, 1), :],
                                          sem.at[slot]).start()
            return carry

        lax.fori_loop(0, tm // GATHER_UNROLL, group, 0)

    @pl.when(i == 0)
    def _():
        issue_step(0, 0)

    @pl.when(i + 1 < pl.num_programs(0))
    def _():
        issue_step(i + 1, (i + 1) % 2)

    slot = i % 2
    pltpu.make_async_copy(y_hbm.at[pl.ds(0, TOP_K * tm), :], buf.at[slot], sem.at[slot]).wait()
    g = g_ref[...]
    ffn = g[:, 0:1] * buf[slot, 0:tm, :]
    for k in range(1, TOP_K):
        ffn = ffn + g[:, k:k + 1] * buf[slot, k * tm:(k + 1) * tm, :]
    o_ref[...] = _layer_norm(DEEPNORM_ALPHA * h_ref[...] + ffn, lw_ref[...], lb_ref[...], LN_EPS)


def _combine(pos_flat, h, gates, ln_w, ln_b, y_slots, tm=256):
    n, d = h.shape
    return pl.pallas_call(
        _combine_kernel,
        grid_spec=pltpu.PrefetchScalarGridSpec(
            num_scalar_prefetch=1,
            grid=(n // tm,),
            in_specs=[pl.BlockSpec((tm, d), lambda i, pos: (i, 0)),
                      pl.BlockSpec((tm, LANES), lambda i, pos: (i, 0)),
                      pl.BlockSpec((1, d), lambda i, pos: (0, 0)),
                      pl.BlockSpec((1, d), lambda i, pos: (0, 0)),
                      pl.BlockSpec(memory_space=pl.ANY)],
            out_specs=pl.BlockSpec((tm, d), lambda i, pos: (i, 0)),
            scratch_shapes=[pltpu.VMEM((2, TOP_K * tm, d), F32), pltpu.SemaphoreType.DMA((2,))]),
        out_shape=jax.ShapeDtypeStruct((n, d), F32),
        compiler_params=_cparams(("arbitrary",)),
        name="moe_combine_ln",
    )(pos_flat, h, gates, ln_w, ln_b, y_slots)


def _moe(h, h_packed, route_i, gates, counts_f, w1, b1, w2, b2, ln_w, ln_b):
    n_tok, d = h.shape
    n_exp = w1.shape[0]
    n_asg = n_tok * TOP_K
    n_blocks = -(-n_asg // MOE_BM) + n_exp

    counts = counts_f[0, :n_exp].astype(jnp.int32)
    padded = (counts + MOE_BM - 1) // MOE_BM * MOE_BM
    pend = jnp.cumsum(padded)
    pstart = pend - padded
    pos = pstart[route_i[:, 0:TOP_K]] + route_i[:, TOP_K:2 * TOP_K]
    pos_flat = pos.reshape(n_asg)
    token_of = jnp.arange(n_asg, dtype=jnp.int32) // TOP_K
    slot_token = jnp.zeros((n_blocks * MOE_BM,), jnp.int32).at[pos_flat].set(token_of)
    n_used = (pend[-1] // MOE_BM).astype(jnp.int32).reshape(1)
    blk = jnp.minimum(jnp.arange(n_blocks, dtype=jnp.int32), n_used[0] - 1)
    block_expert = jnp.minimum(jnp.sum(pend[None, :] <= (blk * MOE_BM)[:, None], axis=1), n_exp - 1).astype(jnp.int32)

    w1g, w1l = _split_w1(w1)
    b1g = b1[:, None, 0::2]
    b1l = b1[:, None, 1::2]
    w2b = _cast_bf16(w2)

    y_slots = _experts(h_packed, slot_token, block_expert, n_used, w1g, w1l, b1g, b1l, w2b, b2[:, None, :])
    return _combine(pos_flat, h, gates, ln_w, ln_b, y_slots)


def _pad_to(a, axis, size):
    pad = [(0, 0)] * a.ndim
    pad[axis] = (0, size - a.shape[axis])
    return jnp.pad(a, pad)


def kernel(x, w_in, mu_shift, w0, w2_decay, a0, a2_iclr, g2_gate, k_k, k_a, r_k, lnx_w, lnx_b, w_pool, pool_scale,
           w_out, ln1_w, ln1_b, router_w, router_b, w1_exp, b1_exp, w2_exp, b2_exp, ln2_w, ln2_b):
    batch, seq, d = x.shape
    assert w_in.shape[0] == 1, "one layer"
    x2d = x.reshape(batch * seq, d)
    c_xw = RKV_COLS
    c_xa = c_xw + DECAY_LORA
    c_xg = c_xa + ICLR_LORA
    c_pool = c_xg + GATE_LORA

    def lora_layout(a):
        return jnp.concatenate([_pad_to(a[:, c_xw:c_xa], 1, LORA_XA - LORA_XW),
                                _pad_to(a[:, c_xa:c_xg], 1, LORA_XG - LORA_XA),
                                _pad_to(a[:, c_xg:c_pool], 1, LORA_PAD - LORA_XG)], axis=1)

    wi = w_in[0]
    w_p = jnp.concatenate([wi[:, :RKV_COLS], wi[:, c_pool:], lora_layout(wi)], axis=1).astype(BF16)
    mu = mu_shift[0][None, :]
    row = lambda a: a.reshape(1, -1)

    proj = _in_proj(x2d, w_p)
    y_rwkv = _rwkv_group(
        proj, batch, seq, mu[:, :RKV_COLS], lora_layout(mu), row(w0[0]),
        _pad_to(w2_decay[0], 0, LANES).astype(BF16), row(a0[0]), _pad_to(a2_iclr[0], 0, LANES).astype(BF16),
        _pad_to(g2_gate[0], 0, LORA_PAD - LORA_XG).astype(BF16), row(k_k[0]), row(k_a[0]), row(r_k[0]),
        row(lnx_w[0]), row(lnx_b[0]))
    y_pool = _pool_group(proj, batch, seq, w_pool[0].astype(BF16), row(pool_scale[0]))

    n_exp = router_w.shape[2]
    router_w_p = _pad_to(router_w[0], 1, LANES)
    router_b_p = jnp.concatenate([router_b[0], jnp.full((LANES - n_exp,), -1e30, F32)])[None, :]
    h1, h1_packed, route_i, gates, counts = _mix_ln_router(y_rwkv, y_pool, x2d, w_out[0].astype(BF16), row(ln1_w[0]),
                                                row(ln1_b[0]), router_w_p, router_b_p)
    out = _moe(h1, h1_packed, route_i, gates, counts, w1_exp[0], b1_exp[0], w2_exp[0], b2_exp[0], row(ln2_w[0]), row(ln2_b[0]))
    return out.reshape(batch, seq, d)
```

```python
import jax
import jax.numpy as jnp
from jax import lax
from jax.experimental import pallas as pl
from jax.experimental.pallas import tpu as pltpu
from jax.experimental.pallas import tpu_sc as plsc

F32 = jnp.float32
BF16 = jnp.bfloat16

RWKV_WIDTH = 1024
HEAD = 64
DECAY_LORA = 64
ICLR_LORA = 64
GATE_LORA = 160
POOL_WINDOWS = (2, 4, 8, 16)
POOL_GROUP_WIDTH = 256
TOP_K = 4
SWIGLU_LIMIT = 7.0
SWIGLU_ALPHA = 1.702
LN_EPS = 1e-5
GN_EPS = 64e-5
DEEPNORM_ALPHA = 2.0 ** 0.25

LANES = 128
LORA_PAD = 512
LORA_XW, LORA_XA, LORA_XG = 0, 128, 256
RKV_COLS = 3 * RWKV_WIDTH
POOL_WIDTH = len(POOL_WINDOWS) * POOL_GROUP_WIDTH
POOL_COL0 = RKV_COLS
LORA_COL0 = POOL_COL0 + POOL_WIDTH
PROJ_COLS = LORA_COL0 + LORA_PAD
CHUNK = 64
SUB = 16
VMEM_LIMIT = 56 * 1024 * 1024


def _cparams(sem):
    return pltpu.CompilerParams(dimension_semantics=sem, vmem_limit_bytes=VMEM_LIMIT)


def _bdot(a, b):
    return jnp.dot(a.astype(BF16), b.astype(BF16), preferred_element_type=F32)


def _bdot_nt(a, b):
    return lax.dot_general(a.astype(BF16), b.astype(BF16), (((1,), (1,)), ((), ())),
                           preferred_element_type=F32)


def _bdot_tn(a, b):
    return lax.dot_general(a.astype(BF16), b.astype(BF16), (((0,), (0,)), ((), ())),
                           preferred_element_type=F32)


def _in_proj_kernel(x_ref, w_ref, o_ref, xb_ref):
    @pl.when(pl.program_id(1) == 0)
    def _():
        xb_ref[...] = x_ref[...].astype(BF16)

    o_ref[...] = jnp.dot(xb_ref[...], w_ref[...], preferred_element_type=F32)


def _in_proj(x2d, w_p, tm=1024, tn=768):
    n, d = x2d.shape
    cols = w_p.shape[1]
    return pl.pallas_call(
        _in_proj_kernel,
        grid=(n // tm, cols // tn),
        in_specs=[pl.BlockSpec((tm, d), lambda i, j: (i, 0)),
                  pl.BlockSpec((d, tn), lambda i, j: (0, j))],
        out_specs=pl.BlockSpec((tm, tn), lambda i, j: (i, j)),
        out_shape=jax.ShapeDtypeStruct((n, cols), F32),
        scratch_shapes=[pltpu.VMEM((tm, d), BF16)],
        compiler_params=_cparams(("arbitrary", "arbitrary")),
        name="in_proj",
    )(x2d, w_p)


def _shifted(x_ref, prev_ref, mu_ref, first_row):
    x = x_ref[...]
    xp = jnp.where(first_row, prev_ref[...], pltpu.roll(x, 1, axis=0))
    prev_ref[...] = x_ref[x.shape[0] - 1:x.shape[0], :]
    return x + (xp - x) * mu_ref[...]


def _rwkv_kernel(r_ref, k_ref, v_ref, lo_ref, mur_ref, muk_ref, muv_ref, mul_ref,
                 w0_ref, w2d_ref, a0_ref, a2_ref, g2_ref, kk_ref, ka_ref, rk_ref, lnw_ref, lnb_ref,
                 o_ref,
                 pr_ref, pk_ref, pv_ref, plo_ref, s_ref,
                 rs_ref, lw_ref, ks_ref, vs_ref, kks_ref, as_ref, gs_ref):
    tb, width = r_ref.shape
    n_pairs = width // LANES
    c2 = 2 * CHUNK

    @pl.when(pl.program_id(2) == 0)
    def _():
        pr_ref[...] = jnp.zeros_like(pr_ref)
        pk_ref[...] = jnp.zeros_like(pk_ref)
        pv_ref[...] = jnp.zeros_like(pv_ref)
        plo_ref[...] = jnp.zeros_like(plo_ref)
        s_ref[...] = jnp.zeros_like(s_ref)

    first_row = lax.broadcasted_iota(jnp.int32, (tb, 1), 0) == 0
    r = _shifted(r_ref, pr_ref, mur_ref, first_row)
    k = _shifted(k_ref, pk_ref, muk_ref, first_row)
    v = _shifted(v_ref, pv_ref, muv_ref, first_row)
    lo = _shifted(lo_ref, plo_ref, mul_ref, first_row)

    xw = jnp.tanh(lo[:, LORA_XW:LORA_XW + LANES])
    xa = lo[:, LORA_XA:LORA_XA + LANES]
    xg = jax.nn.sigmoid(lo[:, LORA_XG:LORA_PAD])
    z = w0_ref[...] + _bdot(xw, w2d_ref[...])
    w_logit = -(jnp.maximum(-z, 0.0) + jnp.log(1.0 + jnp.exp(-jnp.abs(z)))) - 0.5
    a_ic = jax.nn.sigmoid(a0_ref[...] + _bdot(xa, a2_ref[...]))
    rs_ref[...] = r
    lw_ref[...] = -jnp.exp(w_logit)
    ks_ref[...] = k * (1.0 + (a_ic - 1.0) * ka_ref[...])
    vs_ref[...] = v
    kks_ref[...] = k * kk_ref[...]
    as_ref[...] = a_ic
    gs_ref[...] = _bdot(xg, g2_ref[...])

    lane = lax.broadcasted_iota(jnp.int32, (CHUNK, LANES), 1)
    head0 = lane < HEAD
    row = lax.broadcasted_iota(jnp.int32, (c2, c2), 0)
    col = lax.broadcasted_iota(jnp.int32, (c2, c2), 1)
    strict = row > col
    incl = row >= col
    same_sub = (row // SUB) == (col // SUB)
    eye = (row == col).astype(F32)
    tri = (lax.broadcasted_iota(jnp.int32, (CHUNK, CHUNK), 0)
           >= lax.broadcasted_iota(jnp.int32, (CHUNK, CHUNK), 1)).astype(F32)
    def head_sum(x):
        s0 = jnp.sum(jnp.where(head0, x, 0.0), axis=-1, keepdims=True)
        s1 = jnp.sum(jnp.where(head0, 0.0, x), axis=-1, keepdims=True)
        return jnp.where(head0, s0, s1)

    def stack(x):
        return jnp.concatenate([jnp.where(head0, x, 0.0), jnp.where(head0, 0.0, x)], axis=0)

    def operands(rc, lw, lwi, kc, vc, kk, ac):
        kkn = kk * lax.rsqrt(jnp.maximum(head_sum(kk * kk), 1e-24))
        b_ = kkn * ac
        tot = lwi[CHUNK - 1:CHUNK, :]
        e_inv = jnp.exp(-lwi)
        e_out = jnp.exp(tot - lwi)
        at = stack(-kkn * jnp.exp(lwi - lw)).astype(BF16)
        rt = stack(rc * jnp.exp(lwi)).astype(BF16)
        bk = jnp.concatenate([stack(b_ * e_inv), stack(kc * e_inv)], axis=0).astype(BF16)
        bkh = jnp.concatenate([stack(b_ * e_out), stack(kc * e_out)], axis=0).astype(BF16)
        return at, rt, bk, bkh, stack(vc).astype(BF16), jnp.exp(tot)

    def finish(y_st, rc, kc, vc, gate, rk, lnw, lnb):
        y = y_st[:CHUNK, :] + y_st[CHUNK:, :]
        mu = head_sum(y) * (1.0 / HEAD)
        yc = y - mu
        var = head_sum(yc * yc) * (1.0 / HEAD)
        yn = yc * lax.rsqrt(var + GN_EPS) * lnw + lnb
        yn = yn + head_sum(rc * kc * rk) * vc
        return (yn * gate).astype(o_ref.dtype)

    def chunk(c, carry):
        sl = pl.ds(pl.multiple_of(c * CHUNK, CHUNK), CHUNK)
        lw = lw_ref[sl, :]
        lwi = jnp.dot(tri, lw, preferred_element_type=F32, precision=lax.Precision.HIGHEST)
        pairs = range(n_pairs)
        ls = [slice(p * LANES, (p + 1) * LANES) for p in pairs]
        ops = [operands(rs_ref[sl, l], lw[:, l], lwi[:, l], ks_ref[sl, l], vs_ref[sl, l], kks_ref[sl, l],
                        as_ref[sl, l]) for l in ls]
        at, rt, bk, bkh, vst, dec = (list(z) for z in zip(*ops))
        a_all = [_bdot_nt(jnp.concatenate([at[p], rt[p]], axis=0), bk[p]) for p in pairs]
        a_ab = [jnp.where(strict, a[:c2, :c2], 0.0) for a in a_all]
        a_ak = [jnp.where(strict, a[:c2, c2:], 0.0) for a in a_all]
        a_r = [jnp.where(jnp.concatenate([incl, incl], axis=1), a[c2:, :], 0.0).astype(BF16) for a in a_all]
        akv = [_bdot(a_ak[p], vst[p]).astype(BF16) for p in pairs]

        d1 = [jnp.where(same_sub, a, 0.0) for a in a_ab]
        l1 = [a_ab[p] - d1[p] for p in pairs]
        d2 = [_bdot(d, d) for d in d1]
        d4 = [_bdot(d, d) for d in d2]
        p12 = [_bdot(eye + d1[p], eye + d2[p]) for p in pairs]
        d8 = [_bdot(d, d) for d in d4]
        p48 = [_bdot(eye + d4[p], eye + d8[p]) for p in pairs]
        t_d = [_bdot(p12[p], p48[p]).astype(BF16) for p in pairs]
        m1 = [_bdot(t_d[p], l1[p]) for p in pairs]
        m2 = [_bdot(m, m) for m in m1]
        q = [_bdot(eye + m1[p], eye + m2[p]) for p in pairs]
        t_inv = [_bdot(q[p], t_d[p]) for p in pairs]
        tx = [_bdot(t_inv[p], jnp.concatenate([at[p], akv[p]], axis=1)) for p in pairs]

        s_bd = [s_ref[p] for p in pairs]
        wr = [_bdot_nt(jnp.concatenate([tx[p][:, :LANES].astype(BF16), rt[p]], axis=0), s_bd[p]) for p in pairs]
        uv = [jnp.concatenate([(wr[p][:c2, :] + tx[p][:, LANES:]).astype(BF16), vst[p]], axis=0) for p in pairs]
        y_st = [wr[p][c2:, :] + _bdot(a_r[p], uv[p]) for p in pairs]
        for p in pairs:
            s_ref[p] = s_bd[p] * dec[p] + _bdot_tn(uv[p], bkh[p])
        for p, l in zip(pairs, ls):
            o_ref[sl, l] = finish(y_st[p], rs_ref[sl, l], ks_ref[sl, l], vs_ref[sl, l], gs_ref[sl, l],
                                  rk_ref[:, l], lnw_ref[:, l], lnb_ref[:, l])
        return carry

    lax.fori_loop(0, tb // CHUNK, chunk, 0)


RWKV_PAIRS_PER_STEP = 8


def _rwkv_group(proj, batch, seq, mu_rkv, mu_lora, w0, w2d, a0, a2, g2, k_k, k_a, r_k, lnx_w, lnx_b, tb=512,
                pairs=RWKV_PAIRS_PER_STEP):
    width = pairs * LANES
    n_g = RWKV_WIDTH // width
    n_tb = seq // tb
    row = lambda b, h, t: b * n_tb + t
    vec = lambda off: pl.BlockSpec((1, width), lambda b, h, t: (0, off + h))
    lora_blk = LORA_COL0 // LORA_PAD
    in_specs = [
        pl.BlockSpec((tb, width), lambda b, h, t: (row(b, h, t), h)),
        pl.BlockSpec((tb, width), lambda b, h, t: (row(b, h, t), n_g + h)),
        pl.BlockSpec((tb, width), lambda b, h, t: (row(b, h, t), 2 * n_g + h)),
        pl.BlockSpec((tb, LORA_PAD), lambda b, h, t: (row(b, h, t), lora_blk)),
        vec(0), vec(n_g), vec(2 * n_g),
        pl.BlockSpec((1, LORA_PAD), lambda b, h, t: (0, 0)),
        vec(0),
        pl.BlockSpec((LANES, width), lambda b, h, t: (0, h)),
        vec(0),
        pl.BlockSpec((LANES, width), lambda b, h, t: (0, h)),
        pl.BlockSpec((LORA_PAD - LORA_XG, width), lambda b, h, t: (0, h)),
        vec(0), vec(0), vec(0), vec(0), vec(0),
    ]
    blk = lambda: pltpu.VMEM((tb, width), F32)
    return pl.pallas_call(
        _rwkv_kernel,
        grid=(batch, n_g, n_tb),
        in_specs=in_specs,
        out_specs=pl.BlockSpec((tb, width), lambda b, h, t: (row(b, h, t), h)),
        out_shape=jax.ShapeDtypeStruct((batch * seq, RWKV_WIDTH), BF16),
        scratch_shapes=[pltpu.VMEM((1, width), F32), pltpu.VMEM((1, width), F32), pltpu.VMEM((1, width), F32),
                        pltpu.VMEM((1, LORA_PAD), F32), pltpu.VMEM((pairs, 2 * CHUNK, LANES), F32),
                        blk(), blk(), blk(), blk(), blk(), blk(), blk()],
        compiler_params=_cparams(("arbitrary", "arbitrary", "arbitrary")),
        name="rwkv7_group",
    )(proj, proj, proj, proj, mu_rkv, mu_rkv, mu_rkv, mu_lora, w0, w2d, a0, a2, g2, k_k, k_a, r_k, lnx_w, lnx_b)


HALO = 16


def _pool_kernel(p_ref, w_ref, sc_ref, o_ref, ext_ref):
    tb = p_ref.shape[0]
    t = pl.program_id(1)

    @pl.when(t == 0)
    def _():
        ext_ref[0:HALO, :] = jnp.zeros((HALO, ext_ref.shape[1]), F32)

    @pl.when(t > 0)
    def _():
        ext_ref[0:HALO, :] = ext_ref[tb:tb + HALO, :]

    ext_ref[HALO:HALO + tb, :] = p_ref[...]
    t_idx = t * tb + lax.broadcasted_iota(jnp.int32, (tb, 1), 0)
    for gi, win in enumerate(POOL_WINDOWS):
        cs = slice(gi * POOL_GROUP_WIDTH, (gi + 1) * POOL_GROUP_WIDTH)
        acc = ext_ref[HALO:HALO + tb, cs]
        for d in range(1, win):
            acc = acc + ext_ref[HALO - d:HALO - d + tb, cs]
        count = jnp.minimum(t_idx + 1, win).astype(F32)
        pooled = acc / count - ext_ref[HALO:HALO + tb, cs]
        mixed = _bdot(pooled, w_ref[gi])
        o_ref[:, cs] = (mixed * sc_ref[:, cs]).astype(o_ref.dtype)


def _pool_group(proj, batch, seq, w_pool_b, pool_scale, tb=512):
    n_tb = seq // tb
    col_blk = POOL_COL0 // POOL_WIDTH
    return pl.pallas_call(
        _pool_kernel,
        grid=(batch, n_tb),
        in_specs=[pl.BlockSpec((tb, POOL_WIDTH), lambda b, t: (b * n_tb + t, col_blk)),
                  pl.BlockSpec(w_pool_b.shape, lambda b, t: (0, 0, 0)),
                  pl.BlockSpec((1, POOL_WIDTH), lambda b, t: (0, 0))],
        out_specs=pl.BlockSpec((tb, POOL_WIDTH), lambda b, t: (b * n_tb + t, 0)),
        out_shape=jax.ShapeDtypeStruct((batch * seq, POOL_WIDTH), BF16),
        scratch_shapes=[pltpu.VMEM((tb + HALO, POOL_WIDTH), F32)],
        compiler_params=_cparams(("arbitrary", "arbitrary")),
        name="pool_group",
    )(proj, w_pool_b, pool_scale)


def _layer_norm(x, w, b, eps):
    mu = jnp.mean(x, axis=-1, keepdims=True)
    xc = x - mu
    var = jnp.mean(xc * xc, axis=-1, keepdims=True)
    return xc * lax.rsqrt(var + eps) * w + b


def _mix_kernel(yr_ref, yp_ref, x_ref, wo_ref, lw_ref, lb_ref, rwh_ref, rwl_ref, rb_ref,
                h_ref, hp_ref, ri_ref, g_ref, cnt_ref, carry_ref):
    tm = x_ref.shape[0]
    half = yr_ref.shape[1]

    @pl.when(pl.program_id(0) == 0)
    def _():
        carry_ref[...] = jnp.zeros_like(carry_ref)

    mix = (jnp.dot(yr_ref[...], wo_ref[0:half, :], preferred_element_type=F32)
           + jnp.dot(yp_ref[...], wo_ref[half:, :], preferred_element_type=F32))
    h = _layer_norm(DEEPNORM_ALPHA * x_ref[...] + mix, lw_ref[...], lb_ref[...], LN_EPS)
    h_ref[...] = h
    half_d = h.shape[1] // 2
    lo = pltpu.bitcast(h[:, :half_d].astype(BF16).astype(F32), jnp.uint32) >> 16
    hi = pltpu.bitcast(h[:, half_d:].astype(BF16).astype(F32), jnp.uint32)
    hp_ref[...] = lo | hi

    h_hi = h.astype(BF16)
    h_lo = (h - h_hi.astype(F32)).astype(BF16)
    logits = (jnp.dot(h_hi, rwh_ref[...], preferred_element_type=F32)
              + jnp.dot(h_lo, rwh_ref[...], preferred_element_type=F32)
              + jnp.dot(h_hi, rwl_ref[...], preferred_element_type=F32)) + rb_ref[...]
    lane = lax.broadcasted_iota(jnp.int32, (tm, LANES), 1).astype(F32)
    idxs, vals = [], []
    left = logits
    for _ in range(TOP_K):
        m = jnp.max(left, axis=-1, keepdims=True)
        idx = jnp.min(jnp.where(left == m, lane, float(LANES)), axis=-1, keepdims=True)
        idxs.append(idx)
        vals.append(m)
        left = jnp.where(lane == idx, -jnp.inf, left)
    exps = [jnp.exp(v - vals[0]) for v in vals]
    denom = exps[0] + exps[1] + exps[2] + exps[3]

    onehot = jnp.zeros((tm, LANES), F32)
    for idx in idxs:
        onehot = onehot + (lane == idx).astype(F32)
    tri = (lax.broadcasted_iota(jnp.int32, (tm, tm), 0)
           > lax.broadcasted_iota(jnp.int32, (tm, tm), 1)).astype(BF16)
    before = jnp.dot(tri, onehot.astype(BF16), preferred_element_type=F32) + carry_ref[...]
    carry_ref[...] = carry_ref[...] + jnp.sum(onehot, axis=0, keepdims=True)
    cnt_ref[...] = carry_ref[...]

    ri = jnp.zeros((tm, LANES), F32)
    gt = jnp.zeros((tm, LANES), F32)
    for k in range(TOP_K):
        rank = jnp.sum(jnp.where(lane == idxs[k], before, 0.0), axis=-1, keepdims=True)
        ri = jnp.where(lane == float(k), idxs[k], ri)
        ri = jnp.where(lane == float(TOP_K + k), rank, ri)
        gt = jnp.where(lane == float(k), exps[k] / denom, gt)
    ri_ref[...] = ri.astype(jnp.int32)
    g_ref[...] = gt


def _mix_ln_router(y_rwkv, y_pool, x2d, w_out_b, ln_w, ln_b, router_w_p, router_b_p, tm=512):
    n, d = x2d.shape
    rw_hi = router_w_p.astype(BF16)
    rw_lo = (router_w_p - rw_hi.astype(F32)).astype(BF16)
    half = y_rwkv.shape[1]
    const = lambda shape: pl.BlockSpec(shape, lambda i: (0, 0))
    return pl.pallas_call(
        _mix_kernel,
        grid=(n // tm,),
        in_specs=[pl.BlockSpec((tm, half), lambda i: (i, 0)),
                  pl.BlockSpec((tm, y_pool.shape[1]), lambda i: (i, 0)),
                  pl.BlockSpec((tm, d), lambda i: (i, 0)),
                  const(w_out_b.shape), const((1, d)), const((1, d)),
                  const(router_w_p.shape), const(router_w_p.shape), const((1, LANES))],
        out_specs=[pl.BlockSpec((tm, d), lambda i: (i, 0)),
                   pl.BlockSpec((tm, d // 2), lambda i: (i, 0)),
                   pl.BlockSpec((tm, LANES), lambda i: (i, 0)),
                   pl.BlockSpec((tm, LANES), lambda i: (i, 0)),
                   const((1, LANES))],
        out_shape=[jax.ShapeDtypeStruct((n, d), F32),
                   jax.ShapeDtypeStruct((n, d // 2), jnp.uint32),
                   jax.ShapeDtypeStruct((n, LANES), jnp.int32),
                   jax.ShapeDtypeStruct((n, LANES), F32),
                   jax.ShapeDtypeStruct((1, LANES), F32)],
        scratch_shapes=[pltpu.VMEM((1, LANES), F32)],
        compiler_params=_cparams(("arbitrary",)),
        name="mix_ln_router",
    )(y_rwkv, y_pool, x2d, w_out_b, ln_w, ln_b, rw_hi, rw_lo, router_b_p)


MOE_BM = 512
GATHER_UNROLL = 8


MXU_DIM = 256
OUT_CHUNK = 512


def _split_w1_kernel(w_ref, g_ref, l_ref):
    half = MXU_DIM // 2
    src = lax.broadcasted_iota(jnp.int32, (MXU_DIM, MXU_DIM), 0)
    dst = lax.broadcasted_iota(jnp.int32, (MXU_DIM, MXU_DIM), 1)
    perm = (src == jnp.where(dst < half, 2 * dst, 2 * (dst - half) + 1)).astype(BF16)
    for c in range(w_ref.shape[2] // MXU_DIM):
        w = w_ref[0, :, c * MXU_DIM:(c + 1) * MXU_DIM].astype(BF16)
        o = jnp.dot(w, perm, preferred_element_type=F32)
        g_ref[0, :, c * half:(c + 1) * half] = o[:, :half].astype(BF16)
        l_ref[0, :, c * half:(c + 1) * half] = o[:, half:].astype(BF16)


def _split_w1(w1, tr=512, tc=2048):
    n_exp, d, f2 = w1.shape
    out = jax.ShapeDtypeStruct((n_exp, d, f2 // 2), BF16)
    return pl.pallas_call(
        _split_w1_kernel,
        grid=(n_exp, d // tr, f2 // tc),
        in_specs=[pl.BlockSpec((1, tr, tc), lambda e, i, j: (e, i, j))],
        out_specs=[pl.BlockSpec((1, tr, tc // 2), lambda e, i, j: (e, i, j)),
                   pl.BlockSpec((1, tr, tc // 2), lambda e, i, j: (e, i, j))],
        out_shape=[out, out],
        compiler_params=_cparams(("arbitrary", "arbitrary", "arbitrary")),
        name="split_w1",
    )(w1)


def _cast_kernel(w_ref, o_ref):
    o_ref[...] = w_ref[...].astype(o_ref.dtype)


def _cast_bf16(w, tr=512):
    n_exp, r, c = w.shape
    return pl.pallas_call(
        _cast_kernel,
        grid=(n_exp, r // tr),
        in_specs=[pl.BlockSpec((1, tr, c), lambda e, i: (e, i, 0))],
        out_specs=pl.BlockSpec((1, tr, c), lambda e, i: (e, i, 0)),
        out_shape=jax.ShapeDtypeStruct(w.shape, BF16),
        compiler_params=_cparams(("arbitrary", "arbitrary")),
        name="cast_w2",
    )(w)


SC_WINDOW = 128
SC_COLS = 256


def _dispatch(h_packed, slot_token):
    n_slots = slot_token.shape[0]
    width = h_packed.shape[1]
    mesh = plsc.VectorSubcoreMesh(core_axis_name="c", subcore_axis_name="s")
    n_workers = mesh.num_cores * mesh.num_subcores
    n_windows = n_slots // SC_WINDOW
    assert n_slots % SC_WINDOW == 0 and n_windows % n_workers == 0 and width % SC_COLS == 0

    @pl.kernel(out_type=jax.ShapeDtypeStruct((n_slots, width), h_packed.dtype), mesh=mesh,
               scratch_types=[pltpu.VMEM((SC_WINDOW,), jnp.int32), pltpu.VMEM((SC_WINDOW, SC_COLS), h_packed.dtype)])
    def gather(h_hbm, tok_hbm, x_hbm, idx_ref, buf_ref):
        worker = lax.axis_index("c") * mesh.num_subcores + lax.axis_index("s")

        @pl.loop(0, n_windows // n_workers)
        def _(t):
            rows = pl.ds((t * n_workers + worker) * SC_WINDOW, SC_WINDOW)
            pltpu.sync_copy(tok_hbm.at[rows], idx_ref)
            for c in range(0, width, SC_COLS):
                pltpu.sync_copy(h_hbm.at[idx_ref, pl.ds(c, SC_COLS)], buf_ref)
                pltpu.sync_copy(buf_ref, x_hbm.at[rows, pl.ds(c, SC_COLS)])

    return gather(h_packed, slot_token)


def _ffn_kernel(be_ref, nu_ref, x_ref, w1g_ref, w1l_ref, b1g_ref, b1l_ref, w2_ref, b2_ref, o_ref, xb_ref):
    m = pl.program_id(0)
    f = pl.program_id(1)

    @pl.when(f == 0)
    def _():
        half_d = xb_ref.shape[1] // 2
        words = x_ref[...]
        xb_ref[:, :half_d] = pltpu.bitcast(words << 16, F32).astype(BF16)
        xb_ref[:, half_d:] = pltpu.bitcast(words & jnp.uint32(0xFFFF0000), F32).astype(BF16)
        o_ref[...] = jnp.broadcast_to(b2_ref[0], o_ref.shape)

    @pl.when(m < nu_ref[0])
    def _():
        x = xb_ref[...]
        hg = jnp.dot(x, w1g_ref[0], preferred_element_type=F32) + b1g_ref[0]
        hl = jnp.dot(x, w1l_ref[0], preferred_element_type=F32) + b1l_ref[0]
        x_glu = jnp.minimum(hg, SWIGLU_LIMIT)
        x_lin = jnp.clip(hl, -SWIGLU_LIMIT, SWIGLU_LIMIT)
        act = x_glu * jax.nn.sigmoid(SWIGLU_ALPHA * x_glu) * (x_lin + 1.0)
        act = act.astype(BF16)
        for c in range(0, o_ref.shape[1], OUT_CHUNK):
            o_ref[:, c:c + OUT_CHUNK] += jnp.dot(act, w2_ref[0, :, c:c + OUT_CHUNK], preferred_element_type=F32)


def _experts(x_slots, block_expert, n_used, w1g, w1l, b1g, b1l, w2, b2, bm=MOE_BM, tf=1024):
    d = w1g.shape[1]
    n_slots = x_slots.shape[0]
    n_blocks = n_slots // bm
    ff = w1g.shape[2]
    n_f = ff // tf

    def m_eff(m, nu):
        return jnp.minimum(m, jnp.maximum(nu[0] - 1, 0))

    def f_eff(m, f, nu):
        return jnp.where(m < nu[0], f, n_f - 1)

    return pl.pallas_call(
        _ffn_kernel,
        grid_spec=pltpu.PrefetchScalarGridSpec(
            num_scalar_prefetch=2,
            grid=(n_blocks, n_f),
            in_specs=[
                pl.BlockSpec((bm, d // 2), lambda m, f, be, nu: (m_eff(m, nu), 0)),
                pl.BlockSpec((1, d, tf), lambda m, f, be, nu: (be[m], 0, f_eff(m, f, nu))),
                pl.BlockSpec((1, d, tf), lambda m, f, be, nu: (be[m], 0, f_eff(m, f, nu))),
                pl.BlockSpec((1, 1, tf), lambda m, f, be, nu: (be[m], 0, f_eff(m, f, nu))),
                pl.BlockSpec((1, 1, tf), lambda m, f, be, nu: (be[m], 0, f_eff(m, f, nu))),
                pl.BlockSpec((1, tf, d), lambda m, f, be, nu: (be[m], f_eff(m, f, nu), 0)),
                pl.BlockSpec((1, 1, d), lambda m, f, be, nu: (be[m], 0, 0)),
            ],
            out_specs=pl.BlockSpec((bm, d), lambda m, f, be, nu: (m, 0)),
            scratch_shapes=[pltpu.VMEM((bm, d), BF16)]),
        out_shape=jax.ShapeDtypeStruct((n_slots, d), F32),
        compiler_params=_cparams(("arbitrary", "arbitrary")),
        name="moe_experts",
    )(block_expert, n_used, x_slots, w1g, w1l, b1g, b1l, w2, b2)


def _combine_kernel(pos_ref, h_ref, g_ref, lw_ref, lb_ref, y_hbm, o_ref, buf, sem):
    i = pl.program_id(0)
    tm = h_ref.shape[0]

    def issue_step(step, slot):
        def group(gi, carry):
            n0 = pl.multiple_of(gi * GATHER_UNROLL, GATHER_UNROLL)
            for j in range(GATHER_UNROLL):
                for k in range(TOP_K):
                    p = pos_ref[(step * tm + n0 + j) * TOP_K + k]
                    pltpu.make_async_copy(y_hbm.at[pl.ds(p, 1), :], buf.at[slot, pl.ds(k * tm + n0 + j, 1), :],
                                          sem.at[slot]).start()
            return carry

        lax.fori_loop(0, tm // GATHER_UNROLL, group, 0)

    @pl.when(i == 0)
    def _():
        issue_step(0, 0)

    @pl.when(i + 1 < pl.num_programs(0))
    def _():
        issue_step(i + 1, (i + 1) % 2)

    slot = i % 2
    pltpu.make_async_copy(y_hbm.at[pl.ds(0, TOP_K * tm), :], buf.at[slot], sem.at[slot]).wait()
    g = g_ref[...]
    ffn = g[:, 0:1] * buf[slot, 0:tm, :]
    for k in range(1, TOP_K):
        ffn = ffn + g[:, k:k + 1] * buf[slot, k * tm:(k + 1) * tm, :]
    o_ref[...] = _layer_norm(DEEPNORM_ALPHA * h_ref[...] + ffn, lw_ref[...], lb_ref[...], LN_EPS)


def _combine(pos_flat, h, gates, ln_w, ln_b, y_slots, tm=256):
    n, d = h.shape
    return pl.pallas_call(
        _combine_kernel,
        grid_spec=pltpu.PrefetchScalarGridSpec(
            num_scalar_prefetch=1,
            grid=(n // tm,),
            in_specs=[pl.BlockSpec((tm, d), lambda i, pos: (i, 0)),
                      pl.BlockSpec((tm, LANES), lambda i, pos: (i, 0)),
                      pl.BlockSpec((1, d), lambda i, pos: (0, 0)),
                      pl.BlockSpec((1, d), lambda i, pos: (0, 0)),
                      pl.BlockSpec(memory_space=pl.ANY)],
            out_specs=pl.BlockSpec((tm, d), lambda i, pos: (i, 0)),
            scratch_shapes=[pltpu.VMEM((2, TOP_K * tm, d), F32), pltpu.SemaphoreType.DMA((2,))]),
        out_shape=jax.ShapeDtypeStruct((n, d), F32),
        compiler_params=_cparams(("arbitrary",)),
        name="moe_combine_ln",
    )(pos_flat, h, gates, ln_w, ln_b, y_slots)


def _moe(h, h_packed, route_i, gates, counts_f, w1, b1, w2, b2, ln_w, ln_b):
    n_tok, d = h.shape
    n_exp = w1.shape[0]
    n_asg = n_tok * TOP_K
    n_blocks = -(-n_asg // MOE_BM) + n_exp

    counts = counts_f[0, :n_exp].astype(jnp.int32)
    padded = (counts + MOE_BM - 1) // MOE_BM * MOE_BM
    pend = jnp.cumsum(padded)
    pstart = pend - padded
    pos = pstart[route_i[:, 0:TOP_K]] + route_i[:, TOP_K:2 * TOP_K]
    pos_flat = pos.reshape(n_asg)
    token_of = jnp.arange(n_asg, dtype=jnp.int32) // TOP_K
    slot_token = jnp.zeros((n_blocks * MOE_BM,), jnp.int32).at[pos_flat].set(token_of)
    n_used = (pend[-1] // MOE_BM).astype(jnp.int32).reshape(1)
    blk = jnp.minimum(jnp.arange(n_blocks, dtype=jnp.int32), n_used[0] - 1)
    block_expert = jnp.minimum(jnp.sum(pend[None, :] <= (blk * MOE_BM)[:, None], axis=1), n_exp - 1).astype(jnp.int32)

    w1g, w1l = _split_w1(w1)
    b1g = b1[:, None, 0::2]
    b1l = b1[:, None, 1::2]
    w2b = _cast_bf16(w2)

    x_slots = _dispatch(h_packed, slot_token)
    y_slots = _experts(x_slots, block_expert, n_used, w1g, w1l, b1g, b1l, w2b, b2[:, None, :])
    return _combine(pos_flat, h, gates, ln_w, ln_b, y_slots)


def _pad_to(a, axis, size):
    pad = [(0, 0)] * a.ndim
    pad[axis] = (0, size - a.shape[axis])
    return jnp.pad(a, pad)


def kernel(x, w_in, mu_shift, w0, w2_decay, a0, a2_iclr, g2_gate, k_k, k_a, r_k, lnx_w, lnx_b, w_pool, pool_scale,
           w_out, ln1_w, ln1_b, router_w, router_b, w1_exp, b1_exp, w2_exp, b2_exp, ln2_w, ln2_b):
    batch, seq, d = x.shape
    assert w_in.shape[0] == 1, "one layer"
    x2d = x.reshape(batch * seq, d)
    c_xw = RKV_COLS
    c_xa = c_xw + DECAY_LORA
    c_xg = c_xa + ICLR_LORA
    c_pool = c_xg + GATE_LORA

    def lora_layout(a):
        return jnp.concatenate([_pad_to(a[:, c_xw:c_xa], 1, LORA_XA - LORA_XW),
                                _pad_to(a[:, c_xa:c_xg], 1, LORA_XG - LORA_XA),
                                _pad_to(a[:, c_xg:c_pool], 1, LORA_PAD - LORA_XG)], axis=1)

    wi = w_in[0]
    w_p = jnp.concatenate([wi[:, :RKV_COLS], wi[:, c_pool:], lora_layout(wi)], axis=1).astype(BF16)
    mu = mu_shift[0][None, :]
    row = lambda a: a.reshape(1, -1)

    proj = _in_proj(x2d, w_p)
    y_rwkv = _rwkv_group(
        proj, batch, seq, mu[:, :RKV_COLS], lora_layout(mu), row(w0[0]),
        _pad_to(w2_decay[0], 0, LANES).astype(BF16), row(a0[0]), _pad_to(a2_iclr[0], 0, LANES).astype(BF16),
        _pad_to(g2_gate[0], 0, LORA_PAD - LORA_XG).astype(BF16), row(k_k[0]), row(k_a[0]), row(r_k[0]),
        row(lnx_w[0]), row(lnx_b[0]))
    y_pool = _pool_group(proj, batch, seq, w_pool[0].astype(BF16), row(pool_scale[0]))

    n_exp = router_w.shape[2]
    router_w_p = _pad_to(router_w[0], 1, LANES)
    router_b_p = jnp.concatenate([router_b[0], jnp.full((LANES - n_exp,), -1e30, F32)])[None, :]
    h1, h1_packed, route_i, gates, counts = _mix_ln_router(y_rwkv, y_pool, x2d, w_out[0].astype(BF16), row(ln1_w[0]),
                                                row(ln1_b[0]), router_w_p, router_b_p)
    out = _moe(h1, h1_packed, route_i, gates, counts, w1_exp[0], b1_exp[0], w2_exp[0], b2_exp[0], row(ln2_w[0]), row(ln2_b[0]))
    return out.reshape(batch, seq, d)
```

```python
import jax
import jax.numpy as jnp
from jax import lax
from jax.experimental import pallas as pl
from jax.experimental.pallas import tpu as pltpu

F32 = jnp.float32
BF16 = jnp.bfloat16

RWKV_WIDTH = 1024
HEAD = 64
DECAY_LORA = 64
ICLR_LORA = 64
GATE_LORA = 160
POOL_WINDOWS = (2, 4, 8, 16)
POOL_GROUP_WIDTH = 256
TOP_K = 4
SWIGLU_LIMIT = 7.0
SWIGLU_ALPHA = 1.702
LN_EPS = 1e-5
GN_EPS = 64e-5
DEEPNORM_ALPHA = 2.0 ** 0.25

LANES = 128
LORA_PAD = 512
LORA_XW, LORA_XA, LORA_XG = 0, 128, 256
RKV_COLS = 3 * RWKV_WIDTH
POOL_WIDTH = len(POOL_WINDOWS) * POOL_GROUP_WIDTH
POOL_COL0 = RKV_COLS
LORA_COL0 = POOL_COL0 + POOL_WIDTH
PROJ_COLS = LORA_COL0 + LORA_PAD
CHUNK = 64
SUB = 16
VMEM_LIMIT = 56 * 1024 * 1024


def _cparams(sem):
    return pltpu.CompilerParams(dimension_semantics=sem, vmem_limit_bytes=VMEM_LIMIT)


def _bdot(a, b):
    return jnp.dot(a.astype(BF16), b.astype(BF16), preferred_element_type=F32)


def _bdot_nt(a, b):
    return lax.dot_general(a.astype(BF16), b.astype(BF16), (((1,), (1,)), ((), ())),
                           preferred_element_type=F32)


def _bdot_tn(a, b):
    return lax.dot_general(a.astype(BF16), b.astype(BF16), (((0,), (0,)), ((), ())),
                           preferred_element_type=F32)


def _in_proj_kernel(x_ref, w_ref, o_ref, xb_ref):
    @pl.when(pl.program_id(1) == 0)
    def _():
        xb_ref[...] = x_ref[...].astype(BF16)

    o_ref[...] = jnp.dot(xb_ref[...], w_ref[...], preferred_element_type=F32)


def _in_proj(x2d, w_p, tm=1024, tn=768):
    n, d = x2d.shape
    cols = w_p.shape[1]
    return pl.pallas_call(
        _in_proj_kernel,
        grid=(n // tm, cols // tn),
        in_specs=[pl.BlockSpec((tm, d), lambda i, j: (i, 0)),
                  pl.BlockSpec((d, tn), lambda i, j: (0, j))],
        out_specs=pl.BlockSpec((tm, tn), lambda i, j: (i, j)),
        out_shape=jax.ShapeDtypeStruct((n, cols), F32),
        scratch_shapes=[pltpu.VMEM((tm, d), BF16)],
        compiler_params=_cparams(("arbitrary", "arbitrary")),
        name="in_proj",
    )(x2d, w_p)


def _shifted(x_ref, prev_ref, mu_ref, first_row):
    x = x_ref[...]
    xp = jnp.where(first_row, prev_ref[...], pltpu.roll(x, 1, axis=0))
    prev_ref[...] = x_ref[x.shape[0] - 1:x.shape[0], :]
    return x + (xp - x) * mu_ref[...]


def _rwkv_kernel(r_ref, k_ref, v_ref, lo_ref, mur_ref, muk_ref, muv_ref, mul_ref,
                 w0_ref, w2d_ref, a0_ref, a2_ref, g2_ref, kk_ref, ka_ref, rk_ref, lnw_ref, lnb_ref,
                 o_ref,
                 pr_ref, pk_ref, pv_ref, plo_ref, s_ref,
                 rs_ref, lw_ref, ks_ref, vs_ref, kks_ref, as_ref, gs_ref):
    tb, width = r_ref.shape
    n_pairs = width // LANES
    c2 = 2 * CHUNK

    @pl.when(pl.program_id(2) == 0)
    def _():
        pr_ref[...] = jnp.zeros_like(pr_ref)
        pk_ref[...] = jnp.zeros_like(pk_ref)
        pv_ref[...] = jnp.zeros_like(pv_ref)
        plo_ref[...] = jnp.zeros_like(plo_ref)
        s_ref[...] = jnp.zeros_like(s_ref)

    first_row = lax.broadcasted_iota(jnp.int32, (tb, 1), 0) == 0
    r = _shifted(r_ref, pr_ref, mur_ref, first_row)
    k = _shifted(k_ref, pk_ref, muk_ref, first_row)
    v = _shifted(v_ref, pv_ref, muv_ref, first_row)
    lo = _shifted(lo_ref, plo_ref, mul_ref, first_row)

    xw = jnp.tanh(lo[:, LORA_XW:LORA_XW + LANES])
    xa = lo[:, LORA_XA:LORA_XA + LANES]
    xg = jax.nn.sigmoid(lo[:, LORA_XG:LORA_PAD])
    z = w0_ref[...] + _bdot(xw, w2d_ref[...])
    w_logit = -(jnp.maximum(-z, 0.0) + jnp.log(1.0 + jnp.exp(-jnp.abs(z)))) - 0.5
    a_ic = jax.nn.sigmoid(a0_ref[...] + _bdot(xa, a2_ref[...]))
    rs_ref[...] = r
    lw_ref[...] = -jnp.exp(w_logit)
    ks_ref[...] = k * (1.0 + (a_ic - 1.0) * ka_ref[...])
    vs_ref[...] = v
    kks_ref[...] = k * kk_ref[...]
    as_ref[...] = a_ic
    gs_ref[...] = _bdot(xg, g2_ref[...])

    lane = lax.broadcasted_iota(jnp.int32, (CHUNK, LANES), 1)
    head0 = lane < HEAD
    row = lax.broadcasted_iota(jnp.int32, (c2, c2), 0)
    col = lax.broadcasted_iota(jnp.int32, (c2, c2), 1)
    strict = row > col
    incl = row >= col
    same_sub = (row // SUB) == (col // SUB)
    eye = (row == col).astype(F32)
    tri = (lax.broadcasted_iota(jnp.int32, (CHUNK, CHUNK), 0)
           >= lax.broadcasted_iota(jnp.int32, (CHUNK, CHUNK), 1)).astype(F32)
    def head_sum(x):
        s0 = jnp.sum(jnp.where(head0, x, 0.0), axis=-1, keepdims=True)
        s1 = jnp.sum(jnp.where(head0, 0.0, x), axis=-1, keepdims=True)
        return jnp.where(head0, s0, s1)

    def stack(x):
        return jnp.concatenate([jnp.where(head0, x, 0.0), jnp.where(head0, 0.0, x)], axis=0)

    def operands(rc, lw, lwi, kc, vc, kk, ac):
        kkn = kk * lax.rsqrt(jnp.maximum(head_sum(kk * kk), 1e-24))
        b_ = kkn * ac
        tot = lwi[CHUNK - 1:CHUNK, :]
        e_inv = jnp.exp(-lwi)
        e_out = jnp.exp(tot - lwi)
        at = stack(-kkn * jnp.exp(lwi - lw)).astype(BF16)
        rt = stack(rc * jnp.exp(lwi)).astype(BF16)
        bk = jnp.concatenate([stack(b_ * e_inv), stack(kc * e_inv)], axis=0).astype(BF16)
        bkh = jnp.concatenate([stack(b_ * e_out), stack(kc * e_out)], axis=0).astype(BF16)
        return at, rt, bk, bkh, stack(vc).astype(BF16), jnp.exp(tot)

    def finish(y_st, rc, kc, vc, gate, rk, lnw, lnb):
        y = y_st[:CHUNK, :] + y_st[CHUNK:, :]
        mu = head_sum(y) * (1.0 / HEAD)
        yc = y - mu
        var = head_sum(yc * yc) * (1.0 / HEAD)
        yn = yc * lax.rsqrt(var + GN_EPS) * lnw + lnb
        yn = yn + head_sum(rc * kc * rk) * vc
        return (yn * gate).astype(o_ref.dtype)

    def chunk(c, carry):
        sl = pl.ds(pl.multiple_of(c * CHUNK, CHUNK), CHUNK)
        lw = lw_ref[sl, :]
        lwi = jnp.dot(tri, lw, preferred_element_type=F32, precision=lax.Precision.HIGHEST)
        pairs = range(n_pairs)
        ls = [slice(p * LANES, (p + 1) * LANES) for p in pairs]
        ops = [operands(rs_ref[sl, l], lw[:, l], lwi[:, l], ks_ref[sl, l], vs_ref[sl, l], kks_ref[sl, l],
                        as_ref[sl, l]) for l in ls]
        at, rt, bk, bkh, vst, dec = (list(z) for z in zip(*ops))
        a_all = [_bdot_nt(jnp.concatenate([at[p], rt[p]], axis=0), bk[p]) for p in pairs]
        a_ab = [jnp.where(strict, a[:c2, :c2], 0.0) for a in a_all]
        a_ak = [jnp.where(strict, a[:c2, c2:], 0.0) for a in a_all]
        a_r = [jnp.where(jnp.concatenate([incl, incl], axis=1), a[c2:, :], 0.0).astype(BF16) for a in a_all]
        akv = [_bdot(a_ak[p], vst[p]).astype(BF16) for p in pairs]

        d1 = [jnp.where(same_sub, a, 0.0) for a in a_ab]
        l1 = [a_ab[p] - d1[p] for p in pairs]
        d2 = [_bdot(d, d) for d in d1]
        d4 = [_bdot(d, d) for d in d2]
        p12 = [_bdot(eye + d1[p], eye + d2[p]) for p in pairs]
        d8 = [_bdot(d, d) for d in d4]
        p48 = [_bdot(eye + d4[p], eye + d8[p]) for p in pairs]
        t_d = [_bdot(p12[p], p48[p]).astype(BF16) for p in pairs]
        m1 = [_bdot(t_d[p], l1[p]) for p in pairs]
        m2 = [_bdot(m, m) for m in m1]
        q = [_bdot(eye + m1[p], eye + m2[p]) for p in pairs]
        t_inv = [_bdot(q[p], t_d[p]) for p in pairs]
        tx = [_bdot(t_inv[p], jnp.concatenate([at[p], akv[p]], axis=1)) for p in pairs]

        s_bd = [s_ref[p] for p in pairs]
        wr = [_bdot_nt(jnp.concatenate([tx[p][:, :LANES].astype(BF16), rt[p]], axis=0), s_bd[p]) for p in pairs]
        uv = [jnp.concatenate([(wr[p][:c2, :] + tx[p][:, LANES:]).astype(BF16), vst[p]], axis=0) for p in pairs]
        y_st = [wr[p][c2:, :] + _bdot(a_r[p], uv[p]) for p in pairs]
        for p in pairs:
            s_ref[p] = s_bd[p] * dec[p] + _bdot_tn(uv[p], bkh[p])
        for p, l in zip(pairs, ls):
            o_ref[sl, l] = finish(y_st[p], rs_ref[sl, l], ks_ref[sl, l], vs_ref[sl, l], gs_ref[sl, l],
                                  rk_ref[:, l], lnw_ref[:, l], lnb_ref[:, l])
        return carry

    lax.fori_loop(0, tb // CHUNK, chunk, 0)


RWKV_PAIRS_PER_STEP = 8


def _rwkv_group(proj, batch, seq, mu_rkv, mu_lora, w0, w2d, a0, a2, g2, k_k, k_a, r_k, lnx_w, lnx_b, tb=512,
                pairs=RWKV_PAIRS_PER_STEP):
    width = pairs * LANES
    n_g = RWKV_WIDTH // width
    n_tb = seq // tb
    row = lambda b, h, t: b * n_tb + t
    vec = lambda off: pl.BlockSpec((1, width), lambda b, h, t: (0, off + h))
    lora_blk = LORA_COL0 // LORA_PAD
    in_specs = [
        pl.BlockSpec((tb, width), lambda b, h, t: (row(b, h, t), h)),
        pl.BlockSpec((tb, width), lambda b, h, t: (row(b, h, t), n_g + h)),
        pl.BlockSpec((tb, width), lambda b, h, t: (row(b, h, t), 2 * n_g + h)),
        pl.BlockSpec((tb, LORA_PAD), lambda b, h, t: (row(b, h, t), lora_blk)),
        vec(0), vec(n_g), vec(2 * n_g),
        pl.BlockSpec((1, LORA_PAD), lambda b, h, t: (0, 0)),
        vec(0),
        pl.BlockSpec((LANES, width), lambda b, h, t: (0, h)),
        vec(0),
        pl.BlockSpec((LANES, width), lambda b, h, t: (0, h)),
        pl.BlockSpec((LORA_PAD - LORA_XG, width), lambda b, h, t: (0, h)),
        vec(0), vec(0), vec(0), vec(0), vec(0),
    ]
    blk = lambda: pltpu.VMEM((tb, width), F32)
    return pl.pallas_call(
        _rwkv_kernel,
        grid=(batch, n_g, n_tb),
        in_specs=in_specs,
        out_specs=pl.BlockSpec((tb, width), lambda b, h, t: (row(b, h, t), h)),
        out_shape=jax.ShapeDtypeStruct((batch * seq, RWKV_WIDTH), BF16),
        scratch_shapes=[pltpu.VMEM((1, width), F32), pltpu.VMEM((1, width), F32), pltpu.VMEM((1, width), F32),
                        pltpu.VMEM((1, LORA_PAD), F32), pltpu.VMEM((pairs, 2 * CHUNK, LANES), F32),
                        blk(), blk(), blk(), blk(), blk(), blk(), blk()],
        compiler_params=_cparams(("arbitrary", "arbitrary", "arbitrary")),
        name="rwkv7_group",
    )(proj, proj, proj, proj, mu_rkv, mu_rkv, mu_rkv, mu_lora, w0, w2d, a0, a2, g2, k_k, k_a, r_k, lnx_w, lnx_b)


HALO = 16


def _pool_kernel(p_ref, w_ref, sc_ref, o_ref, ext_ref):
    tb = p_ref.shape[0]
    t = pl.program_id(1)

    @pl.when(t == 0)
    def _():
        ext_ref[0:HALO, :] = jnp.zeros((HALO, ext_ref.shape[1]), F32)

    @pl.when(t > 0)
    def _():
        ext_ref[0:HALO, :] = ext_ref[tb:tb + HALO, :]

    ext_ref[HALO:HALO + tb, :] = p_ref[...]
    t_idx = t * tb + lax.broadcasted_iota(jnp.int32, (tb, 1), 0)
    for gi, win in enumerate(POOL_WINDOWS):
        cs = slice(gi * POOL_GROUP_WIDTH, (gi + 1) * POOL_GROUP_WIDTH)
        acc = ext_ref[HALO:HALO + tb, cs]
        for d in range(1, win):
            acc = acc + ext_ref[HALO - d:HALO - d + tb, cs]
        count = jnp.minimum(t_idx + 1, win).astype(F32)
        pooled = acc / count - ext_ref[HALO:HALO + tb, cs]
        mixed = _bdot(pooled, w_ref[gi])
        o_ref[:, cs] = (mixed * sc_ref[:, cs]).astype(o_ref.dtype)


def _pool_group(proj, batch, seq, w_pool_b, pool_scale, tb=512):
    n_tb = seq // tb
    col_blk = POOL_COL0 // POOL_WIDTH
    return pl.pallas_call(
        _pool_kernel,
        grid=(batch, n_tb),
        in_specs=[pl.BlockSpec((tb, POOL_WIDTH), lambda b, t: (b * n_tb + t, col_blk)),
                  pl.BlockSpec(w_pool_b.shape, lambda b, t: (0, 0, 0)),
                  pl.BlockSpec((1, POOL_WIDTH), lambda b, t: (0, 0))],
        out_specs=pl.BlockSpec((tb, POOL_WIDTH), lambda b, t: (b * n_tb + t, 0)),
        out_shape=jax.ShapeDtypeStruct((batch * seq, POOL_WIDTH), BF16),
        scratch_shapes=[pltpu.VMEM((tb + HALO, POOL_WIDTH), F32)],
        compiler_params=_cparams(("arbitrary", "arbitrary")),
        name="pool_group",
    )(proj, w_pool_b, pool_scale)


def _layer_norm(x, w, b, eps):
    mu = jnp.mean(x, axis=-1, keepdims=True)
    xc = x - mu
    var = jnp.mean(xc * xc, axis=-1, keepdims=True)
    return xc * lax.rsqrt(var + eps) * w + b


def _mix_kernel(yr_ref, yp_ref, x_ref, wo_ref, lw_ref, lb_ref, rwh_ref, rwl_ref, rb_ref,
                h_ref, hp_ref, ri_ref, g_ref, cnt_ref, carry_ref):
    tm = x_ref.shape[0]
    half = yr_ref.shape[1]

    @pl.when(pl.program_id(0) == 0)
    def _():
        carry_ref[...] = jnp.zeros_like(carry_ref)

    mix = (jnp.dot(yr_ref[...], wo_ref[0:half, :], preferred_element_type=F32)
           + jnp.dot(yp_ref[...], wo_ref[half:, :], preferred_element_type=F32))
    h = _layer_norm(DEEPNORM_ALPHA * x_ref[...] + mix, lw_ref[...], lb_ref[...], LN_EPS)
    h_ref[...] = h
    half_d = h.shape[1] // 2
    lo = pltpu.bitcast(h[:, :half_d].astype(BF16).astype(F32), jnp.uint32) >> 16
    hi = pltpu.bitcast(h[:, half_d:].astype(BF16).astype(F32), jnp.uint32)
    words = lo | hi
    chunks = jnp.stack([words[:, c * LANES:(c + 1) * LANES] for c in range(hp_ref.shape[1])], axis=0)
    hp_ref[...] = pltpu.einshape("ctl->tcl", chunks)

    h_hi = h.astype(BF16)
    h_lo = (h - h_hi.astype(F32)).astype(BF16)
    logits = (jnp.dot(h_hi, rwh_ref[...], preferred_element_type=F32)
              + jnp.dot(h_lo, rwh_ref[...], preferred_element_type=F32)
              + jnp.dot(h_hi, rwl_ref[...], preferred_element_type=F32)) + rb_ref[...]
    lane = lax.broadcasted_iota(jnp.int32, (tm, LANES), 1).astype(F32)
    idxs, vals = [], []
    left = logits
    for _ in range(TOP_K):
        m = jnp.max(left, axis=-1, keepdims=True)
        idx = jnp.min(jnp.where(left == m, lane, float(LANES)), axis=-1, keepdims=True)
        idxs.append(idx)
        vals.append(m)
        left = jnp.where(lane == idx, -jnp.inf, left)
    exps = [jnp.exp(v - vals[0]) for v in vals]
    denom = exps[0] + exps[1] + exps[2] + exps[3]

    onehot = jnp.zeros((tm, LANES), F32)
    for idx in idxs:
        onehot = onehot + (lane == idx).astype(F32)
    tri = (lax.broadcasted_iota(jnp.int32, (tm, tm), 0)
           > lax.broadcasted_iota(jnp.int32, (tm, tm), 1)).astype(BF16)
    before = jnp.dot(tri, onehot.astype(BF16), preferred_element_type=F32) + carry_ref[...]
    carry_ref[...] = carry_ref[...] + jnp.sum(onehot, axis=0, keepdims=True)
    cnt_ref[...] = carry_ref[...]

    ri = jnp.zeros((tm, LANES), F32)
    gt = jnp.zeros((tm, LANES), F32)
    for k in range(TOP_K):
        rank = jnp.sum(jnp.where(lane == idxs[k], before, 0.0), axis=-1, keepdims=True)
        ri = jnp.where(lane == float(k), idxs[k], ri)
        ri = jnp.where(lane == float(TOP_K + k), rank, ri)
        gt = jnp.where(lane == float(k), exps[k] / denom, gt)
    ri_ref[...] = ri.astype(jnp.int32)
    g_ref[...] = gt


def _mix_ln_router(y_rwkv, y_pool, x2d, w_out_b, ln_w, ln_b, router_w_p, router_b_p, tm=512):
    n, d = x2d.shape
    rw_hi = router_w_p.astype(BF16)
    rw_lo = (router_w_p - rw_hi.astype(F32)).astype(BF16)
    half = y_rwkv.shape[1]
    const = lambda shape: pl.BlockSpec(shape, lambda i: (0, 0))
    return pl.pallas_call(
        _mix_kernel,
        grid=(n // tm,),
        in_specs=[pl.BlockSpec((tm, half), lambda i: (i, 0)),
                  pl.BlockSpec((tm, y_pool.shape[1]), lambda i: (i, 0)),
                  pl.BlockSpec((tm, d), lambda i: (i, 0)),
                  const(w_out_b.shape), const((1, d)), const((1, d)),
                  const(router_w_p.shape), const(router_w_p.shape), const((1, LANES))],
        out_specs=[pl.BlockSpec((tm, d), lambda i: (i, 0)),
                   pl.BlockSpec((tm, d // 2 // LANES, LANES), lambda i: (i, 0, 0)),
                   pl.BlockSpec((tm, LANES), lambda i: (i, 0)),
                   pl.BlockSpec((tm, LANES), lambda i: (i, 0)),
                   const((1, LANES))],
        out_shape=[jax.ShapeDtypeStruct((n, d), F32),
                   jax.ShapeDtypeStruct((n, d // 2 // LANES, LANES), jnp.uint32),
                   jax.ShapeDtypeStruct((n, LANES), jnp.int32),
                   jax.ShapeDtypeStruct((n, LANES), F32),
                   jax.ShapeDtypeStruct((1, LANES), F32)],
        scratch_shapes=[pltpu.VMEM((1, LANES), F32)],
        compiler_params=_cparams(("arbitrary",)),
        name="mix_ln_router",
    )(y_rwkv, y_pool, x2d, w_out_b, ln_w, ln_b, rw_hi, rw_lo, router_b_p)


MOE_BM = 512
GATHER_UNROLL = 8


MXU_DIM = 256
OUT_CHUNK = 512


def _split_w1_kernel(w_ref, g_ref, l_ref):
    half = MXU_DIM // 2
    src = lax.broadcasted_iota(jnp.int32, (MXU_DIM, MXU_DIM), 0)
    dst = lax.broadcasted_iota(jnp.int32, (MXU_DIM, MXU_DIM), 1)
    perm = (src == jnp.where(dst < half, 2 * dst, 2 * (dst - half) + 1)).astype(BF16)
    for c in range(w_ref.shape[2] // MXU_DIM):
        w = w_ref[0, :, c * MXU_DIM:(c + 1) * MXU_DIM].astype(BF16)
        o = jnp.dot(w, perm, preferred_element_type=F32)
        g_ref[0, :, c * half:(c + 1) * half] = o[:, :half].astype(BF16)
        l_ref[0, :, c * half:(c + 1) * half] = o[:, half:].astype(BF16)


def _split_w1(w1, tr=512, tc=2048):
    n_exp, d, f2 = w1.shape
    out = jax.ShapeDtypeStruct((n_exp, d, f2 // 2), BF16)
    return pl.pallas_call(
        _split_w1_kernel,
        grid=(n_exp, d // tr, f2 // tc),
        in_specs=[pl.BlockSpec((1, tr, tc), lambda e, i, j: (e, i, j))],
        out_specs=[pl.BlockSpec((1, tr, tc // 2), lambda e, i, j: (e, i, j)),
                   pl.BlockSpec((1, tr, tc // 2), lambda e, i, j: (e, i, j))],
        out_shape=[out, out],
        compiler_params=_cparams(("arbitrary", "arbitrary", "arbitrary")),
        name="split_w1",
    )(w1)


def _cast_kernel(w_ref, o_ref):
    o_ref[...] = w_ref[...].astype(o_ref.dtype)


def _cast_bf16(w, tr=512):
    n_exp, r, c = w.shape
    return pl.pallas_call(
        _cast_kernel,
        grid=(n_exp, r // tr),
        in_specs=[pl.BlockSpec((1, tr, c), lambda e, i: (e, i, 0))],
        out_specs=pl.BlockSpec((1, tr, c), lambda e, i: (e, i, 0)),
        out_shape=jax.ShapeDtypeStruct(w.shape, BF16),
        compiler_params=_cparams(("arbitrary", "arbitrary")),
        name="cast_w2",
    )(w)


def _scatter_kernel(pos_ref, hp_ref, zero_hbm, x_hbm, sem):
    del zero_hbm
    i = pl.program_id(0)
    tm = hp_ref.shape[0]

    def group(gi, carry):
        n0 = pl.multiple_of(gi * GATHER_UNROLL, GATHER_UNROLL)
        for j in range(GATHER_UNROLL):
            for k in range(TOP_K):
                p = pos_ref[(i * tm + n0 + j) * TOP_K + k]
                pltpu.make_async_copy(hp_ref.at[n0 + j], x_hbm.at[p], sem).start()
        return carry

    lax.fori_loop(0, tm // GATHER_UNROLL, group, 0)
    for k in range(TOP_K):
        pltpu.make_async_copy(hp_ref, x_hbm.at[pl.ds(0, tm)], sem).wait()


def _dispatch(h_packed, pos_flat, n_slots, tm=512):
    n = h_packed.shape[0]
    tile = h_packed.shape[1:]
    return pl.pallas_call(
        _scatter_kernel,
        grid_spec=pltpu.PrefetchScalarGridSpec(
            num_scalar_prefetch=1,
            grid=(n // tm,),
            in_specs=[pl.BlockSpec((tm,) + tile, lambda i, pos: (i, 0, 0)),
                      pl.BlockSpec(memory_space=pl.ANY)],
            out_specs=pl.BlockSpec(memory_space=pl.ANY),
            scratch_shapes=[pltpu.SemaphoreType.DMA(())]),
        out_shape=jax.ShapeDtypeStruct((n_slots,) + tile, h_packed.dtype),
        input_output_aliases={2: 0},
        compiler_params=_cparams(("arbitrary",)),
        name="moe_dispatch",
    )(pos_flat, h_packed, jnp.zeros((n_slots,) + tile, h_packed.dtype))


def _ffn_kernel(be_ref, nu_ref, x_ref, w1g_ref, w1l_ref, b1g_ref, b1l_ref, w2_ref, b2_ref, o_ref, xb_ref):
    m = pl.program_id(0)
    f = pl.program_id(1)

    @pl.when(f == 0)
    def _():
        half_d = xb_ref.shape[1] // 2
        packed = pltpu.einshape("tcl->ctl", x_ref[...])
        for c in range(x_ref.shape[1]):
            words = packed[c]
            xb_ref[:, c * LANES:(c + 1) * LANES] = pltpu.bitcast(words << 16, F32).astype(BF16)
            xb_ref[:, half_d + c * LANES:half_d + (c + 1) * LANES] = pltpu.bitcast(
                words & jnp.uint32(0xFFFF0000), F32).astype(BF16)
        o_ref[...] = jnp.broadcast_to(b2_ref[0], o_ref.shape)

    @pl.when(m < nu_ref[0])
    def _():
        x = xb_ref[...]
        hg = jnp.dot(x, w1g_ref[0], preferred_element_type=F32) + b1g_ref[0]
        hl = jnp.dot(x, w1l_ref[0], preferred_element_type=F32) + b1l_ref[0]
        x_glu = jnp.minimum(hg, SWIGLU_LIMIT)
        x_lin = jnp.clip(hl, -SWIGLU_LIMIT, SWIGLU_LIMIT)
        act = x_glu * jax.nn.sigmoid(SWIGLU_ALPHA * x_glu) * (x_lin + 1.0)
        act = act.astype(BF16)
        for c in range(0, o_ref.shape[1], OUT_CHUNK):
            o_ref[:, c:c + OUT_CHUNK] += jnp.dot(act, w2_ref[0, :, c:c + OUT_CHUNK], preferred_element_type=F32)


def _experts(x_slots, block_expert, n_used, w1g, w1l, b1g, b1l, w2, b2, bm=MOE_BM, tf=1024):
    d = w1g.shape[1]
    n_slots = x_slots.shape[0]
    n_blocks = n_slots // bm
    ff = w1g.shape[2]
    n_f = ff // tf

    def m_eff(m, nu):
        return jnp.minimum(m, jnp.maximum(nu[0] - 1, 0))

    def f_eff(m, f, nu):
        return jnp.where(m < nu[0], f, n_f - 1)

    return pl.pallas_call(
        _ffn_kernel,
        grid_spec=pltpu.PrefetchScalarGridSpec(
            num_scalar_prefetch=2,
            grid=(n_blocks, n_f),
            in_specs=[
                pl.BlockSpec((bm,) + x_slots.shape[1:], lambda m, f, be, nu: (m_eff(m, nu), 0, 0)),
                pl.BlockSpec((1, d, tf), lambda m, f, be, nu: (be[m], 0, f_eff(m, f, nu))),
                pl.BlockSpec((1, d, tf), lambda m, f, be, nu: (be[m], 0, f_eff(m, f, nu))),
                pl.BlockSpec((1, 1, tf), lambda m, f, be, nu: (be[m], 0, f_eff(m, f, nu))),
                pl.BlockSpec((1, 1, tf), lambda m, f, be, nu: (be[m], 0, f_eff(m, f, nu))),
                pl.BlockSpec((1, tf, d), lambda m, f, be, nu: (be[m], f_eff(m, f, nu), 0)),
                pl.BlockSpec((1, 1, d), lambda m, f, be, nu: (be[m], 0, 0)),
            ],
            out_specs=pl.BlockSpec((bm, d), lambda m, f, be, nu: (m, 0)),
            scratch_shapes=[pltpu.VMEM((bm, d), BF16)]),
        out_shape=jax.ShapeDtypeStruct((n_slots, d), F32),
        compiler_params=_cparams(("arbitrary", "arbitrary")),
        name="moe_experts",
    )(block_expert, n_used, x_slots, w1g, w1l, b1g, b1l, w2, b2)


def _combine_kernel(pos_ref, h_ref, g_ref, lw_ref, lb_ref, y_hbm, o_ref, buf, sem):
    i = pl.program_id(0)
    tm = h_ref.shape[0]

    def issue_step(step, slot):
        def group(gi, carry):
            n0 = pl.multiple_of(gi * GATHER_UNROLL, GATHER_UNROLL)
            for j in range(GATHER_UNROLL):
                for k in range(TOP_K):
                    p = pos_ref[(step * tm + n0 + j) * TOP_K + k]
                    pltpu.make_async_copy(y_hbm.at[pl.ds(p, 1), :], buf.at[slot, pl.ds(k * tm + n0 + j, 1), :],
                                          sem.at[slot]).start()
            return carry

        lax.fori_loop(0, tm // GATHER_UNROLL, group, 0)

    @pl.when(i == 0)
    def _():
        issue_step(0, 0)

    @pl.when(i + 1 < pl.num_programs(0))
    def _():
        issue_step(i + 1, (i + 1) % 2)

    slot = i % 2
    pltpu.make_async_copy(y_hbm.at[pl.ds(0, TOP_K * tm), :], buf.at[slot], sem.at[slot]).wait()
    g = g_ref[...]
    ffn = g[:, 0:1] * buf[slot, 0:tm, :]
    for k in range(1, TOP_K):
        ffn = ffn + g[:, k:k + 1] * buf[slot, k * tm:(k + 1) * tm, :]
    o_ref[...] = _layer_norm(DEEPNORM_ALPHA * h_ref[...] + ffn, lw_ref[...], lb_ref[...], LN_EPS)


def _combine(pos_flat, h, gates, ln_w, ln_b, y_slots, tm=256):
    n, d = h.shape
    return pl.pallas_call(
        _combine_kernel,
        grid_spec=pltpu.PrefetchScalarGridSpec(
            num_scalar_prefetch=1,
            grid=(n // tm,),
            in_specs=[pl.BlockSpec((tm, d), lambda i, pos: (i, 0)),
                      pl.BlockSpec((tm, LANES), lambda i, pos: (i, 0)),
                      pl.BlockSpec((1, d), lambda i, pos: (0, 0)),
                      pl.BlockSpec((1, d), lambda i, pos: (0, 0)),
                      pl.BlockSpec(memory_space=pl.ANY)],
            out_specs=pl.BlockSpec((tm, d), lambda i, pos: (i, 0)),
            scratch_shapes=[pltpu.VMEM((2, TOP_K * tm, d), F32), pltpu.SemaphoreType.DMA((2,))]),
        out_shape=jax.ShapeDtypeStruct((n, d), F32),
        compiler_params=_cparams(("arbitrary",)),
        name="moe_combine_ln",
    )(pos_flat, h, gates, ln_w, ln_b, y_slots)


def _moe(h, h_packed, route_i, gates, counts_f, w1, b1, w2, b2, ln_w, ln_b):
    n_tok, d = h.shape
    n_exp = w1.shape[0]
    n_asg = n_tok * TOP_K
    n_blocks = -(-n_asg // MOE_BM) + n_exp

    counts = counts_f[0, :n_exp].astype(jnp.int32)
    padded = (counts + MOE_BM - 1) // MOE_BM * MOE_BM
    pend = jnp.cumsum(padded)
    pstart = pend - padded
    pos = pstart[route_i[:, 0:TOP_K]] + route_i[:, TOP_K:2 * TOP_K]
    pos_flat = pos.reshape(n_asg)
    n_used = (pend[-1] // MOE_BM).astype(jnp.int32).reshape(1)
    blk = jnp.minimum(jnp.arange(n_blocks, dtype=jnp.int32), n_used[0] - 1)
    block_expert = jnp.minimum(jnp.sum(pend[None, :] <= (blk * MOE_BM)[:, None], axis=1), n_exp - 1).astype(jnp.int32)

    w1g, w1l = _split_w1(w1)
    b1g = b1[:, None, 0::2]
    b1l = b1[:, None, 1::2]
    w2b = _cast_bf16(w2)

    x_slots = _dispatch(h_packed, pos_flat, n_blocks * MOE_BM)
    y_slots = _experts(x_slots, block_expert, n_used, w1g, w1l, b1g, b1l, w2b, b2[:, None, :])
    return _combine(pos_flat, h, gates, ln_w, ln_b, y_slots)


def _pad_to(a, axis, size):
    pad = [(0, 0)] * a.ndim
    pad[axis] = (0, size - a.shape[axis])
    return jnp.pad(a, pad)


def kernel(x, w_in, mu_shift, w0, w2_decay, a0, a2_iclr, g2_gate, k_k, k_a, r_k, lnx_w, lnx_b, w_pool, pool_scale,
           w_out, ln1_w, ln1_b, router_w, router_b, w1_exp, b1_exp, w2_exp, b2_exp, ln2_w, ln2_b):
    batch, seq, d = x.shape
    assert w_in.shape[0] == 1, "one layer"
    x2d = x.reshape(batch * seq, d)
    c_xw = RKV_COLS
    c_xa = c_xw + DECAY_LORA
    c_xg = c_xa + ICLR_LORA
    c_pool = c_xg + GATE_LORA

    def lora_layout(a):
        return jnp.concatenate([_pad_to(a[:, c_xw:c_xa], 1, LORA_XA - LORA_XW),
                                _pad_to(a[:, c_xa:c_xg], 1, LORA_XG - LORA_XA),
                                _pad_to(a[:, c_xg:c_pool], 1, LORA_PAD - LORA_XG)], axis=1)

    wi = w_in[0]
    w_p = jnp.concatenate([wi[:, :RKV_COLS], wi[:, c_pool:], lora_layout(wi)], axis=1).astype(BF16)
    mu = mu_shift[0][None, :]
    row = lambda a: a.reshape(1, -1)

    proj = _in_proj(x2d, w_p)
    y_rwkv = _rwkv_group(
        proj, batch, seq, mu[:, :RKV_COLS], lora_layout(mu), row(w0[0]),
        _pad_to(w2_decay[0], 0, LANES).astype(BF16), row(a0[0]), _pad_to(a2_iclr[0], 0, LANES).astype(BF16),
        _pad_to(g2_gate[0], 0, LORA_PAD - LORA_XG).astype(BF16), row(k_k[0]), row(k_a[0]), row(r_k[0]),
        row(lnx_w[0]), row(lnx_b[0]))
    y_pool = _pool_group(proj, batch, seq, w_pool[0].astype(BF16), row(pool_scale[0]))

    n_exp = router_w.shape[2]
    router_w_p = _pad_to(router_w[0], 1, LANES)
    router_b_p = jnp.concatenate([router_b[0], jnp.full((LANES - n_exp,), -1e30, F32)])[None, :]
    h1, h1_packed, route_i, gates, counts = _mix_ln_router(y_rwkv, y_pool, x2d, w_out[0].astype(BF16), row(ln1_w[0]),
                                                row(ln1_b[0]), router_w_p, router_b_p)
    out = _moe(h1, h1_packed, route_i, gates, counts, w1_exp[0], b1_exp[0], w2_exp[0], b2_exp[0], row(ln2_w[0]), row(ln2_b[0]))
    return out.reshape(batch, seq, d)
```

```python
import jax
import jax.numpy as jnp
from jax import lax
from jax.experimental import pallas as pl
from jax.experimental.pallas import tpu as pltpu

F32 = jnp.float32
BF16 = jnp.bfloat16

RWKV_WIDTH = 1024
HEAD = 64
DECAY_LORA = 64
ICLR_LORA = 64
GATE_LORA = 160
POOL_WINDOWS = (2, 4, 8, 16)
POOL_GROUP_WIDTH = 256
TOP_K = 4
SWIGLU_LIMIT = 7.0
SWIGLU_ALPHA = 1.702
LN_EPS = 1e-5
GN_EPS = 64e-5
DEEPNORM_ALPHA = 2.0 ** 0.25

LANES = 128
LORA_PAD = 512
LORA_XW, LORA_XA, LORA_XG = 0, 128, 256
RKV_COLS = 3 * RWKV_WIDTH
POOL_WIDTH = len(POOL_WINDOWS) * POOL_GROUP_WIDTH
POOL_COL0 = RKV_COLS
LORA_COL0 = POOL_COL0 + POOL_WIDTH
PROJ_COLS = LORA_COL0 + LORA_PAD
CHUNK = 64
SUB = 16
VMEM_LIMIT = 56 * 1024 * 1024


def _cparams(sem):
    return pltpu.CompilerParams(dimension_semantics=sem, vmem_limit_bytes=VMEM_LIMIT)


def _bdot(a, b):
    return jnp.dot(a.astype(BF16), b.astype(BF16), preferred_element_type=F32)


def _bdot_nt(a, b):
    return lax.dot_general(a.astype(BF16), b.astype(BF16), (((1,), (1,)), ((), ())),
                           preferred_element_type=F32)


def _bdot_tn(a, b):
    return lax.dot_general(a.astype(BF16), b.astype(BF16), (((0,), (0,)), ((), ())),
                           preferred_element_type=F32)


def _in_proj_kernel(x_ref, w_ref, o_ref, xb_ref):
    @pl.when(pl.program_id(1) == 0)
    def _():
        xb_ref[...] = x_ref[...].astype(BF16)

    o_ref[...] = jnp.dot(xb_ref[...], w_ref[...], preferred_element_type=F32)


def _in_proj(x2d, w_p, tm=1024, tn=768):
    n, d = x2d.shape
    cols = w_p.shape[1]
    return pl.pallas_call(
        _in_proj_kernel,
        grid=(n // tm, cols // tn),
        in_specs=[pl.BlockSpec((tm, d), lambda i, j: (i, 0)),
                  pl.BlockSpec((d, tn), lambda i, j: (0, j))],
        out_specs=pl.BlockSpec((tm, tn), lambda i, j: (i, j)),
        out_shape=jax.ShapeDtypeStruct((n, cols), F32),
        scratch_shapes=[pltpu.VMEM((tm, d), BF16)],
        compiler_params=_cparams(("arbitrary", "arbitrary")),
        name="in_proj",
    )(x2d, w_p)


def _shifted(x_ref, prev_ref, mu_ref, first_row):
    x = x_ref[...]
    xp = jnp.where(first_row, prev_ref[...], pltpu.roll(x, 1, axis=0))
    prev_ref[...] = x_ref[x.shape[0] - 1:x.shape[0], :]
    return x + (xp - x) * mu_ref[...]


def _rwkv_kernel(r_ref, k_ref, v_ref, lo_ref, mur_ref, muk_ref, muv_ref, mul_ref,
                 w0_ref, w2d_ref, a0_ref, a2_ref, g2_ref, kk_ref, ka_ref, rk_ref, lnw_ref, lnb_ref,
                 o_ref,
                 pr_ref, pk_ref, pv_ref, plo_ref, s_ref,
                 rs_ref, lw_ref, ks_ref, vs_ref, kks_ref, as_ref, gs_ref):
    tb, width = r_ref.shape
    n_pairs = width // LANES
    c2 = 2 * CHUNK

    @pl.when(pl.program_id(2) == 0)
    def _():
        pr_ref[...] = jnp.zeros_like(pr_ref)
        pk_ref[...] = jnp.zeros_like(pk_ref)
        pv_ref[...] = jnp.zeros_like(pv_ref)
        plo_ref[...] = jnp.zeros_like(plo_ref)
        s_ref[...] = jnp.zeros_like(s_ref)

    first_row = lax.broadcasted_iota(jnp.int32, (tb, 1), 0) == 0
    r = _shifted(r_ref, pr_ref, mur_ref, first_row)
    k = _shifted(k_ref, pk_ref, muk_ref, first_row)
    v = _shifted(v_ref, pv_ref, muv_ref, first_row)
    lo = _shifted(lo_ref, plo_ref, mul_ref, first_row)

    xw = jnp.tanh(lo[:, LORA_XW:LORA_XW + LANES])
    xa = lo[:, LORA_XA:LORA_XA + LANES]
    xg = jax.nn.sigmoid(lo[:, LORA_XG:LORA_PAD])
    z = w0_ref[...] + _bdot(xw, w2d_ref[...])
    w_logit = -(jnp.maximum(-z, 0.0) + jnp.log(1.0 + jnp.exp(-jnp.abs(z)))) - 0.5
    a_ic = jax.nn.sigmoid(a0_ref[...] + _bdot(xa, a2_ref[...]))
    rs_ref[...] = r
    lw_ref[...] = -jnp.exp(w_logit)
    ks_ref[...] = k * (1.0 + (a_ic - 1.0) * ka_ref[...])
    vs_ref[...] = v
    kks_ref[...] = k * kk_ref[...]
    as_ref[...] = a_ic
    gs_ref[...] = _bdot(xg, g2_ref[...])

    lane = lax.broadcasted_iota(jnp.int32, (CHUNK, LANES), 1)
    head0 = lane < HEAD
    row = lax.broadcasted_iota(jnp.int32, (c2, c2), 0)
    col = lax.broadcasted_iota(jnp.int32, (c2, c2), 1)
    strict = row > col
    incl = row >= col
    same_sub = (row // SUB) == (col // SUB)
    eye = (row == col).astype(F32)
    tri = (lax.broadcasted_iota(jnp.int32, (CHUNK, CHUNK), 0)
           >= lax.broadcasted_iota(jnp.int32, (CHUNK, CHUNK), 1)).astype(F32)
    def head_sum(x):
        s0 = jnp.sum(jnp.where(head0, x, 0.0), axis=-1, keepdims=True)
        s1 = jnp.sum(jnp.where(head0, 0.0, x), axis=-1, keepdims=True)
        return jnp.where(head0, s0, s1)

    def stack(x):
        return jnp.concatenate([jnp.where(head0, x, 0.0), jnp.where(head0, 0.0, x)], axis=0)

    def operands(rc, lw, lwi, kc, vc, kk, ac):
        kkn = kk * lax.rsqrt(jnp.maximum(head_sum(kk * kk), 1e-24))
        b_ = kkn * ac
        tot = lwi[CHUNK - 1:CHUNK, :]
        e_inv = jnp.exp(-lwi)
        e_out = jnp.exp(tot - lwi)
        at = stack(-kkn * jnp.exp(lwi - lw)).astype(BF16)
        rt = stack(rc * jnp.exp(lwi)).astype(BF16)
        bk = jnp.concatenate([stack(b_ * e_inv), stack(kc * e_inv)], axis=0).astype(BF16)
        bkh = jnp.concatenate([stack(b_ * e_out), stack(kc * e_out)], axis=0).astype(BF16)
        return at, rt, bk, bkh, stack(vc).astype(BF16), jnp.exp(tot)

    def finish(y_st, rc, kc, vc, gate, rk, lnw, lnb):
        y = y_st[:CHUNK, :] + y_st[CHUNK:, :]
        mu = head_sum(y) * (1.0 / HEAD)
        yc = y - mu
        var = head_sum(yc * yc) * (1.0 / HEAD)
        yn = yc * lax.rsqrt(var + GN_EPS) * lnw + lnb
        yn = yn + head_sum(rc * kc * rk) * vc
        return (yn * gate).astype(o_ref.dtype)

    def chunk(c, carry):
        sl = pl.ds(pl.multiple_of(c * CHUNK, CHUNK), CHUNK)
        lw = lw_ref[sl, :]
        lwi = jnp.dot(tri, lw, preferred_element_type=F32, precision=lax.Precision.HIGHEST)
        pairs = range(n_pairs)
        ls = [slice(p * LANES, (p + 1) * LANES) for p in pairs]
        ops = [operands(rs_ref[sl, l], lw[:, l], lwi[:, l], ks_ref[sl, l], vs_ref[sl, l], kks_ref[sl, l],
                        as_ref[sl, l]) for l in ls]
        at, rt, bk, bkh, vst, dec = (list(z) for z in zip(*ops))
        a_all = [_bdot_nt(jnp.concatenate([at[p], rt[p]], axis=0), bk[p]) for p in pairs]
        a_ab = [jnp.where(strict, a[:c2, :c2], 0.0) for a in a_all]
        a_ak = [jnp.where(strict, a[:c2, c2:], 0.0) for a in a_all]
        a_r = [jnp.where(jnp.concatenate([incl, incl], axis=1), a[c2:, :], 0.0).astype(BF16) for a in a_all]
        akv = [_bdot(a_ak[p], vst[p]).astype(BF16) for p in pairs]

        d1 = [jnp.where(same_sub, a, 0.0) for a in a_ab]
        l1 = [a_ab[p] - d1[p] for p in pairs]
        d2 = [_bdot(d, d) for d in d1]
        d4 = [_bdot(d, d) for d in d2]
        p12 = [_bdot(eye + d1[p], eye + d2[p]) for p in pairs]
        d8 = [_bdot(d, d) for d in d4]
        p48 = [_bdot(eye + d4[p], eye + d8[p]) for p in pairs]
        t_d = [_bdot(p12[p], p48[p]).astype(BF16) for p in pairs]
        m1 = [_bdot(t_d[p], l1[p]) for p in pairs]
        m2 = [_bdot(m, m) for m in m1]
        q = [_bdot(eye + m1[p], eye + m2[p]) for p in pairs]
        t_inv = [_bdot(q[p], t_d[p]) for p in pairs]
        tx = [_bdot(t_inv[p], jnp.concatenate([at[p], akv[p]], axis=1)) for p in pairs]

        s_bd = [s_ref[p] for p in pairs]
        wr = [_bdot_nt(jnp.concatenate([tx[p][:, :LANES].astype(BF16), rt[p]], axis=0), s_bd[p]) for p in pairs]
        uv = [jnp.concatenate([(wr[p][:c2, :] + tx[p][:, LANES:]).astype(BF16), vst[p]], axis=0) for p in pairs]
        y_st = [wr[p][c2:, :] + _bdot(a_r[p], uv[p]) for p in pairs]
        for p in pairs:
            s_ref[p] = s_bd[p] * dec[p] + _bdot_tn(uv[p], bkh[p])
        for p, l in zip(pairs, ls):
            o_ref[sl, l] = finish(y_st[p], rs_ref[sl, l], ks_ref[sl, l], vs_ref[sl, l], gs_ref[sl, l],
                                  rk_ref[:, l], lnw_ref[:, l], lnb_ref[:, l])
        return carry

    lax.fori_loop(0, tb // CHUNK, chunk, 0)


RWKV_PAIRS_PER_STEP = 8


def _rwkv_group(proj, batch, seq, mu_rkv, mu_lora, w0, w2d, a0, a2, g2, k_k, k_a, r_k, lnx_w, lnx_b, tb=512,
                pairs=RWKV_PAIRS_PER_STEP):
    width = pairs * LANES
    n_g = RWKV_WIDTH // width
    n_tb = seq // tb
    row = lambda b, h, t: b * n_tb + t
    vec = lambda off: pl.BlockSpec((1, width), lambda b, h, t: (0, off + h))
    lora_blk = LORA_COL0 // LORA_PAD
    in_specs = [
        pl.BlockSpec((tb, width), lambda b, h, t: (row(b, h, t), h)),
        pl.BlockSpec((tb, width), lambda b, h, t: (row(b, h, t), n_g + h)),
        pl.BlockSpec((tb, width), lambda b, h, t: (row(b, h, t), 2 * n_g + h)),
        pl.BlockSpec((tb, LORA_PAD), lambda b, h, t: (row(b, h, t), lora_blk)),
        vec(0), vec(n_g), vec(2 * n_g),
        pl.BlockSpec((1, LORA_PAD), lambda b, h, t: (0, 0)),
        vec(0),
        pl.BlockSpec((LANES, width), lambda b, h, t: (0, h)),
        vec(0),
        pl.BlockSpec((LANES, width), lambda b, h, t: (0, h)),
        pl.BlockSpec((LORA_PAD - LORA_XG, width), lambda b, h, t: (0, h)),
        vec(0), vec(0), vec(0), vec(0), vec(0),
    ]
    blk = lambda: pltpu.VMEM((tb, width), F32)
    return pl.pallas_call(
        _rwkv_kernel,
        grid=(batch, n_g, n_tb),
        in_specs=in_specs,
        out_specs=pl.BlockSpec((tb, width), lambda b, h, t: (row(b, h, t), h)),
        out_shape=jax.ShapeDtypeStruct((batch * seq, RWKV_WIDTH), BF16),
        scratch_shapes=[pltpu.VMEM((1, width), F32), pltpu.VMEM((1, width), F32), pltpu.VMEM((1, width), F32),
                        pltpu.VMEM((1, LORA_PAD), F32), pltpu.VMEM((pairs, 2 * CHUNK, LANES), F32),
                        blk(), blk(), blk(), blk(), blk(), blk(), blk()],
        compiler_params=_cparams(("arbitrary", "arbitrary", "arbitrary")),
        name="rwkv7_group",
    )(proj, proj, proj, proj, mu_rkv, mu_rkv, mu_rkv, mu_lora, w0, w2d, a0, a2, g2, k_k, k_a, r_k, lnx_w, lnx_b)


HALO = 16


def _pool_kernel(p_ref, w_ref, sc_ref, o_ref, ext_ref):
    tb = p_ref.shape[0]
    t = pl.program_id(1)

    @pl.when(t == 0)
    def _():
        ext_ref[0:HALO, :] = jnp.zeros((HALO, ext_ref.shape[1]), F32)

    @pl.when(t > 0)
    def _():
        ext_ref[0:HALO, :] = ext_ref[tb:tb + HALO, :]

    ext_ref[HALO:HALO + tb, :] = p_ref[...]
    t_idx = t * tb + lax.broadcasted_iota(jnp.int32, (tb, 1), 0)
    for gi, win in enumerate(POOL_WINDOWS):
        cs = slice(gi * POOL_GROUP_WIDTH, (gi + 1) * POOL_GROUP_WIDTH)
        acc = ext_ref[HALO:HALO + tb, cs]
        for d in range(1, win):
            acc = acc + ext_ref[HALO - d:HALO - d + tb, cs]
        count = jnp.minimum(t_idx + 1, win).astype(F32)
        pooled = acc / count - ext_ref[HALO:HALO + tb, cs]
        mixed = _bdot(pooled, w_ref[gi])
        o_ref[:, cs] = (mixed * sc_ref[:, cs]).astype(o_ref.dtype)


def _pool_group(proj, batch, seq, w_pool_b, pool_scale, tb=512):
    n_tb = seq // tb
    col_blk = POOL_COL0 // POOL_WIDTH
    return pl.pallas_call(
        _pool_kernel,
        grid=(batch, n_tb),
        in_specs=[pl.BlockSpec((tb, POOL_WIDTH), lambda b, t: (b * n_tb + t, col_blk)),
                  pl.BlockSpec(w_pool_b.shape, lambda b, t: (0, 0, 0)),
                  pl.BlockSpec((1, POOL_WIDTH), lambda b, t: (0, 0))],
        out_specs=pl.BlockSpec((tb, POOL_WIDTH), lambda b, t: (b * n_tb + t, 0)),
        out_shape=jax.ShapeDtypeStruct((batch * seq, POOL_WIDTH), BF16),
        scratch_shapes=[pltpu.VMEM((tb + HALO, POOL_WIDTH), F32)],
        compiler_params=_cparams(("arbitrary", "arbitrary")),
        name="pool_group",
    )(proj, w_pool_b, pool_scale)


def _layer_norm(x, w, b, eps):
    mu = jnp.mean(x, axis=-1, keepdims=True)
    xc = x - mu
    var = jnp.mean(xc * xc, axis=-1, keepdims=True)
    return xc * lax.rsqrt(var + eps) * w + b


def _mix_kernel(yr_ref, yp_ref, x_ref, wo_ref, lw_ref, lb_ref, rwh_ref, rwl_ref, rb_ref,
                h_ref, hp_ref, ri_ref, g_ref, cnt_ref, carry_ref):
    tm = x_ref.shape[0]
    half = yr_ref.shape[1]

    @pl.when(pl.program_id(0) == 0)
    def _():
        carry_ref[...] = jnp.zeros_like(carry_ref)

    mix = (jnp.dot(yr_ref[...], wo_ref[0:half, :], preferred_element_type=F32)
           + jnp.dot(yp_ref[...], wo_ref[half:, :], preferred_element_type=F32))
    h = _layer_norm(DEEPNORM_ALPHA * x_ref[...] + mix, lw_ref[...], lb_ref[...], LN_EPS)
    h_ref[...] = h
    half_d = h.shape[1] // 2
    lo = pltpu.bitcast(h[:, :half_d].astype(BF16).astype(F32), jnp.uint32) >> 16
    hi = pltpu.bitcast(h[:, half_d:].astype(BF16).astype(F32), jnp.uint32)
    words = lo | hi
    chunks = jnp.stack([words[:, c * LANES:(c + 1) * LANES] for c in range(hp_ref.shape[1])], axis=0)
    hp_ref[...] = pltpu.einshape("ctl->tcl", chunks)

    h_hi = h.astype(BF16)
    h_lo = (h - h_hi.astype(F32)).astype(BF16)
    logits = (jnp.dot(h_hi, rwh_ref[...], preferred_element_type=F32)
              + jnp.dot(h_lo, rwh_ref[...], preferred_element_type=F32)
              + jnp.dot(h_hi, rwl_ref[...], preferred_element_type=F32)) + rb_ref[...]
    lane = lax.broadcasted_iota(jnp.int32, (tm, LANES), 1).astype(F32)
    idxs, vals = [], []
    left = logits
    for _ in range(TOP_K):
        m = jnp.max(left, axis=-1, keepdims=True)
        idx = jnp.min(jnp.where(left == m, lane, float(LANES)), axis=-1, keepdims=True)
        idxs.append(idx)
        vals.append(m)
        left = jnp.where(lane == idx, -jnp.inf, left)
    exps = [jnp.exp(v - vals[0]) for v in vals]
    denom = exps[0] + exps[1] + exps[2] + exps[3]

    onehot = jnp.zeros((tm, LANES), F32)
    for idx in idxs:
        onehot = onehot + (lane == idx).astype(F32)
    tri = (lax.broadcasted_iota(jnp.int32, (tm, tm), 0)
           > lax.broadcasted_iota(jnp.int32, (tm, tm), 1)).astype(BF16)
    before = jnp.dot(tri, onehot.astype(BF16), preferred_element_type=F32) + carry_ref[...]
    carry_ref[...] = carry_ref[...] + jnp.sum(onehot, axis=0, keepdims=True)
    cnt_ref[...] = carry_ref[...]

    ri = jnp.zeros((tm, LANES), F32)
    gt = jnp.zeros((tm, LANES), F32)
    for k in range(TOP_K):
        rank = jnp.sum(jnp.where(lane == idxs[k], before, 0.0), axis=-1, keepdims=True)
        ri = jnp.where(lane == float(k), idxs[k], ri)
        ri = jnp.where(lane == float(TOP_K + k), rank, ri)
        gt = jnp.where(lane == float(k), exps[k] / denom, gt)
    ri_ref[...] = ri.astype(jnp.int32)
    g_ref[...] = gt


def _mix_ln_router(y_rwkv, y_pool, x2d, w_out_b, ln_w, ln_b, router_w_p, router_b_p, tm=512):
    n, d = x2d.shape
    rw_hi = router_w_p.astype(BF16)
    rw_lo = (router_w_p - rw_hi.astype(F32)).astype(BF16)
    half = y_rwkv.shape[1]
    const = lambda shape: pl.BlockSpec(shape, lambda i: (0, 0))
    return pl.pallas_call(
        _mix_kernel,
        grid=(n // tm,),
        in_specs=[pl.BlockSpec((tm, half), lambda i: (i, 0)),
                  pl.BlockSpec((tm, y_pool.shape[1]), lambda i: (i, 0)),
                  pl.BlockSpec((tm, d), lambda i: (i, 0)),
                  const(w_out_b.shape), const((1, d)), const((1, d)),
                  const(router_w_p.shape), const(router_w_p.shape), const((1, LANES))],
        out_specs=[pl.BlockSpec((tm, d), lambda i: (i, 0)),
                   pl.BlockSpec((tm, d // 2 // LANES, LANES), lambda i: (i, 0, 0)),
                   pl.BlockSpec((tm, LANES), lambda i: (i, 0)),
                   pl.BlockSpec((tm, LANES), lambda i: (i, 0)),
                   const((1, LANES))],
        out_shape=[jax.ShapeDtypeStruct((n, d), F32),
                   jax.ShapeDtypeStruct((n, d // 2 // LANES, LANES), jnp.uint32),
                   jax.ShapeDtypeStruct((n, LANES), jnp.int32),
                   jax.ShapeDtypeStruct((n, LANES), F32),
                   jax.ShapeDtypeStruct((1, LANES), F32)],
        scratch_shapes=[pltpu.VMEM((1, LANES), F32)],
        compiler_params=_cparams(("arbitrary",)),
        name="mix_ln_router",
    )(y_rwkv, y_pool, x2d, w_out_b, ln_w, ln_b, rw_hi, rw_lo, router_b_p)


MOE_BM = 512
GATHER_UNROLL = 8


MXU_DIM = 256
OUT_CHUNK = 512


def _split_w1_kernel(w_ref, g_ref, l_ref):
    half = MXU_DIM // 2
    src = lax.broadcasted_iota(jnp.int32, (MXU_DIM, MXU_DIM), 0)
    dst = lax.broadcasted_iota(jnp.int32, (MXU_DIM, MXU_DIM), 1)
    perm = (src == jnp.where(dst < half, 2 * dst, 2 * (dst - half) + 1)).astype(BF16)
    for c in range(w_ref.shape[2] // MXU_DIM):
        w = w_ref[0, :, c * MXU_DIM:(c + 1) * MXU_DIM].astype(BF16)
        o = jnp.dot(w, perm, preferred_element_type=F32)
        g_ref[0, :, c * half:(c + 1) * half] = o[:, :half].astype(BF16)
        l_ref[0, :, c * half:(c + 1) * half] = o[:, half:].astype(BF16)


def _split_w1(w1, tr=512, tc=2048):
    n_exp, d, f2 = w1.shape
    out = jax.ShapeDtypeStruct((n_exp, d, f2 // 2), BF16)
    return pl.pallas_call(
        _split_w1_kernel,
        grid=(n_exp, d // tr, f2 // tc),
        in_specs=[pl.BlockSpec((1, tr, tc), lambda e, i, j: (e, i, j))],
        out_specs=[pl.BlockSpec((1, tr, tc // 2), lambda e, i, j: (e, i, j)),
                   pl.BlockSpec((1, tr, tc // 2), lambda e, i, j: (e, i, j))],
        out_shape=[out, out],
        compiler_params=_cparams(("arbitrary", "arbitrary", "arbitrary")),
        name="split_w1",
    )(w1)


def _cast_kernel(w_ref, o_ref):
    o_ref[...] = w_ref[...].astype(o_ref.dtype)


def _cast_bf16(w, tr=512):
    n_exp, r, c = w.shape
    return pl.pallas_call(
        _cast_kernel,
        grid=(n_exp, r // tr),
        in_specs=[pl.BlockSpec((1, tr, c), lambda e, i: (e, i, 0))],
        out_specs=pl.BlockSpec((1, tr, c), lambda e, i: (e, i, 0)),
        out_shape=jax.ShapeDtypeStruct(w.shape, BF16),
        compiler_params=_cparams(("arbitrary", "arbitrary")),
        name="cast_w2",
    )(w)


def _scatter_kernel(pos_ref, hp_ref, zero_hbm, x_hbm, sem):
    del zero_hbm
    i = pl.program_id(0)
    tm = hp_ref.shape[0]

    def group(gi, carry):
        n0 = pl.multiple_of(gi * GATHER_UNROLL, GATHER_UNROLL)
        for j in range(GATHER_UNROLL):
            for k in range(TOP_K):
                p = pos_ref[(i * tm + n0 + j) * TOP_K + k]
                pltpu.make_async_copy(hp_ref.at[n0 + j], x_hbm.at[p], sem).start()
        return carry

    lax.fori_loop(0, tm // GATHER_UNROLL, group, 0)
    for k in range(TOP_K):
        pltpu.make_async_copy(hp_ref, x_hbm.at[pl.ds(0, tm)], sem).wait()


def _dispatch(h_packed, pos_flat, n_slots, tm=512):
    n = h_packed.shape[0]
    tile = h_packed.shape[1:]
    return pl.pallas_call(
        _scatter_kernel,
        grid_spec=pltpu.PrefetchScalarGridSpec(
            num_scalar_prefetch=1,
            grid=(n // tm,),
            in_specs=[pl.BlockSpec((tm,) + tile, lambda i, pos: (i, 0, 0)),
                      pl.BlockSpec(memory_space=pl.ANY)],
            out_specs=pl.BlockSpec(memory_space=pl.ANY),
            scratch_shapes=[pltpu.SemaphoreType.DMA(())]),
        out_shape=jax.ShapeDtypeStruct((n_slots,) + tile, h_packed.dtype),
        input_output_aliases={2: 0},
        compiler_params=_cparams(("arbitrary",)),
        name="moe_dispatch",
    )(pos_flat, h_packed, jnp.zeros((n_slots,) + tile, h_packed.dtype))


def _ffn_kernel(be_ref, nu_ref, x_ref, w1g_ref, w1l_ref, b1g_ref, b1l_ref, w2_ref, b2_ref, o_ref):
    m = pl.program_id(0)
    f = pl.program_id(1)

    @pl.when(f == 0)
    def _():
        o_ref[...] = jnp.broadcast_to(b2_ref[0], o_ref.shape)

    @pl.when(m < nu_ref[0])
    def _():
        packed = pltpu.einshape("tcl->ctl", x_ref[...])
        chunks = [packed[c] for c in range(x_ref.shape[1])]
        x = jnp.concatenate([pltpu.bitcast(w << 16, F32).astype(BF16) for w in chunks]
                            + [pltpu.bitcast(w & jnp.uint32(0xFFFF0000), F32).astype(BF16) for w in chunks], axis=1)
        hg = jnp.dot(x, w1g_ref[0], preferred_element_type=F32) + b1g_ref[0]
        hl = jnp.dot(x, w1l_ref[0], preferred_element_type=F32) + b1l_ref[0]
        x_glu = jnp.minimum(hg, SWIGLU_LIMIT)
        x_lin = jnp.clip(hl, -SWIGLU_LIMIT, SWIGLU_LIMIT)
        act = x_glu * jax.nn.sigmoid(SWIGLU_ALPHA * x_glu) * (x_lin + 1.0)
        act = act.astype(BF16)
        for c in range(0, o_ref.shape[1], OUT_CHUNK):
            o_ref[:, c:c + OUT_CHUNK] += jnp.dot(act, w2_ref[0, :, c:c + OUT_CHUNK], preferred_element_type=F32)


def _experts(x_slots, block_expert, n_used, w1g, w1l, b1g, b1l, w2, b2, bm=MOE_BM, tf=1024):
    d = w1g.shape[1]
    n_slots = x_slots.shape[0]
    n_blocks = n_slots // bm
    ff = w1g.shape[2]
    n_f = ff // tf

    def m_eff(m, nu):
        return jnp.minimum(m, jnp.maximum(nu[0] - 1, 0))

    def f_eff(m, f, nu):
        return jnp.where(m < nu[0], f, n_f - 1)

    return pl.pallas_call(
        _ffn_kernel,
        grid_spec=pltpu.PrefetchScalarGridSpec(
            num_scalar_prefetch=2,
            grid=(n_blocks, n_f),
            in_specs=[
                pl.BlockSpec((bm,) + x_slots.shape[1:], lambda m, f, be, nu: (m_eff(m, nu), 0, 0)),
                pl.BlockSpec((1, d, tf), lambda m, f, be, nu: (be[m], 0, f_eff(m, f, nu))),
                pl.BlockSpec((1, d, tf), lambda m, f, be, nu: (be[m], 0, f_eff(m, f, nu))),
                pl.BlockSpec((1, 1, tf), lambda m, f, be, nu: (be[m], 0, f_eff(m, f, nu))),
                pl.BlockSpec((1, 1, tf), lambda m, f, be, nu: (be[m], 0, f_eff(m, f, nu))),
                pl.BlockSpec((1, tf, d), lambda m, f, be, nu: (be[m], f_eff(m, f, nu), 0)),
                pl.BlockSpec((1, 1, d), lambda m, f, be, nu: (be[m], 0, 0)),
            ],
            out_specs=pl.BlockSpec((bm, d), lambda m, f, be, nu: (m, 0))),
        out_shape=jax.ShapeDtypeStruct((n_slots, d), F32),
        compiler_params=_cparams(("arbitrary", "arbitrary")),
        name="moe_experts",
    )(block_expert, n_used, x_slots, w1g, w1l, b1g, b1l, w2, b2)


def _combine_kernel(pos_ref, h_ref, g_ref, lw_ref, lb_ref, y_hbm, o_ref, buf, sem):
    i = pl.program_id(0)
    tm = h_ref.shape[0]

    def issue_step(step, slot):
        def group(gi, carry):
            n0 = pl.multiple_of(gi * GATHER_UNROLL, GATHER_UNROLL)
            for j in range(GATHER_UNROLL):
                for k in range(TOP_K):
                    p = pos_ref[(step * tm + n0 + j) * TOP_K + k]
                    pltpu.make_async_copy(y_hbm.at[pl.ds(p, 1), :], buf.at[slot, pl.ds(k * tm + n0 + j, 1), :],
                                          sem.at[slot]).start()
            return carry

        lax.fori_loop(0, tm // GATHER_UNROLL, group, 0)

    @pl.when(i == 0)
    def _():
        issue_step(0, 0)

    @pl.when(i + 1 < pl.num_programs(0))
    def _():
        issue_step(i + 1, (i + 1) % 2)

    slot = i % 2
    pltpu.make_async_copy(y_hbm.at[pl.ds(0, TOP_K * tm), :], buf.at[slot], sem.at[slot]).wait()
    g = g_ref[...]
    ffn = g[:, 0:1] * buf[slot, 0:tm, :]
    for k in range(1, TOP_K):
        ffn = ffn + g[:, k:k + 1] * buf[slot, k * tm:(k + 1) * tm, :]
    o_ref[...] = _layer_norm(DEEPNORM_ALPHA * h_ref[...] + ffn, lw_ref[...], lb_ref[...], LN_EPS)


def _combine(pos_flat, h, gates, ln_w, ln_b, y_slots, tm=256):
    n, d = h.shape
    return pl.pallas_call(
        _combine_kernel,
        grid_spec=pltpu.PrefetchScalarGridSpec(
            num_scalar_prefetch=1,
            grid=(n // tm,),
            in_specs=[pl.BlockSpec((tm, d), lambda i, pos: (i, 0)),
                      pl.BlockSpec((tm, LANES), lambda i, pos: (i, 0)),
                      pl.BlockSpec((1, d), lambda i, pos: (0, 0)),
                      pl.BlockSpec((1, d), lambda i, pos: (0, 0)),
                      pl.BlockSpec(memory_space=pl.ANY)],
            out_specs=pl.BlockSpec((tm, d), lambda i, pos: (i, 0)),
            scratch_shapes=[pltpu.VMEM((2, TOP_K * tm, d), F32), pltpu.SemaphoreType.DMA((2,))]),
        out_shape=jax.ShapeDtypeStruct((n, d), F32),
        compiler_params=_cparams(("arbitrary",)),
        name="moe_combine_ln",
    )(pos_flat, h, gates, ln_w, ln_b, y_slots)


def _moe(h, h_packed, route_i, gates, counts_f, w1, b1, w2, b2, ln_w, ln_b):
    n_tok, d = h.shape
    n_exp = w1.shape[0]
    n_asg = n_tok * TOP_K
    n_blocks = -(-n_asg // MOE_BM) + n_exp

    counts = counts_f[0, :n_exp].astype(jnp.int32)
    padded = (counts + MOE_BM - 1) // MOE_BM * MOE_BM
    pend = jnp.cumsum(padded)
    pstart = pend - padded
    pos = pstart[route_i[:, 0:TOP_K]] + route_i[:, TOP_K:2 * TOP_K]
    pos_flat = pos.reshape(n_asg)
    n_used = (pend[-1] // MOE_BM).astype(jnp.int32).reshape(1)
    blk = jnp.minimum(jnp.arange(n_blocks, dtype=jnp.int32), n_used[0] - 1)
    block_expert = jnp.minimum(jnp.sum(pend[None, :] <= (blk * MOE_BM)[:, None], axis=1), n_exp - 1).astype(jnp.int32)

    w1g, w1l = _split_w1(w1)
    b1g = b1[:, None, 0::2]
    b1l = b1[:, None, 1::2]
    w2b = _cast_bf16(w2)

    x_slots = _dispatch(h_packed, pos_flat, n_blocks * MOE_BM)
    y_slots = _experts(x_slots, block_expert, n_used, w1g, w1l, b1g, b1l, w2b, b2[:, None, :])
    return _combine(pos_flat, h, gates, ln_w, ln_b, y_slots)


def _pad_to(a, axis, size):
    pad = [(0, 0)] * a.ndim
    pad[axis] = (0, size - a.shape[axis])
    return jnp.pad(a, pad)


def kernel(x, w_in, mu_shift, w0, w2_decay, a0, a2_iclr, g2_gate, k_k, k_a, r_k, lnx_w, lnx_b, w_pool, pool_scale,
           w_out, ln1_w, ln1_b, router_w, router_b, w1_exp, b1_exp, w2_exp, b2_exp, ln2_w, ln2_b):
    batch, seq, d = x.shape
    assert w_in.shape[0] == 1, "one layer"
    x2d = x.reshape(batch * seq, d)
    c_xw = RKV_COLS
    c_xa = c_xw + DECAY_LORA
    c_xg = c_xa + ICLR_LORA
    c_pool = c_xg + GATE_LORA

    def lora_layout(a):
        return jnp.concatenate([_pad_to(a[:, c_xw:c_xa], 1, LORA_XA - LORA_XW),
                                _pad_to(a[:, c_xa:c_xg], 1, LORA_XG - LORA_XA),
                                _pad_to(a[:, c_xg:c_pool], 1, LORA_PAD - LORA_XG)], axis=1)

    wi = w_in[0]
    w_p = jnp.concatenate([wi[:, :RKV_COLS], wi[:, c_pool:], lora_layout(wi)], axis=1).astype(BF16)
    mu = mu_shift[0][None, :]
    row = lambda a: a.reshape(1, -1)

    proj = _in_proj(x2d, w_p)
    y_rwkv = _rwkv_group(
        proj, batch, seq, mu[:, :RKV_COLS], lora_layout(mu), row(w0[0]),
        _pad_to(w2_decay[0], 0, LANES).astype(BF16), row(a0[0]), _pad_to(a2_iclr[0], 0, LANES).astype(BF16),
        _pad_to(g2_gate[0], 0, LORA_PAD - LORA_XG).astype(BF16), row(k_k[0]), row(k_a[0]), row(r_k[0]),
        row(lnx_w[0]), row(lnx_b[0]))
    y_pool = _pool_group(proj, batch, seq, w_pool[0].astype(BF16), row(pool_scale[0]))

    n_exp = router_w.shape[2]
    router_w_p = _pad_to(router_w[0], 1, LANES)
    router_b_p = jnp.concatenate([router_b[0], jnp.full((LANES - n_exp,), -1e30, F32)])[None, :]
    h1, h1_packed, route_i, gates, counts = _mix_ln_router(y_rwkv, y_pool, x2d, w_out[0].astype(BF16), row(ln1_w[0]),
                                                row(ln1_b[0]), router_w_p, router_b_p)
    out = _moe(h1, h1_packed, route_i, gates, counts, w1_exp[0], b1_exp[0], w2_exp[0], b2_exp[0], row(ln2_w[0]), row(ln2_b[0]))
    return out.reshape(batch, seq, d)
```

```python
import jax
import jax.numpy as jnp
from jax import lax
from jax.experimental import pallas as pl
from jax.experimental.pallas import tpu as pltpu

F32 = jnp.float32
BF16 = jnp.bfloat16

RWKV_WIDTH = 1024
HEAD = 64
DECAY_LORA = 64
ICLR_LORA = 64
GATE_LORA = 160
POOL_WINDOWS = (2, 4, 8, 16)
POOL_GROUP_WIDTH = 256
TOP_K = 4
SWIGLU_LIMIT = 7.0
SWIGLU_ALPHA = 1.702
LN_EPS = 1e-5
GN_EPS = 64e-5
DEEPNORM_ALPHA = 2.0 ** 0.25

LANES = 128
LORA_PAD = 512
LORA_XW, LORA_XA, LORA_XG = 0, 128, 256
RKV_COLS = 3 * RWKV_WIDTH
POOL_WIDTH = len(POOL_WINDOWS) * POOL_GROUP_WIDTH
POOL_COL0 = RKV_COLS
LORA_COL0 = POOL_COL0 + POOL_WIDTH
PROJ_COLS = LORA_COL0 + LORA_PAD
CHUNK = 64
SUB = 16
VMEM_LIMIT = 56 * 1024 * 1024


def _cparams(sem):
    return pltpu.CompilerParams(dimension_semantics=sem, vmem_limit_bytes=VMEM_LIMIT)


def _bdot(a, b):
    return jnp.dot(a.astype(BF16), b.astype(BF16), preferred_element_type=F32)


def _bdot_nt(a, b):
    return lax.dot_general(a.astype(BF16), b.astype(BF16), (((1,), (1,)), ((), ())),
                           preferred_element_type=F32)


def _bdot_tn(a, b):
    return lax.dot_general(a.astype(BF16), b.astype(BF16), (((0,), (0,)), ((), ())),
                           preferred_element_type=F32)


def _in_proj_kernel(x_ref, w_ref, o_ref, xb_ref):
    @pl.when(pl.program_id(1) == 0)
    def _():
        xb_ref[...] = x_ref[...].astype(BF16)

    o_ref[...] = jnp.dot(xb_ref[...], w_ref[...], preferred_element_type=F32)


def _in_proj(x2d, w_p, tm=1024, tn=768):
    n, d = x2d.shape
    cols = w_p.shape[1]
    return pl.pallas_call(
        _in_proj_kernel,
        grid=(n // tm, cols // tn),
        in_specs=[pl.BlockSpec((tm, d), lambda i, j: (i, 0)),
                  pl.BlockSpec((d, tn), lambda i, j: (0, j))],
        out_specs=pl.BlockSpec((tm, tn), lambda i, j: (i, j)),
        out_shape=jax.ShapeDtypeStruct((n, cols), F32),
        scratch_shapes=[pltpu.VMEM((tm, d), BF16)],
        compiler_params=_cparams(("arbitrary", "arbitrary")),
        name="in_proj",
    )(x2d, w_p)


def _shifted(x_ref, prev_ref, mu_ref, first_row):
    x = x_ref[...]
    xp = jnp.where(first_row, prev_ref[...], pltpu.roll(x, 1, axis=0))
    prev_ref[...] = x_ref[x.shape[0] - 1:x.shape[0], :]
    return x + (xp - x) * mu_ref[...]


def _rwkv_kernel(r_ref, k_ref, v_ref, lo_ref, mur_ref, muk_ref, muv_ref, mul_ref,
                 w0_ref, w2d_ref, a0_ref, a2_ref, g2_ref, kk_ref, ka_ref, rk_ref, lnw_ref, lnb_ref,
                 o_ref,
                 pr_ref, pk_ref, pv_ref, plo_ref, s_ref,
                 rs_ref, lw_ref, ks_ref, vs_ref, kks_ref, as_ref, gs_ref):
    tb, width = r_ref.shape
    n_pairs = width // LANES
    c2 = 2 * CHUNK

    @pl.when(pl.program_id(2) == 0)
    def _():
        pr_ref[...] = jnp.zeros_like(pr_ref)
        pk_ref[...] = jnp.zeros_like(pk_ref)
        pv_ref[...] = jnp.zeros_like(pv_ref)
        plo_ref[...] = jnp.zeros_like(plo_ref)
        s_ref[...] = jnp.zeros_like(s_ref)

    first_row = lax.broadcasted_iota(jnp.int32, (tb, 1), 0) == 0
    r = _shifted(r_ref, pr_ref, mur_ref, first_row)
    k = _shifted(k_ref, pk_ref, muk_ref, first_row)
    v = _shifted(v_ref, pv_ref, muv_ref, first_row)
    lo = _shifted(lo_ref, plo_ref, mul_ref, first_row)

    xw = jnp.tanh(lo[:, LORA_XW:LORA_XW + LANES])
    xa = lo[:, LORA_XA:LORA_XA + LANES]
    xg = jax.nn.sigmoid(lo[:, LORA_XG:LORA_PAD])
    z = w0_ref[...] + _bdot(xw, w2d_ref[...])
    w_logit = -(jnp.maximum(-z, 0.0) + jnp.log(1.0 + jnp.exp(-jnp.abs(z)))) - 0.5
    a_ic = jax.nn.sigmoid(a0_ref[...] + _bdot(xa, a2_ref[...]))
    rs_ref[...] = r
    lw_ref[...] = -jnp.exp(w_logit)
    ks_ref[...] = k * (1.0 + (a_ic - 1.0) * ka_ref[...])
    vs_ref[...] = v
    kks_ref[...] = k * kk_ref[...]
    as_ref[...] = a_ic
    gs_ref[...] = _bdot(xg, g2_ref[...])

    lane = lax.broadcasted_iota(jnp.int32, (CHUNK, LANES), 1)
    head0 = lane < HEAD
    row = lax.broadcasted_iota(jnp.int32, (c2, c2), 0)
    col = lax.broadcasted_iota(jnp.int32, (c2, c2), 1)
    strict = row > col
    incl = row >= col
    same_sub = (row // SUB) == (col // SUB)
    eye = (row == col).astype(F32)
    tri = (lax.broadcasted_iota(jnp.int32, (CHUNK, CHUNK), 0)
           >= lax.broadcasted_iota(jnp.int32, (CHUNK, CHUNK), 1)).astype(F32)
    def head_sum(x):
        s0 = jnp.sum(jnp.where(head0, x, 0.0), axis=-1, keepdims=True)
        s1 = jnp.sum(jnp.where(head0, 0.0, x), axis=-1, keepdims=True)
        return jnp.where(head0, s0, s1)

    def stack(x):
        return jnp.concatenate([jnp.where(head0, x, 0.0), jnp.where(head0, 0.0, x)], axis=0)

    def operands(rc, lw, lwi, kc, vc, kk, ac):
        kkn = kk * lax.rsqrt(jnp.maximum(head_sum(kk * kk), 1e-24))
        b_ = kkn * ac
        tot = lwi[CHUNK - 1:CHUNK, :]
        e_inv = jnp.exp(-lwi)
        e_out = jnp.exp(tot - lwi)
        at = stack(-kkn * jnp.exp(lwi - lw)).astype(BF16)
        rt = stack(rc * jnp.exp(lwi)).astype(BF16)
        bk = jnp.concatenate([stack(b_ * e_inv), stack(kc * e_inv)], axis=0).astype(BF16)
        bkh = jnp.concatenate([stack(b_ * e_out), stack(kc * e_out)], axis=0).astype(BF16)
        return at, rt, bk, bkh, stack(vc).astype(BF16), jnp.exp(tot)

    def finish(y_st, rc, kc, vc, gate, rk, lnw, lnb):
        y = y_st[:CHUNK, :] + y_st[CHUNK:, :]
        mu = head_sum(y) * (1.0 / HEAD)
        yc = y - mu
        var = head_sum(yc * yc) * (1.0 / HEAD)
        yn = yc * lax.rsqrt(var + GN_EPS) * lnw + lnb
        yn = yn + head_sum(rc * kc * rk) * vc
        return (yn * gate).astype(o_ref.dtype)

    def chunk(c, carry):
        sl = pl.ds(pl.multiple_of(c * CHUNK, CHUNK), CHUNK)
        lw = lw_ref[sl, :]
        lwi = jnp.dot(tri, lw, preferred_element_type=F32, precision=lax.Precision.HIGHEST)
        pairs = range(n_pairs)
        ls = [slice(p * LANES, (p + 1) * LANES) for p in pairs]
        ops = [operands(rs_ref[sl, l], lw[:, l], lwi[:, l], ks_ref[sl, l], vs_ref[sl, l], kks_ref[sl, l],
                        as_ref[sl, l]) for l in ls]
        at, rt, bk, bkh, vst, dec = (list(z) for z in zip(*ops))
        a_all = [_bdot_nt(jnp.concatenate([at[p], rt[p]], axis=0), bk[p]) for p in pairs]
        a_ab = [jnp.where(strict, a[:c2, :c2], 0.0) for a in a_all]
        a_ak = [jnp.where(strict, a[:c2, c2:], 0.0) for a in a_all]
        a_r = [jnp.where(jnp.concatenate([incl, incl], axis=1), a[c2:, :], 0.0).astype(BF16) for a in a_all]
        akv = [_bdot(a_ak[p], vst[p]).astype(BF16) for p in pairs]

        d1 = [jnp.where(same_sub, a, 0.0) for a in a_ab]
        l1 = [a_ab[p] - d1[p] for p in pairs]
        d2 = [_bdot(d, d) for d in d1]
        d4 = [_bdot(d, d) for d in d2]
        p12 = [_bdot(eye + d1[p], eye + d2[p]) for p in pairs]
        d8 = [_bdot(d, d) for d in d4]
        p48 = [_bdot(eye + d4[p], eye + d8[p]) for p in pairs]
        t_d = [_bdot(p12[p], p48[p]).astype(BF16) for p in pairs]
        m1 = [_bdot(t_d[p], l1[p]) for p in pairs]
        m2 = [_bdot(m, m) for m in m1]
        q = [_bdot(eye + m1[p], eye + m2[p]) for p in pairs]
        t_inv = [_bdot(q[p], t_d[p]) for p in pairs]
        tx = [_bdot(t_inv[p], jnp.concatenate([at[p], akv[p]], axis=1)) for p in pairs]

        s_bd = [s_ref[p] for p in pairs]
        wr = [_bdot_nt(jnp.concatenate([tx[p][:, :LANES].astype(BF16), rt[p]], axis=0), s_bd[p]) for p in pairs]
        uv = [jnp.concatenate([(wr[p][:c2, :] + tx[p][:, LANES:]).astype(BF16), vst[p]], axis=0) for p in pairs]
        y_st = [wr[p][c2:, :] + _bdot(a_r[p], uv[p]) for p in pairs]
        for p in pairs:
            s_ref[p] = s_bd[p] * dec[p] + _bdot_tn(uv[p], bkh[p])
        for p, l in zip(pairs, ls):
            o_ref[sl, l] = finish(y_st[p], rs_ref[sl, l], ks_ref[sl, l], vs_ref[sl, l], gs_ref[sl, l],
                                  rk_ref[:, l], lnw_ref[:, l], lnb_ref[:, l])
        return carry

    lax.fori_loop(0, tb // CHUNK, chunk, 0)


RWKV_PAIRS_PER_STEP = 8


def _rwkv_group(proj, batch, seq, mu_rkv, mu_lora, w0, w2d, a0, a2, g2, k_k, k_a, r_k, lnx_w, lnx_b, tb=512,
                pairs=RWKV_PAIRS_PER_STEP):
    width = pairs * LANES
    n_g = RWKV_WIDTH // width
    n_tb = seq // tb
    row = lambda b, h, t: b * n_tb + t
    vec = lambda off: pl.BlockSpec((1, width), lambda b, h, t: (0, off + h))
    lora_blk = LORA_COL0 // LORA_PAD
    in_specs = [
        pl.BlockSpec((tb, width), lambda b, h, t: (row(b, h, t), h)),
        pl.BlockSpec((tb, width), lambda b, h, t: (row(b, h, t), n_g + h)),
        pl.BlockSpec((tb, width), lambda b, h, t: (row(b, h, t), 2 * n_g + h)),
        pl.BlockSpec((tb, LORA_PAD), lambda b, h, t: (row(b, h, t), lora_blk)),
        vec(0), vec(n_g), vec(2 * n_g),
        pl.BlockSpec((1, LORA_PAD), lambda b, h, t: (0, 0)),
        vec(0),
        pl.BlockSpec((LANES, width), lambda b, h, t: (0, h)),
        vec(0),
        pl.BlockSpec((LANES, width), lambda b, h, t: (0, h)),
        pl.BlockSpec((LORA_PAD - LORA_XG, width), lambda b, h, t: (0, h)),
        vec(0), vec(0), vec(0), vec(0), vec(0),
    ]
    blk = lambda: pltpu.VMEM((tb, width), F32)
    return pl.pallas_call(
        _rwkv_kernel,
        grid=(batch, n_g, n_tb),
        in_specs=in_specs,
        out_specs=pl.BlockSpec((tb, width), lambda b, h, t: (row(b, h, t), h)),
        out_shape=jax.ShapeDtypeStruct((batch * seq, RWKV_WIDTH), BF16),
        scratch_shapes=[pltpu.VMEM((1, width), F32), pltpu.VMEM((1, width), F32), pltpu.VMEM((1, width), F32),
                        pltpu.VMEM((1, LORA_PAD), F32), pltpu.VMEM((pairs, 2 * CHUNK, LANES), F32),
                        blk(), blk(), blk(), blk(), blk(), blk(), blk()],
        compiler_params=_cparams(("arbitrary", "arbitrary", "arbitrary")),
        name="rwkv7_group",
    )(proj, proj, proj, proj, mu_rkv, mu_rkv, mu_rkv, mu_lora, w0, w2d, a0, a2, g2, k_k, k_a, r_k, lnx_w, lnx_b)


HALO = 16


def _pool_kernel(p_ref, w_ref, sc_ref, o_ref, ext_ref):
    tb = p_ref.shape[0]
    t = pl.program_id(1)

    @pl.when(t == 0)
    def _():
        ext_ref[0:HALO, :] = jnp.zeros((HALO, ext_ref.shape[1]), F32)

    @pl.when(t > 0)
    def _():
        ext_ref[0:HALO, :] = ext_ref[tb:tb + HALO, :]

    ext_ref[HALO:HALO + tb, :] = p_ref[...]
    t_idx = t * tb + lax.broadcasted_iota(jnp.int32, (tb, 1), 0)
    for gi, win in enumerate(POOL_WINDOWS):
        cs = slice(gi * POOL_GROUP_WIDTH, (gi + 1) * POOL_GROUP_WIDTH)
        acc = ext_ref[HALO:HALO + tb, cs]
        for d in range(1, win):
            acc = acc + ext_ref[HALO - d:HALO - d + tb, cs]
        count = jnp.minimum(t_idx + 1, win).astype(F32)
        pooled = acc / count - ext_ref[HALO:HALO + tb, cs]
        mixed = _bdot(pooled, w_ref[gi])
        o_ref[:, cs] = (mixed * sc_ref[:, cs]).astype(o_ref.dtype)


def _pool_group(proj, batch, seq, w_pool_b, pool_scale, tb=512):
    n_tb = seq // tb
    col_blk = POOL_COL0 // POOL_WIDTH
    return pl.pallas_call(
        _pool_kernel,
        grid=(batch, n_tb),
        in_specs=[pl.BlockSpec((tb, POOL_WIDTH), lambda b, t: (b * n_tb + t, col_blk)),
                  pl.BlockSpec(w_pool_b.shape, lambda b, t: (0, 0, 0)),
                  pl.BlockSpec((1, POOL_WIDTH), lambda b, t: (0, 0))],
        out_specs=pl.BlockSpec((tb, POOL_WIDTH), lambda b, t: (b * n_tb + t, 0)),
        out_shape=jax.ShapeDtypeStruct((batch * seq, POOL_WIDTH), BF16),
        scratch_shapes=[pltpu.VMEM((tb + HALO, POOL_WIDTH), F32)],
        compiler_params=_cparams(("arbitrary", "arbitrary")),
        name="pool_group",
    )(proj, w_pool_b, pool_scale)


def _layer_norm(x, w, b, eps):
    mu = jnp.mean(x, axis=-1, keepdims=True)
    xc = x - mu
    var = jnp.mean(xc * xc, axis=-1, keepdims=True)
    return xc * lax.rsqrt(var + eps) * w + b


def _mix_kernel(yr_ref, yp_ref, x_ref, wo_ref, lw_ref, lb_ref, rwh_ref, rwl_ref, rb_ref,
                h_ref, hp_ref, ri_ref, g_ref, cnt_ref, carry_ref):
    tm = x_ref.shape[0]
    half = yr_ref.shape[1]

    @pl.when(pl.program_id(0) == 0)
    def _():
        carry_ref[...] = jnp.zeros_like(carry_ref)

    mix = (jnp.dot(yr_ref[...], wo_ref[0:half, :], preferred_element_type=F32)
           + jnp.dot(yp_ref[...], wo_ref[half:, :], preferred_element_type=F32))
    h = _layer_norm(DEEPNORM_ALPHA * x_ref[...] + mix, lw_ref[...], lb_ref[...], LN_EPS)
    h_ref[...] = h
    half_d = h.shape[1] // 2
    lo = pltpu.bitcast(h[:, :half_d].astype(BF16).astype(F32), jnp.uint32) >> 16
    hi = pltpu.bitcast(h[:, half_d:].astype(BF16).astype(F32), jnp.uint32)
    words = lo | hi
    chunks = jnp.stack([words[:, c * LANES:(c + 1) * LANES] for c in range(hp_ref.shape[1])], axis=0)
    hp_ref[...] = pltpu.einshape("ctl->tcl", chunks)

    h_hi = h.astype(BF16)
    h_lo = (h - h_hi.astype(F32)).astype(BF16)
    logits = (jnp.dot(h_hi, rwh_ref[...], preferred_element_type=F32)
              + jnp.dot(h_lo, rwh_ref[...], preferred_element_type=F32)
              + jnp.dot(h_hi, rwl_ref[...], preferred_element_type=F32)) + rb_ref[...]
    lane = lax.broadcasted_iota(jnp.int32, (tm, LANES), 1).astype(F32)
    idxs, vals = [], []
    left = logits
    for _ in range(TOP_K):
        m = jnp.max(left, axis=-1, keepdims=True)
        idx = jnp.min(jnp.where(left == m, lane, float(LANES)), axis=-1, keepdims=True)
        idxs.append(idx)
        vals.append(m)
        left = jnp.where(lane == idx, -jnp.inf, left)
    exps = [jnp.exp(v - vals[0]) for v in vals]
    denom = exps[0] + exps[1] + exps[2] + exps[3]

    onehot = jnp.zeros((tm, LANES), F32)
    for idx in idxs:
        onehot = onehot + (lane == idx).astype(F32)
    tri = (lax.broadcasted_iota(jnp.int32, (tm, tm), 0)
           > lax.broadcasted_iota(jnp.int32, (tm, tm), 1)).astype(BF16)
    before = jnp.dot(tri, onehot.astype(BF16), preferred_element_type=F32) + carry_ref[...]
    carry_ref[...] = carry_ref[...] + jnp.sum(onehot, axis=0, keepdims=True)
    cnt_ref[...] = carry_ref[...]

    ri = jnp.zeros((tm, LANES), F32)
    gt = jnp.zeros((tm, LANES), F32)
    for k in range(TOP_K):
        rank = jnp.sum(jnp.where(lane == idxs[k], before, 0.0), axis=-1, keepdims=True)
        ri = jnp.where(lane == float(k), idxs[k], ri)
        ri = jnp.where(lane == float(TOP_K + k), rank, ri)
        gt = jnp.where(lane == float(k), exps[k] / denom, gt)
    ri_ref[...] = ri.astype(jnp.int32)
    g_ref[...] = gt


def _mix_ln_router(y_rwkv, y_pool, x2d, w_out_b, ln_w, ln_b, router_w_p, router_b_p, tm=512):
    n, d = x2d.shape
    rw_hi = router_w_p.astype(BF16)
    rw_lo = (router_w_p - rw_hi.astype(F32)).astype(BF16)
    half = y_rwkv.shape[1]
    const = lambda shape: pl.BlockSpec(shape, lambda i: (0, 0))
    return pl.pallas_call(
        _mix_kernel,
        grid=(n // tm,),
        in_specs=[pl.BlockSpec((tm, half), lambda i: (i, 0)),
                  pl.BlockSpec((tm, y_pool.shape[1]), lambda i: (i, 0)),
                  pl.BlockSpec((tm, d), lambda i: (i, 0)),
                  const(w_out_b.shape), const((1, d)), const((1, d)),
                  const(router_w_p.shape), const(router_w_p.shape), const((1, LANES))],
        out_specs=[pl.BlockSpec((tm, d), lambda i: (i, 0)),
                   pl.BlockSpec((tm, d // 2 // LANES, LANES), lambda i: (i, 0, 0)),
                   pl.BlockSpec((tm, LANES), lambda i: (i, 0)),
                   pl.BlockSpec((tm, LANES), lambda i: (i, 0)),
                   const((1, LANES))],
        out_shape=[jax.ShapeDtypeStruct((n, d), F32),
                   jax.ShapeDtypeStruct((n, d // 2 // LANES, LANES), jnp.uint32),
                   jax.ShapeDtypeStruct((n, LANES), jnp.int32),
                   jax.ShapeDtypeStruct((n, LANES), F32),
                   jax.ShapeDtypeStruct((1, LANES), F32)],
        scratch_shapes=[pltpu.VMEM((1, LANES), F32)],
        compiler_params=_cparams(("arbitrary",)),
        name="mix_ln_router",
    )(y_rwkv, y_pool, x2d, w_out_b, ln_w, ln_b, rw_hi, rw_lo, router_b_p)


MOE_BM = 512
GATHER_UNROLL = 8


MXU_DIM = 256
OUT_CHUNK = 512


def _split_w1_kernel(w_ref, g_ref, l_ref):
    half = MXU_DIM // 2
    src = lax.broadcasted_iota(jnp.int32, (MXU_DIM, MXU_DIM), 0)
    dst = lax.broadcasted_iota(jnp.int32, (MXU_DIM, MXU_DIM), 1)
    perm = (src == jnp.where(dst < half, 2 * dst, 2 * (dst - half) + 1)).astype(BF16)
    for c in range(w_ref.shape[2] // MXU_DIM):
        w = w_ref[0, :, c * MXU_DIM:(c + 1) * MXU_DIM].astype(BF16)
        o = jnp.dot(w, perm, preferred_element_type=F32)
        g_ref[0, :, c * half:(c + 1) * half] = o[:, :half].astype(BF16)
        l_ref[0, :, c * half:(c + 1) * half] = o[:, half:].astype(BF16)


def _split_w1(w1, tr=512, tc=2048):
    n_exp, d, f2 = w1.shape
    out = jax.ShapeDtypeStruct((n_exp, d, f2 // 2), BF16)
    return pl.pallas_call(
        _split_w1_kernel,
        grid=(n_exp, d // tr, f2 // tc),
        in_specs=[pl.BlockSpec((1, tr, tc), lambda e, i, j: (e, i, j))],
        out_specs=[pl.BlockSpec((1, tr, tc // 2), lambda e, i, j: (e, i, j)),
                   pl.BlockSpec((1, tr, tc // 2), lambda e, i, j: (e, i, j))],
        out_shape=[out, out],
        compiler_params=_cparams(("arbitrary", "arbitrary", "arbitrary")),
        name="split_w1",
    )(w1)


def _cast_kernel(w_ref, o_ref):
    o_ref[...] = w_ref[...].astype(o_ref.dtype)


def _cast_bf16(w, tr=512):
    n_exp, r, c = w.shape
    return pl.pallas_call(
        _cast_kernel,
        grid=(n_exp, r // tr),
        in_specs=[pl.BlockSpec((1, tr, c), lambda e, i: (e, i, 0))],
        out_specs=pl.BlockSpec((1, tr, c), lambda e, i: (e, i, 0)),
        out_shape=jax.ShapeDtypeStruct(w.shape, BF16),
        compiler_params=_cparams(("arbitrary", "arbitrary")),
        name="cast_w2",
    )(w)


def _scatter_kernel(pos_ref, hp_ref, zero_hbm, x_hbm, sem):
    del zero_hbm
    i = pl.program_id(0)
    tm = hp_ref.shape[0]

    def group(gi, carry):
        n0 = pl.multiple_of(gi * GATHER_UNROLL, GATHER_UNROLL)
        for j in range(GATHER_UNROLL):
            for k in range(TOP_K):
                p = pos_ref[(i * tm + n0 + j) * TOP_K + k]
                pltpu.make_async_copy(hp_ref.at[n0 + j], x_hbm.at[p], sem).start()
        return carry

    lax.fori_loop(0, tm // GATHER_UNROLL, group, 0)
    for k in range(TOP_K):
        pltpu.make_async_copy(hp_ref, x_hbm.at[pl.ds(0, tm)], sem).wait()


def _dispatch(h_packed, pos_flat, n_slots, tm=512):
    n = h_packed.shape[0]
    tile = h_packed.shape[1:]
    return pl.pallas_call(
        _scatter_kernel,
        grid_spec=pltpu.PrefetchScalarGridSpec(
            num_scalar_prefetch=1,
            grid=(n // tm,),
            in_specs=[pl.BlockSpec((tm,) + tile, lambda i, pos: (i, 0, 0)),
                      pl.BlockSpec(memory_space=pl.ANY)],
            out_specs=pl.BlockSpec(memory_space=pl.ANY),
            scratch_shapes=[pltpu.SemaphoreType.DMA(())]),
        out_shape=jax.ShapeDtypeStruct((n_slots,) + tile, h_packed.dtype),
        input_output_aliases={2: 0},
        compiler_params=_cparams(("arbitrary",)),
        name="moe_dispatch",
    )(pos_flat, h_packed, jnp.zeros((n_slots,) + tile, h_packed.dtype))


ROW_GROUP = 128


def _ffn_hidden(words, w1g_ref, w1l_ref, b1g_ref, b1l_ref):
    packed = pltpu.einshape("tcl->ctl", words)
    chunks = [packed[c] for c in range(words.shape[1])]
    x = jnp.concatenate([pltpu.bitcast(w << 16, F32).astype(BF16) for w in chunks]
                        + [pltpu.bitcast(w & jnp.uint32(0xFFFF0000), F32).astype(BF16) for w in chunks], axis=1)
    hg = jnp.dot(x, w1g_ref[0], preferred_element_type=F32) + b1g_ref[0]
    hl = jnp.dot(x, w1l_ref[0], preferred_element_type=F32) + b1l_ref[0]
    x_glu = jnp.minimum(hg, SWIGLU_LIMIT)
    x_lin = jnp.clip(hl, -SWIGLU_LIMIT, SWIGLU_LIMIT)
    return (x_glu * jax.nn.sigmoid(SWIGLU_ALPHA * x_glu) * (x_lin + 1.0)).astype(BF16)


def _ffn_kernel(be_ref, nu_ref, bv_ref, x_ref, w1g_ref, w1l_ref, b1g_ref, b1l_ref, w2_ref, b2_ref, o_ref):
    m = pl.program_id(0)
    f = pl.program_id(1)
    bm = o_ref.shape[0]
    valid = bv_ref[m]

    @pl.when(f == 0)
    def _():
        o_ref[...] = jnp.broadcast_to(b2_ref[0], o_ref.shape)

    whole = valid > bm - ROW_GROUP

    @pl.when(whole)
    def _():
        act = _ffn_hidden(x_ref[...], w1g_ref, w1l_ref, b1g_ref, b1l_ref)
        for c in range(0, o_ref.shape[1], OUT_CHUNK):
            o_ref[:, c:c + OUT_CHUNK] += jnp.dot(act, w2_ref[0, :, c:c + OUT_CHUNK], preferred_element_type=F32)

    @pl.when((valid > 0) & jnp.logical_not(whole))
    def _():
        def group(g, carry):
            rows = pl.ds(pl.multiple_of(g * ROW_GROUP, ROW_GROUP), ROW_GROUP)
            act = _ffn_hidden(x_ref[rows], w1g_ref, w1l_ref, b1g_ref, b1l_ref)
            for c in range(0, o_ref.shape[1], OUT_CHUNK):
                o_ref[rows, c:c + OUT_CHUNK] += jnp.dot(act, w2_ref[0, :, c:c + OUT_CHUNK],
                                                        preferred_element_type=F32)
            return carry

        lax.fori_loop(0, (valid + ROW_GROUP - 1) // ROW_GROUP, group, 0)


def _experts(x_slots, block_expert, n_used, block_valid, w1g, w1l, b1g, b1l, w2, b2, bm=MOE_BM, tf=1024):
    d = w1g.shape[1]
    n_slots = x_slots.shape[0]
    n_blocks = n_slots // bm
    ff = w1g.shape[2]
    n_f = ff // tf

    def m_eff(m, nu):
        return jnp.minimum(m, jnp.maximum(nu[0] - 1, 0))

    def f_eff(m, f, nu):
        return jnp.where(m < nu[0], f, n_f - 1)

    return pl.pallas_call(
        _ffn_kernel,
        grid_spec=pltpu.PrefetchScalarGridSpec(
            num_scalar_prefetch=3,
            grid=(n_blocks, n_f),
            in_specs=[
                pl.BlockSpec((bm,) + x_slots.shape[1:], lambda m, f, be, nu, bv: (m_eff(m, nu), 0, 0)),
                pl.BlockSpec((1, d, tf), lambda m, f, be, nu, bv: (be[m], 0, f_eff(m, f, nu))),
                pl.BlockSpec((1, d, tf), lambda m, f, be, nu, bv: (be[m], 0, f_eff(m, f, nu))),
                pl.BlockSpec((1, 1, tf), lambda m, f, be, nu, bv: (be[m], 0, f_eff(m, f, nu))),
                pl.BlockSpec((1, 1, tf), lambda m, f, be, nu, bv: (be[m], 0, f_eff(m, f, nu))),
                pl.BlockSpec((1, tf, d), lambda m, f, be, nu, bv: (be[m], f_eff(m, f, nu), 0)),
                pl.BlockSpec((1, 1, d), lambda m, f, be, nu, bv: (be[m], 0, 0)),
            ],
            out_specs=pl.BlockSpec((bm, d), lambda m, f, be, nu, bv: (m, 0))),
        out_shape=jax.ShapeDtypeStruct((n_slots, d), F32),
        compiler_params=_cparams(("arbitrary", "arbitrary")),
        name="moe_experts",
    )(block_expert, n_used, block_valid, x_slots, w1g, w1l, b1g, b1l, w2, b2)


def _combine_kernel(pos_ref, h_ref, g_ref, lw_ref, lb_ref, y_hbm, o_ref, buf, sem):
    i = pl.program_id(0)
    tm = h_ref.shape[0]

    def issue_step(step, slot):
        def group(gi, carry):
            n0 = pl.multiple_of(gi * GATHER_UNROLL, GATHER_UNROLL)
            for j in range(GATHER_UNROLL):
                for k in range(TOP_K):
                    p = pos_ref[(step * tm + n0 + j) * TOP_K + k]
                    pltpu.make_async_copy(y_hbm.at[pl.ds(p, 1), :], buf.at[slot, pl.ds(k * tm + n0 + j, 1), :],
                                          sem.at[slot]).start()
            return carry

        lax.fori_loop(0, tm // GATHER_UNROLL, group, 0)

    @pl.when(i == 0)
    def _():
        issue_step(0, 0)

    @pl.when(i + 1 < pl.num_programs(0))
    def _():
        issue_step(i + 1, (i + 1) % 2)

    slot = i % 2
    pltpu.make_async_copy(y_hbm.at[pl.ds(0, TOP_K * tm), :], buf.at[slot], sem.at[slot]).wait()
    g = g_ref[...]
    ffn = g[:, 0:1] * buf[slot, 0:tm, :]
    for k in range(1, TOP_K):
        ffn = ffn + g[:, k:k + 1] * buf[slot, k * tm:(k + 1) * tm, :]
    o_ref[...] = _layer_norm(DEEPNORM_ALPHA * h_ref[...] + ffn, lw_ref[...], lb_ref[...], LN_EPS)


def _combine(pos_flat, h, gates, ln_w, ln_b, y_slots, tm=256):
    n, d = h.shape
    return pl.pallas_call(
        _combine_kernel,
        grid_spec=pltpu.PrefetchScalarGridSpec(
            num_scalar_prefetch=1,
            grid=(n // tm,),
            in_specs=[pl.BlockSpec((tm, d), lambda i, pos: (i, 0)),
                      pl.BlockSpec((tm, LANES), lambda i, pos: (i, 0)),
                      pl.BlockSpec((1, d), lambda i, pos: (0, 0)),
                      pl.BlockSpec((1, d), lambda i, pos: (0, 0)),
                      pl.BlockSpec(memory_space=pl.ANY)],
            out_specs=pl.BlockSpec((tm, d), lambda i, pos: (i, 0)),
            scratch_shapes=[pltpu.VMEM((2, TOP_K * tm, d), F32), pltpu.SemaphoreType.DMA((2,))]),
        out_shape=jax.ShapeDtypeStruct((n, d), F32),
        compiler_params=_cparams(("arbitrary",)),
        name="moe_combine_ln",
    )(pos_flat, h, gates, ln_w, ln_b, y_slots)


def _moe(h, h_packed, route_i, gates, counts_f, w1, b1, w2, b2, ln_w, ln_b):
    n_tok, d = h.shape
    n_exp = w1.shape[0]
    n_asg = n_tok * TOP_K
    n_blocks = -(-n_asg // MOE_BM) + n_exp

    counts = counts_f[0, :n_exp].astype(jnp.int32)
    padded = (counts + MOE_BM - 1) // MOE_BM * MOE_BM
    pend = jnp.cumsum(padded)
    pstart = pend - padded
    pos = pstart[route_i[:, 0:TOP_K]] + route_i[:, TOP_K:2 * TOP_K]
    pos_flat = pos.reshape(n_asg)
    n_used = (pend[-1] // MOE_BM).astype(jnp.int32).reshape(1)
    blk = jnp.minimum(jnp.arange(n_blocks, dtype=jnp.int32), n_used[0] - 1)
    block_expert = jnp.minimum(jnp.sum(pend[None, :] <= (blk * MOE_BM)[:, None], axis=1), n_exp - 1).astype(jnp.int32)
    block_valid = jnp.clip((pstart + counts)[block_expert] - jnp.arange(n_blocks, dtype=jnp.int32) * MOE_BM,
                           0, MOE_BM).astype(jnp.int32)

    w1g, w1l = _split_w1(w1)
    b1g = b1[:, None, 0::2]
    b1l = b1[:, None, 1::2]
    w2b = _cast_bf16(w2)

    x_slots = _dispatch(h_packed, pos_flat, n_blocks * MOE_BM)
    y_slots = _experts(x_slots, block_expert, n_used, block_valid, w1g, w1l, b1g, b1l, w2b, b2[:, None, :])
    return _combine(pos_flat, h, gates, ln_w, ln_b, y_slots)


def _pad_to(a, axis, size):
    pad = [(0, 0)] * a.ndim
    pad[axis] = (0, size - a.shape[axis])
    return jnp.pad(a, pad)


def kernel(x, w_in, mu_shift, w0, w2_decay, a0, a2_iclr, g2_gate, k_k, k_a, r_k, lnx_w, lnx_b, w_pool, pool_scale,
           w_out, ln1_w, ln1_b, router_w, router_b, w1_exp, b1_exp, w2_exp, b2_exp, ln2_w, ln2_b):
    batch, seq, d = x.shape
    assert w_in.shape[0] == 1, "one layer"
    x2d = x.reshape(batch * seq, d)
    c_xw = RKV_COLS
    c_xa = c_xw + DECAY_LORA
    c_xg = c_xa + ICLR_LORA
    c_pool = c_xg + GATE_LORA

    def lora_layout(a):
        return jnp.concatenate([_pad_to(a[:, c_xw:c_xa], 1, LORA_XA - LORA_XW),
                                _pad_to(a[:, c_xa:c_xg], 1, LORA_XG - LORA_XA),
                                _pad_to(a[:, c_xg:c_pool], 1, LORA_PAD - LORA_XG)], axis=1)

    wi = w_in[0]
    w_p = jnp.concatenate([wi[:, :RKV_COLS], wi[:, c_pool:], lora_layout(wi)], axis=1).astype(BF16)
    mu = mu_shift[0][None, :]
    row = lambda a: a.reshape(1, -1)

    proj = _in_proj(x2d, w_p)
    y_rwkv = _rwkv_group(
        proj, batch, seq, mu[:, :RKV_COLS], lora_layout(mu), row(w0[0]),
        _pad_to(w2_decay[0], 0, LANES).astype(BF16), row(a0[0]), _pad_to(a2_iclr[0], 0, LANES).astype(BF16),
        _pad_to(g2_gate[0], 0, LORA_PAD - LORA_XG).astype(BF16), row(k_k[0]), row(k_a[0]), row(r_k[0]),
        row(lnx_w[0]), row(lnx_b[0]))
    y_pool = _pool_group(proj, batch, seq, w_pool[0].astype(BF16), row(pool_scale[0]))

    n_exp = router_w.shape[2]
    router_w_p = _pad_to(router_w[0], 1, LANES)
    router_b_p = jnp.concatenate([router_b[0], jnp.full((LANES - n_exp,), -1e30, F32)])[None, :]
    h1, h1_packed, route_i, gates, counts = _mix_ln_router(y_rwkv, y_pool, x2d, w_out[0].astype(BF16), row(ln1_w[0]),
                                                row(ln1_b[0]), router_w_p, router_b_p)
    out = _moe(h1, h1_packed, route_i, gates, counts, w1_exp[0], b1_exp[0], w2_exp[0], b2_exp[0], row(ln2_w[0]), row(ln2_b[0]))
    return out.reshape(batch, seq, d)
```

```python
import jax
import jax.numpy as jnp
from jax import lax
from jax.experimental import pallas as pl
from jax.experimental.pallas import tpu as pltpu

F32 = jnp.float32
BF16 = jnp.bfloat16

RWKV_WIDTH = 1024
HEAD = 64
DECAY_LORA = 64
ICLR_LORA = 64
GATE_LORA = 160
POOL_WINDOWS = (2, 4, 8, 16)
POOL_GROUP_WIDTH = 256
TOP_K = 4
SWIGLU_LIMIT = 7.0
SWIGLU_ALPHA = 1.702
LN_EPS = 1e-5
GN_EPS = 64e-5
DEEPNORM_ALPHA = 2.0 ** 0.25

LANES = 128
LORA_PAD = 512
LORA_XW, LORA_XA, LORA_XG = 0, 128, 256
RKV_COLS = 3 * RWKV_WIDTH
POOL_WIDTH = len(POOL_WINDOWS) * POOL_GROUP_WIDTH
POOL_COL0 = RKV_COLS
LORA_COL0 = POOL_COL0 + POOL_WIDTH
PROJ_COLS = LORA_COL0 + LORA_PAD
CHUNK = 64
SUB = 16
CHUNKS_PER_ITER = 2
VMEM_LIMIT = 56 * 1024 * 1024


def _cparams(sem):
    return pltpu.CompilerParams(dimension_semantics=sem, vmem_limit_bytes=VMEM_LIMIT)


def _bdot(a, b):
    return jnp.dot(a.astype(BF16), b.astype(BF16), preferred_element_type=F32)


def _bdot_nt(a, b):
    return lax.dot_general(a.astype(BF16), b.astype(BF16), (((1,), (1,)), ((), ())),
                           preferred_element_type=F32)


def _bdot_tn(a, b):
    return lax.dot_general(a.astype(BF16), b.astype(BF16), (((0,), (0,)), ((), ())),
                           preferred_element_type=F32)


def _in_proj_kernel(x_ref, w_ref, o_ref, xb_ref):
    @pl.when(pl.program_id(1) == 0)
    def _():
        xb_ref[...] = x_ref[...].astype(BF16)

    o_ref[...] = jnp.dot(xb_ref[...], w_ref[...], preferred_element_type=F32)


def _in_proj(x2d, w_p, tm=1024, tn=768):
    n, d = x2d.shape
    cols = w_p.shape[1]
    return pl.pallas_call(
        _in_proj_kernel,
        grid=(n // tm, cols // tn),
        in_specs=[pl.BlockSpec((tm, d), lambda i, j: (i, 0)),
                  pl.BlockSpec((d, tn), lambda i, j: (0, j))],
        out_specs=pl.BlockSpec((tm, tn), lambda i, j: (i, j)),
        out_shape=jax.ShapeDtypeStruct((n, cols), F32),
        scratch_shapes=[pltpu.VMEM((tm, d), BF16)],
        compiler_params=_cparams(("arbitrary", "arbitrary")),
        name="in_proj",
    )(x2d, w_p)


def _shifted(x_ref, prev_ref, mu_ref, first_row):
    x = x_ref[...]
    xp = jnp.where(first_row, prev_ref[...], pltpu.roll(x, 1, axis=0))
    prev_ref[...] = x_ref[x.shape[0] - 1:x.shape[0], :]
    return x + (xp - x) * mu_ref[...]


def _rwkv_kernel(r_ref, k_ref, v_ref, lo_ref, mur_ref, muk_ref, muv_ref, mul_ref,
                 w0_ref, w2d_ref, a0_ref, a2_ref, g2_ref, kk_ref, ka_ref, rk_ref, lnw_ref, lnb_ref,
                 o_ref,
                 pr_ref, pk_ref, pv_ref, plo_ref, s_ref,
                 rs_ref, lw_ref, ks_ref, vs_ref, kks_ref, as_ref, gs_ref):
    tb, width = r_ref.shape
    n_pairs = width // LANES
    c2 = 2 * CHUNK

    @pl.when(pl.program_id(2) == 0)
    def _():
        pr_ref[...] = jnp.zeros_like(pr_ref)
        pk_ref[...] = jnp.zeros_like(pk_ref)
        pv_ref[...] = jnp.zeros_like(pv_ref)
        plo_ref[...] = jnp.zeros_like(plo_ref)
        s_ref[...] = jnp.zeros_like(s_ref)

    first_row = lax.broadcasted_iota(jnp.int32, (tb, 1), 0) == 0
    r = _shifted(r_ref, pr_ref, mur_ref, first_row)
    k = _shifted(k_ref, pk_ref, muk_ref, first_row)
    v = _shifted(v_ref, pv_ref, muv_ref, first_row)
    lo = _shifted(lo_ref, plo_ref, mul_ref, first_row)

    xw = jnp.tanh(lo[:, LORA_XW:LORA_XW + LANES])
    xa = lo[:, LORA_XA:LORA_XA + LANES]
    xg = jax.nn.sigmoid(lo[:, LORA_XG:LORA_PAD])
    z = w0_ref[...] + _bdot(xw, w2d_ref[...])
    w_logit = -(jnp.maximum(-z, 0.0) + jnp.log(1.0 + jnp.exp(-jnp.abs(z)))) - 0.5
    a_ic = jax.nn.sigmoid(a0_ref[...] + _bdot(xa, a2_ref[...]))
    rs_ref[...] = r
    lw_ref[...] = -jnp.exp(w_logit)
    ks_ref[...] = k * (1.0 + (a_ic - 1.0) * ka_ref[...])
    vs_ref[...] = v
    kks_ref[...] = k * kk_ref[...]
    as_ref[...] = a_ic
    gs_ref[...] = _bdot(xg, g2_ref[...])

    lane = lax.broadcasted_iota(jnp.int32, (CHUNK, LANES), 1)
    head0 = lane < HEAD
    row = lax.broadcasted_iota(jnp.int32, (c2, c2), 0)
    col = lax.broadcasted_iota(jnp.int32, (c2, c2), 1)
    strict = row > col
    incl = row >= col
    same_sub = (row // SUB) == (col // SUB)
    eye = (row == col).astype(F32)
    tri = (lax.broadcasted_iota(jnp.int32, (CHUNK, CHUNK), 0)
           >= lax.broadcasted_iota(jnp.int32, (CHUNK, CHUNK), 1)).astype(F32)
    def head_sum(x):
        s0 = jnp.sum(jnp.where(head0, x, 0.0), axis=-1, keepdims=True)
        s1 = jnp.sum(jnp.where(head0, 0.0, x), axis=-1, keepdims=True)
        return jnp.where(head0, s0, s1)

    def stack(x):
        return jnp.concatenate([jnp.where(head0, x, 0.0), jnp.where(head0, 0.0, x)], axis=0)

    def operands(rc, lw, lwi, kc, vc, kk, ac):
        kkn = kk * lax.rsqrt(jnp.maximum(head_sum(kk * kk), 1e-24))
        b_ = kkn * ac
        tot = lwi[CHUNK - 1:CHUNK, :]
        e_inv = jnp.exp(-lwi)
        e_out = jnp.exp(tot - lwi)
        at = stack(-kkn * jnp.exp(lwi - lw)).astype(BF16)
        rt = stack(rc * jnp.exp(lwi)).astype(BF16)
        bk = jnp.concatenate([stack(b_ * e_inv), stack(kc * e_inv)], axis=0).astype(BF16)
        bkh = jnp.concatenate([stack(b_ * e_out), stack(kc * e_out)], axis=0).astype(BF16)
        return at, rt, bk, bkh, stack(vc).astype(BF16), jnp.exp(tot)

    def finish(y_st, rc, kc, vc, gate, rk, lnw, lnb):
        y = y_st[:CHUNK, :] + y_st[CHUNK:, :]
        mu = head_sum(y) * (1.0 / HEAD)
        yc = y - mu
        var = head_sum(yc * yc) * (1.0 / HEAD)
        yn = yc * lax.rsqrt(var + GN_EPS) * lnw + lnb
        yn = yn + head_sum(rc * kc * rk) * vc
        return (yn * gate).astype(o_ref.dtype)

    def chunks(it, carry):
        sls = [pl.ds(pl.multiple_of((it * CHUNKS_PER_ITER + u) * CHUNK, CHUNK), CHUNK)
               for u in range(CHUNKS_PER_ITER)]
        lws = [lw_ref[sl, :] for sl in sls]
        lwis = [jnp.dot(tri, lw, preferred_element_type=F32, precision=lax.Precision.HIGHEST) for lw in lws]
        pairs = range(n_pairs)
        ls = [slice(p * LANES, (p + 1) * LANES) for p in pairs]
        items = [(u, p) for u in range(CHUNKS_PER_ITER) for p in pairs]
        n = range(len(items))
        ops = [operands(rs_ref[sls[u], ls[p]], lws[u][:, ls[p]], lwis[u][:, ls[p]], ks_ref[sls[u], ls[p]],
                        vs_ref[sls[u], ls[p]], kks_ref[sls[u], ls[p]], as_ref[sls[u], ls[p]]) for u, p in items]
        at, rt, bk, bkh, vst, dec = (list(z) for z in zip(*ops))
        a_all = [_bdot_nt(jnp.concatenate([at[i], rt[i]], axis=0), bk[i]) for i in n]
        a_ab = [jnp.where(strict, a[:c2, :c2], 0.0) for a in a_all]
        a_ak = [jnp.where(strict, a[:c2, c2:], 0.0) for a in a_all]
        a_r = [jnp.where(jnp.concatenate([incl, incl], axis=1), a[c2:, :], 0.0).astype(BF16) for a in a_all]
        akv = [_bdot(a_ak[i], vst[i]).astype(BF16) for i in n]

        d1 = [jnp.where(same_sub, a, 0.0) for a in a_ab]
        l1 = [a_ab[i] - d1[i] for i in n]
        d2 = [_bdot(d, d) for d in d1]
        d4 = [_bdot(d, d) for d in d2]
        p12 = [_bdot(eye + d1[i], eye + d2[i]) for i in n]
        d8 = [_bdot(d, d) for d in d4]
        p48 = [_bdot(eye + d4[i], eye + d8[i]) for i in n]
        t_d = [_bdot(p12[i], p48[i]).astype(BF16) for i in n]
        m1 = [_bdot(t_d[i], l1[i]) for i in n]
        m2 = [_bdot(m, m) for m in m1]
        q = [_bdot(eye + m1[i], eye + m2[i]) for i in n]
        t_inv = [_bdot(q[i], t_d[i]) for i in n]
        tx = [_bdot(t_inv[i], jnp.concatenate([at[i], akv[i]], axis=1)) for i in n]

        for u in range(CHUNKS_PER_ITER):
            idx = [u * n_pairs + p for p in pairs]
            s_bd = [s_ref[p] for p in pairs]
            wr = [_bdot_nt(jnp.concatenate([tx[i][:, :LANES].astype(BF16), rt[i]], axis=0), s_bd[p])
                  for p, i in zip(pairs, idx)]
            uv = [jnp.concatenate([(wr[p][:c2, :] + tx[i][:, LANES:]).astype(BF16), vst[i]], axis=0)
                  for p, i in zip(pairs, idx)]
            y_st = [wr[p][c2:, :] + _bdot(a_r[i], uv[p]) for p, i in zip(pairs, idx)]
            for p, i in zip(pairs, idx):
                s_ref[p] = s_bd[p] * dec[i] + _bdot_tn(uv[p], bkh[i])
            for p, l in zip(pairs, ls):
                sl = sls[u]
                o_ref[sl, l] = finish(y_st[p], rs_ref[sl, l], ks_ref[sl, l], vs_ref[sl, l], gs_ref[sl, l],
                                      rk_ref[:, l], lnw_ref[:, l], lnb_ref[:, l])
        return carry

    lax.fori_loop(0, tb // (CHUNK * CHUNKS_PER_ITER), chunks, 0)


RWKV_PAIRS_PER_STEP = 8


def _rwkv_group(proj, batch, seq, mu_rkv, mu_lora, w0, w2d, a0, a2, g2, k_k, k_a, r_k, lnx_w, lnx_b, tb=512,
                pairs=RWKV_PAIRS_PER_STEP):
    width = pairs * LANES
    n_g = RWKV_WIDTH // width
    n_tb = seq // tb
    row = lambda b, h, t: b * n_tb + t
    vec = lambda off: pl.BlockSpec((1, width), lambda b, h, t: (0, off + h))
    lora_blk = LORA_COL0 // LORA_PAD
    in_specs = [
        pl.BlockSpec((tb, width), lambda b, h, t: (row(b, h, t), h)),
        pl.BlockSpec((tb, width), lambda b, h, t: (row(b, h, t), n_g + h)),
        pl.BlockSpec((tb, width), lambda b, h, t: (row(b, h, t), 2 * n_g + h)),
        pl.BlockSpec((tb, LORA_PAD), lambda b, h, t: (row(b, h, t), lora_blk)),
        vec(0), vec(n_g), vec(2 * n_g),
        pl.BlockSpec((1, LORA_PAD), lambda b, h, t: (0, 0)),
        vec(0),
        pl.BlockSpec((LANES, width), lambda b, h, t: (0, h)),
        vec(0),
        pl.BlockSpec((LANES, width), lambda b, h, t: (0, h)),
        pl.BlockSpec((LORA_PAD - LORA_XG, width), lambda b, h, t: (0, h)),
        vec(0), vec(0), vec(0), vec(0), vec(0),
    ]
    blk = lambda: pltpu.VMEM((tb, width), F32)
    return pl.pallas_call(
        _rwkv_kernel,
        grid=(batch, n_g, n_tb),
        in_specs=in_specs,
        out_specs=pl.BlockSpec((tb, width), lambda b, h, t: (row(b, h, t), h)),
        out_shape=jax.ShapeDtypeStruct((batch * seq, RWKV_WIDTH), BF16),
        scratch_shapes=[pltpu.VMEM((1, width), F32), pltpu.VMEM((1, width), F32), pltpu.VMEM((1, width), F32),
                        pltpu.VMEM((1, LORA_PAD), F32), pltpu.VMEM((pairs, 2 * CHUNK, LANES), F32),
                        blk(), blk(), blk(), blk(), blk(), blk(), blk()],
        compiler_params=_cparams(("arbitrary", "arbitrary", "arbitrary")),
        name="rwkv7_group",
    )(proj, proj, proj, proj, mu_rkv, mu_rkv, mu_rkv, mu_lora, w0, w2d, a0, a2, g2, k_k, k_a, r_k, lnx_w, lnx_b)


HALO = 16


def _pool_kernel(p_ref, w_ref, sc_ref, o_ref, ext_ref):
    tb = p_ref.shape[0]
    t = pl.program_id(1)

    @pl.when(t == 0)
    def _():
        ext_ref[0:HALO, :] = jnp.zeros((HALO, ext_ref.shape[1]), F32)

    @pl.when(t > 0)
    def _():
        ext_ref[0:HALO, :] = ext_ref[tb:tb + HALO, :]

    ext_ref[HALO:HALO + tb, :] = p_ref[...]
    t_idx = t * tb + lax.broadcasted_iota(jnp.int32, (tb, 1), 0)
    for gi, win in enumerate(POOL_WINDOWS):
        cs = slice(gi * POOL_GROUP_WIDTH, (gi + 1) * POOL_GROUP_WIDTH)
        acc = ext_ref[HALO:HALO + tb, cs]
        for d in range(1, win):
            acc = acc + ext_ref[HALO - d:HALO - d + tb, cs]
        count = jnp.minimum(t_idx + 1, win).astype(F32)
        pooled = acc / count - ext_ref[HALO:HALO + tb, cs]
        mixed = _bdot(pooled, w_ref[gi])
        o_ref[:, cs] = (mixed * sc_ref[:, cs]).astype(o_ref.dtype)


def _pool_group(proj, batch, seq, w_pool_b, pool_scale, tb=512):
    n_tb = seq // tb
    col_blk = POOL_COL0 // POOL_WIDTH
    return pl.pallas_call(
        _pool_kernel,
        grid=(batch, n_tb),
        in_specs=[pl.BlockSpec((tb, POOL_WIDTH), lambda b, t: (b * n_tb + t, col_blk)),
                  pl.BlockSpec(w_pool_b.shape, lambda b, t: (0, 0, 0)),
                  pl.BlockSpec((1, POOL_WIDTH), lambda b, t: (0, 0))],
        out_specs=pl.BlockSpec((tb, POOL_WIDTH), lambda b, t: (b * n_tb + t, 0)),
        out_shape=jax.ShapeDtypeStruct((batch * seq, POOL_WIDTH), BF16),
        scratch_shapes=[pltpu.VMEM((tb + HALO, POOL_WIDTH), F32)],
        compiler_params=_cparams(("arbitrary", "arbitrary")),
        name="pool_group",
    )(proj, w_pool_b, pool_scale)


def _layer_norm(x, w, b, eps):
    mu = jnp.mean(x, axis=-1, keepdims=True)
    xc = x - mu
    var = jnp.mean(xc * xc, axis=-1, keepdims=True)
    return xc * lax.rsqrt(var + eps) * w + b


def _mix_kernel(yr_ref, yp_ref, x_ref, wo_ref, lw_ref, lb_ref, rwh_ref, rwl_ref, rb_ref,
                h_ref, hp_ref, ri_ref, g_ref, cnt_ref, carry_ref):
    tm = x_ref.shape[0]
    half = yr_ref.shape[1]

    @pl.when(pl.program_id(0) == 0)
    def _():
        carry_ref[...] = jnp.zeros_like(carry_ref)

    mix = (jnp.dot(yr_ref[...], wo_ref[0:half, :], preferred_element_type=F32)
           + jnp.dot(yp_ref[...], wo_ref[half:, :], preferred_element_type=F32))
    h = _layer_norm(DEEPNORM_ALPHA * x_ref[...] + mix, lw_ref[...], lb_ref[...], LN_EPS)
    h_ref[...] = h
    half_d = h.shape[1] // 2
    lo = pltpu.bitcast(h[:, :half_d].astype(BF16).astype(F32), jnp.uint32) >> 16
    hi = pltpu.bitcast(h[:, half_d:].astype(BF16).astype(F32), jnp.uint32)
    words = lo | hi
    chunks = jnp.stack([words[:, c * LANES:(c + 1) * LANES] for c in range(hp_ref.shape[1])], axis=0)
    hp_ref[...] = pltpu.einshape("ctl->tcl", chunks)

    h_hi = h.astype(BF16)
    h_lo = (h - h_hi.astype(F32)).astype(BF16)
    logits = (jnp.dot(h_hi, rwh_ref[...], preferred_element_type=F32)
              + jnp.dot(h_lo, rwh_ref[...], preferred_element_type=F32)
              + jnp.dot(h_hi, rwl_ref[...], preferred_element_type=F32)) + rb_ref[...]
    lane = lax.broadcasted_iota(jnp.int32, (tm, LANES), 1).astype(F32)
    idxs, vals = [], []
    left = logits
    for _ in range(TOP_K):
        m = jnp.max(left, axis=-1, keepdims=True)
        idx = jnp.min(jnp.where(left == m, lane, float(LANES)), axis=-1, keepdims=True)
        idxs.append(idx)
        vals.append(m)
        left = jnp.where(lane == idx, -jnp.inf, left)
    exps = [jnp.exp(v - vals[0]) for v in vals]
    denom = exps[0] + exps[1] + exps[2] + exps[3]

    onehot = jnp.zeros((tm, LANES), F32)
    for idx in idxs:
        onehot = onehot + (lane == idx).astype(F32)
    tri = (lax.broadcasted_iota(jnp.int32, (tm, tm), 0)
           > lax.broadcasted_iota(jnp.int32, (tm, tm), 1)).astype(BF16)
    before = jnp.dot(tri, onehot.astype(BF16), preferred_element_type=F32) + carry_ref[...]
    carry_ref[...] = carry_ref[...] + jnp.sum(onehot, axis=0, keepdims=True)
    cnt_ref[...] = carry_ref[...]

    ri = jnp.zeros((tm, LANES), F32)
    gt = jnp.zeros((tm, LANES), F32)
    for k in range(TOP_K):
        rank = jnp.sum(jnp.where(lane == idxs[k], before, 0.0), axis=-1, keepdims=True)
        ri = jnp.where(lane == float(k), idxs[k], ri)
        ri = jnp.where(lane == float(TOP_K + k), rank, ri)
        gt = jnp.where(lane == float(k), exps[k] / denom, gt)
    ri_ref[...] = ri.astype(jnp.int32)
    g_ref[...] = gt


def _mix_ln_router(y_rwkv, y_pool, x2d, w_out_b, ln_w, ln_b, router_w_p, router_b_p, tm=512):
    n, d = x2d.shape
    rw_hi = router_w_p.astype(BF16)
    rw_lo = (router_w_p - rw_hi.astype(F32)).astype(BF16)
    half = y_rwkv.shape[1]
    const = lambda shape: pl.BlockSpec(shape, lambda i: (0, 0))
    return pl.pallas_call(
        _mix_kernel,
        grid=(n // tm,),
        in_specs=[pl.BlockSpec((tm, half), lambda i: (i, 0)),
                  pl.BlockSpec((tm, y_pool.shape[1]), lambda i: (i, 0)),
                  pl.BlockSpec((tm, d), lambda i: (i, 0)),
                  const(w_out_b.shape), const((1, d)), const((1, d)),
                  const(router_w_p.shape), const(router_w_p.shape), const((1, LANES))],
        out_specs=[pl.BlockSpec((tm, d), lambda i: (i, 0)),
                   pl.BlockSpec((tm, d // 2 // LANES, LANES), lambda i: (i, 0, 0)),
                   pl.BlockSpec((tm, LANES), lambda i: (i, 0)),
                   pl.BlockSpec((tm, LANES), lambda i: (i, 0)),
                   const((1, LANES))],
        out_shape=[jax.ShapeDtypeStruct((n, d), F32),
                   jax.ShapeDtypeStruct((n, d // 2 // LANES, LANES), jnp.uint32),
                   jax.ShapeDtypeStruct((n, LANES), jnp.int32),
                   jax.ShapeDtypeStruct((n, LANES), F32),
                   jax.ShapeDtypeStruct((1, LANES), F32)],
        scratch_shapes=[pltpu.VMEM((1, LANES), F32)],
        compiler_params=_cparams(("arbitrary",)),
        name="mix_ln_router",
    )(y_rwkv, y_pool, x2d, w_out_b, ln_w, ln_b, rw_hi, rw_lo, router_b_p)


MOE_BM = 512
GATHER_UNROLL = 8


MXU_DIM = 256
OUT_CHUNK = 512


def _split_w1_kernel(w_ref, g_ref, l_ref):
    half = MXU_DIM // 2
    src = lax.broadcasted_iota(jnp.int32, (MXU_DIM, MXU_DIM), 0)
    dst = lax.broadcasted_iota(jnp.int32, (MXU_DIM, MXU_DIM), 1)
    perm = (src == jnp.where(dst < half, 2 * dst, 2 * (dst - half) + 1)).astype(BF16)
    for c in range(w_ref.shape[2] // MXU_DIM):
        w = w_ref[0, :, c * MXU_DIM:(c + 1) * MXU_DIM].astype(BF16)
        o = jnp.dot(w, perm, preferred_element_type=F32)
        g_ref[0, :, c * half:(c + 1) * half] = o[:, :half].astype(BF16)
        l_ref[0, :, c * half:(c + 1) * half] = o[:, half:].astype(BF16)


def _split_w1(w1, tr=512, tc=2048):
    n_exp, d, f2 = w1.shape
    out = jax.ShapeDtypeStruct((n_exp, d, f2 // 2), BF16)
    return pl.pallas_call(
        _split_w1_kernel,
        grid=(n_exp, d // tr, f2 // tc),
        in_specs=[pl.BlockSpec((1, tr, tc), lambda e, i, j: (e, i, j))],
        out_specs=[pl.BlockSpec((1, tr, tc // 2), lambda e, i, j: (e, i, j)),
                   pl.BlockSpec((1, tr, tc // 2), lambda e, i, j: (e, i, j))],
        out_shape=[out, out],
        compiler_params=_cparams(("arbitrary", "arbitrary", "arbitrary")),
        name="split_w1",
    )(w1)


def _cast_kernel(w_ref, o_ref):
    o_ref[...] = w_ref[...].astype(o_ref.dtype)


def _cast_bf16(w, tr=512):
    n_exp, r, c = w.shape
    return pl.pallas_call(
        _cast_kernel,
        grid=(n_exp, r // tr),
        in_specs=[pl.BlockSpec((1, tr, c), lambda e, i: (e, i, 0))],
        out_specs=pl.BlockSpec((1, tr, c), lambda e, i: (e, i, 0)),
        out_shape=jax.ShapeDtypeStruct(w.shape, BF16),
        compiler_params=_cparams(("arbitrary", "arbitrary")),
        name="cast_w2",
    )(w)


def _scatter_kernel(pos_ref, hp_ref, zero_hbm, x_hbm, sem):
    del zero_hbm
    i = pl.program_id(0)
    tm = hp_ref.shape[0]

    def group(gi, carry):
        n0 = pl.multiple_of(gi * GATHER_UNROLL, GATHER_UNROLL)
        for j in range(GATHER_UNROLL):
            for k in range(TOP_K):
                p = pos_ref[(i * tm + n0 + j) * TOP_K + k]
                pltpu.make_async_copy(hp_ref.at[n0 + j], x_hbm.at[p], sem).start()
        return carry

    lax.fori_loop(0, tm // GATHER_UNROLL, group, 0)
    for k in range(TOP_K):
        pltpu.make_async_copy(hp_ref, x_hbm.at[pl.ds(0, tm)], sem).wait()


def _dispatch(h_packed, pos_flat, n_slots, tm=512):
    n = h_packed.shape[0]
    tile = h_packed.shape[1:]
    return pl.pallas_call(
        _scatter_kernel,
        grid_spec=pltpu.PrefetchScalarGridSpec(
            num_scalar_prefetch=1,
            grid=(n // tm,),
            in_specs=[pl.BlockSpec((tm,) + tile, lambda i, pos: (i, 0, 0)),
                      pl.BlockSpec(memory_space=pl.ANY)],
            out_specs=pl.BlockSpec(memory_space=pl.ANY),
            scratch_shapes=[pltpu.SemaphoreType.DMA(())]),
        out_shape=jax.ShapeDtypeStruct((n_slots,) + tile, h_packed.dtype),
        input_output_aliases={2: 0},
        compiler_params=_cparams(("arbitrary",)),
        name="moe_dispatch",
    )(pos_flat, h_packed, jnp.zeros((n_slots,) + tile, h_packed.dtype))


ROW_GROUP = 128


def _ffn_hidden(words, w1g_ref, w1l_ref, b1g_ref, b1l_ref):
    packed = pltpu.einshape("tcl->ctl", words)
    chunks = [packed[c] for c in range(words.shape[1])]
    x = jnp.concatenate([pltpu.bitcast(w << 16, F32).astype(BF16) for w in chunks]
                        + [pltpu.bitcast(w & jnp.uint32(0xFFFF0000), F32).astype(BF16) for w in chunks], axis=1)
    hg = jnp.dot(x, w1g_ref[0], preferred_element_type=F32) + b1g_ref[0]
    hl = jnp.dot(x, w1l_ref[0], preferred_element_type=F32) + b1l_ref[0]
    x_glu = jnp.minimum(hg, SWIGLU_LIMIT)
    x_lin = jnp.clip(hl, -SWIGLU_LIMIT, SWIGLU_LIMIT)
    return (x_glu * jax.nn.sigmoid(SWIGLU_ALPHA * x_glu) * (x_lin + 1.0)).astype(BF16)


def _ffn_kernel(be_ref, nu_ref, bv_ref, x_ref, w1g_ref, w1l_ref, b1g_ref, b1l_ref, w2_ref, b2_ref, o_ref):
    m = pl.program_id(0)
    f = pl.program_id(1)
    bm = o_ref.shape[0]
    valid = bv_ref[m]

    @pl.when(f == 0)
    def _():
        o_ref[...] = jnp.broadcast_to(b2_ref[0], o_ref.shape)

    whole = valid > bm - ROW_GROUP

    @pl.when(whole)
    def _():
        act = _ffn_hidden(x_ref[...], w1g_ref, w1l_ref, b1g_ref, b1l_ref)
        for c in range(0, o_ref.shape[1], OUT_CHUNK):
            o_ref[:, c:c + OUT_CHUNK] += jnp.dot(act, w2_ref[0, :, c:c + OUT_CHUNK], preferred_element_type=F32)

    @pl.when((valid > 0) & jnp.logical_not(whole))
    def _():
        def group(g, carry):
            rows = pl.ds(pl.multiple_of(g * ROW_GROUP, ROW_GROUP), ROW_GROUP)
            act = _ffn_hidden(x_ref[rows], w1g_ref, w1l_ref, b1g_ref, b1l_ref)
            for c in range(0, o_ref.shape[1], OUT_CHUNK):
                o_ref[rows, c:c + OUT_CHUNK] += jnp.dot(act, w2_ref[0, :, c:c + OUT_CHUNK],
                                                        preferred_element_type=F32)
            return carry

        lax.fori_loop(0, (valid + ROW_GROUP - 1) // ROW_GROUP, group, 0)


def _experts(x_slots, block_expert, n_used, block_valid, w1g, w1l, b1g, b1l, w2, b2, bm=MOE_BM, tf=1024):
    d = w1g.shape[1]
    n_slots = x_slots.shape[0]
    n_blocks = n_slots // bm
    ff = w1g.shape[2]
    n_f = ff // tf

    def m_eff(m, nu):
        return jnp.minimum(m, jnp.maximum(nu[0] - 1, 0))

    def f_eff(m, f, nu):
        return jnp.where(m < nu[0], f, n_f - 1)

    return pl.pallas_call(
        _ffn_kernel,
        grid_spec=pltpu.PrefetchScalarGridSpec(
            num_scalar_prefetch=3,
            grid=(n_blocks, n_f),
            in_specs=[
                pl.BlockSpec((bm,) + x_slots.shape[1:], lambda m, f, be, nu, bv: (m_eff(m, nu), 0, 0)),
                pl.BlockSpec((1, d, tf), lambda m, f, be, nu, bv: (be[m], 0, f_eff(m, f, nu))),
                pl.BlockSpec((1, d, tf), lambda m, f, be, nu, bv: (be[m], 0, f_eff(m, f, nu))),
                pl.BlockSpec((1, 1, tf), lambda m, f, be, nu, bv: (be[m], 0, f_eff(m, f, nu))),
                pl.BlockSpec((1, 1, tf), lambda m, f, be, nu, bv: (be[m], 0, f_eff(m, f, nu))),
                pl.BlockSpec((1, tf, d), lambda m, f, be, nu, bv: (be[m], f_eff(m, f, nu), 0)),
                pl.BlockSpec((1, 1, d), lambda m, f, be, nu, bv: (be[m], 0, 0)),
            ],
            out_specs=pl.BlockSpec((bm, d), lambda m, f, be, nu, bv: (m, 0))),
        out_shape=jax.ShapeDtypeStruct((n_slots, d), F32),
        compiler_params=_cparams(("arbitrary", "arbitrary")),
        name="moe_experts",
    )(block_expert, n_used, block_valid, x_slots, w1g, w1l, b1g, b1l, w2, b2)


def _combine_kernel(pos_ref, h_ref, g_ref, lw_ref, lb_ref, y_hbm, o_ref, buf, sem):
    i = pl.program_id(0)
    tm = h_ref.shape[0]

    def issue_step(step, slot):
        def group(gi, carry):
            n0 = pl.multiple_of(gi * GATHER_UNROLL, GATHER_UNROLL)
            for j in range(GATHER_UNROLL):
                for k in range(TOP_K):
                    p = pos_ref[(step * tm + n0 + j) * TOP_K + k]
                    pltpu.make_async_copy(y_hbm.at[pl.ds(p, 1), :], buf.at[slot, pl.ds(k * tm + n0 + j, 1), :],
                                          sem.at[slot]).start()
            return carry

        lax.fori_loop(0, tm // GATHER_UNROLL, group, 0)

    @pl.when(i == 0)
    def _():
        issue_step(0, 0)

    @pl.when(i + 1 < pl.num_programs(0))
    def _():
        issue_step(i + 1, (i + 1) % 2)

    slot = i % 2
    pltpu.make_async_copy(y_hbm.at[pl.ds(0, TOP_K * tm), :], buf.at[slot], sem.at[slot]).wait()
    g = g_ref[...]
    ffn = g[:, 0:1] * buf[slot, 0:tm, :]
    for k in range(1, TOP_K):
        ffn = ffn + g[:, k:k + 1] * buf[slot, k * tm:(k + 1) * tm, :]
    o_ref[...] = _layer_norm(DEEPNORM_ALPHA * h_ref[...] + ffn, lw_ref[...], lb_ref[...], LN_EPS)


def _combine(pos_flat, h, gates, ln_w, ln_b, y_slots, tm=256):
    n, d = h.shape
    return pl.pallas_call(
        _combine_kernel,
        grid_spec=pltpu.PrefetchScalarGridSpec(
            num_scalar_prefetch=1,
            grid=(n // tm,),
            in_specs=[pl.BlockSpec((tm, d), lambda i, pos: (i, 0)),
                      pl.BlockSpec((tm, LANES), lambda i, pos: (i, 0)),
                      pl.BlockSpec((1, d), lambda i, pos: (0, 0)),
                      pl.BlockSpec((1, d), lambda i, pos: (0, 0)),
                      pl.BlockSpec(memory_space=pl.ANY)],
            out_specs=pl.BlockSpec((tm, d), lambda i, pos: (i, 0)),
            scratch_shapes=[pltpu.VMEM((2, TOP_K * tm, d), F32), pltpu.SemaphoreType.DMA((2,))]),
        out_shape=jax.ShapeDtypeStruct((n, d), F32),
        compiler_params=_cparams(("arbitrary",)),
        name="moe_combine_ln",
    )(pos_flat, h, gates, ln_w, ln_b, y_slots)


def _moe(h, h_packed, route_i, gates, counts_f, w1, b1, w2, b2, ln_w, ln_b):
    n_tok, d = h.shape
    n_exp = w1.shape[0]
    n_asg = n_tok * TOP_K
    n_blocks = -(-n_asg // MOE_BM) + n_exp

    counts = counts_f[0, :n_exp].astype(jnp.int32)
    padded = (counts + MOE_BM - 1) // MOE_BM * MOE_BM
    pend = jnp.cumsum(padded)
    pstart = pend - padded
    pos = pstart[route_i[:, 0:TOP_K]] + route_i[:, TOP_K:2 * TOP_K]
    pos_flat = pos.reshape(n_asg)
    n_used = (pend[-1] // MOE_BM).astype(jnp.int32).reshape(1)
    blk = jnp.minimum(jnp.arange(n_blocks, dtype=jnp.int32), n_used[0] - 1)
    block_expert = jnp.minimum(jnp.sum(pend[None, :] <= (blk * MOE_BM)[:, None], axis=1), n_exp - 1).astype(jnp.int32)
    block_valid = jnp.clip((pstart + counts)[block_expert] - jnp.arange(n_blocks, dtype=jnp.int32) * MOE_BM,
                           0, MOE_BM).astype(jnp.int32)

    w1g, w1l = _split_w1(w1)
    b1g = b1[:, None, 0::2]
    b1l = b1[:, None, 1::2]
    w2b = _cast_bf16(w2)

    x_slots = _dispatch(h_packed, pos_flat, n_blocks * MOE_BM)
    y_slots = _experts(x_slots, block_expert, n_used, block_valid, w1g, w1l, b1g, b1l, w2b, b2[:, None, :])
    return _combine(pos_flat, h, gates, ln_w, ln_b, y_slots)


def _pad_to(a, axis, size):
    pad = [(0, 0)] * a.ndim
    pad[axis] = (0, size - a.shape[axis])
    return jnp.pad(a, pad)


def kernel(x, w_in, mu_shift, w0, w2_decay, a0, a2_iclr, g2_gate, k_k, k_a, r_k, lnx_w, lnx_b, w_pool, pool_scale,
           w_out, ln1_w, ln1_b, router_w, router_b, w1_exp, b1_exp, w2_exp, b2_exp, ln2_w, ln2_b):
    batch, seq, d = x.shape
    assert w_in.shape[0] == 1, "one layer"
    x2d = x.reshape(batch * seq, d)
    c_xw = RKV_COLS
    c_xa = c_xw + DECAY_LORA
    c_xg = c_xa + ICLR_LORA
    c_pool = c_xg + GATE_LORA

    def lora_layout(a):
        return jnp.concatenate([_pad_to(a[:, c_xw:c_xa], 1, LORA_XA - LORA_XW),
                                _pad_to(a[:, c_xa:c_xg], 1, LORA_XG - LORA_XA),
                                _pad_to(a[:, c_xg:c_pool], 1, LORA_PAD - LORA_XG)], axis=1)

    wi = w_in[0]
    w_p = jnp.concatenate([wi[:, :RKV_COLS], wi[:, c_pool:], lora_layout(wi)], axis=1).astype(BF16)
    mu = mu_shift[0][None, :]
    row = lambda a: a.reshape(1, -1)

    proj = _in_proj(x2d, w_p)
    y_rwkv = _rwkv_group(
        proj, batch, seq, mu[:, :RKV_COLS], lora_layout(mu), row(w0[0]),
        _pad_to(w2_decay[0], 0, LANES).astype(BF16), row(a0[0]), _pad_to(a2_iclr[0], 0, LANES).astype(BF16),
        _pad_to(g2_gate[0], 0, LORA_PAD - LORA_XG).astype(BF16), row(k_k[0]), row(k_a[0]), row(r_k[0]),
        row(lnx_w[0]), row(lnx_b[0]))
    y_pool = _pool_group(proj, batch, seq, w_pool[0].astype(BF16), row(pool_scale[0]))

    n_exp = router_w.shape[2]
    router_w_p = _pad_to(router_w[0], 1, LANES)
    router_b_p = jnp.concatenate([router_b[0], jnp.full((LANES - n_exp,), -1e30, F32)])[None, :]
    h1, h1_packed, route_i, gates, counts = _mix_ln_router(y_rwkv, y_pool, x2d, w_out[0].astype(BF16), row(ln1_w[0]),
                                                row(ln1_b[0]), router_w_p, router_b_p)
    out = _moe(h1, h1_packed, route_i, gates, counts, w1_exp[0], b1_exp[0], w2_exp[0], b2_exp[0], row(ln2_w[0]), row(ln2_b[0]))
    return out.reshape(batch, seq, d)
```

```python
import jax
import jax.numpy as jnp
from jax import lax
from jax.experimental import pallas as pl
from jax.experimental.pallas import tpu as pltpu

F32 = jnp.float32
BF16 = jnp.bfloat16

RWKV_WIDTH = 1024
HEAD = 64
DECAY_LORA = 64
ICLR_LORA = 64
GATE_LORA = 160
POOL_WINDOWS = (2, 4, 8, 16)
POOL_GROUP_WIDTH = 256
TOP_K = 4
SWIGLU_LIMIT = 7.0
SWIGLU_ALPHA = 1.702
LN_EPS = 1e-5
GN_EPS = 64e-5
DEEPNORM_ALPHA = 2.0 ** 0.25

LANES = 128
LORA_PAD = 512
LORA_XW, LORA_XA, LORA_XG = 0, 128, 256
RKV_COLS = 3 * RWKV_WIDTH
POOL_WIDTH = len(POOL_WINDOWS) * POOL_GROUP_WIDTH
POOL_COL0 = RKV_COLS
LORA_COL0 = POOL_COL0 + POOL_WIDTH
PROJ_COLS = LORA_COL0 + LORA_PAD
CHUNK = 64
SUB = 16
CHUNKS_PER_ITER = 2
VMEM_LIMIT = 56 * 1024 * 1024


def _cparams(sem):
    return pltpu.CompilerParams(dimension_semantics=sem, vmem_limit_bytes=VMEM_LIMIT)


def _bdot(a, b):
    return jnp.dot(a.astype(BF16), b.astype(BF16), preferred_element_type=F32)


def _bdot_nt(a, b):
    return lax.dot_general(a.astype(BF16), b.astype(BF16), (((1,), (1,)), ((), ())),
                           preferred_element_type=F32)


def _bdot_tn(a, b):
    return lax.dot_general(a.astype(BF16), b.astype(BF16), (((0,), (0,)), ((), ())),
                           preferred_element_type=F32)


def _in_proj_kernel(x_ref, w_ref, o_ref, xb_ref):
    @pl.when(pl.program_id(1) == 0)
    def _():
        xb_ref[...] = x_ref[...].astype(BF16)

    o_ref[...] = jnp.dot(xb_ref[...], w_ref[...], preferred_element_type=F32)


def _in_proj(x2d, w_p, tm=1024, tn=1536):
    n, d = x2d.shape
    cols = w_p.shape[1]
    return pl.pallas_call(
        _in_proj_kernel,
        grid=(n // tm, cols // tn),
        in_specs=[pl.BlockSpec((tm, d), lambda i, j: (i, 0)),
                  pl.BlockSpec((d, tn), lambda i, j: (0, j))],
        out_specs=pl.BlockSpec((tm, tn), lambda i, j: (i, j)),
        out_shape=jax.ShapeDtypeStruct((n, cols), F32),
        scratch_shapes=[pltpu.VMEM((tm, d), BF16)],
        compiler_params=_cparams(("arbitrary", "arbitrary")),
        name="in_proj",
    )(x2d, w_p)


def _shifted(x_ref, prev_ref, mu_ref, first_row):
    x = x_ref[...]
    xp = jnp.where(first_row, prev_ref[...], pltpu.roll(x, 1, axis=0))
    prev_ref[...] = x_ref[x.shape[0] - 1:x.shape[0], :]
    return x + (xp - x) * mu_ref[...]


def _rwkv_kernel(r_ref, k_ref, v_ref, lo_ref, mur_ref, muk_ref, muv_ref, mul_ref,
                 w0_ref, w2d_ref, a0_ref, a2_ref, g2_ref, kk_ref, ka_ref, rk_ref, lnw_ref, lnb_ref,
                 o_ref,
                 pr_ref, pk_ref, pv_ref, plo_ref, s_ref,
                 rs_ref, lw_ref, ks_ref, vs_ref, kks_ref, as_ref, gs_ref):
    tb, width = r_ref.shape
    n_pairs = width // LANES
    c2 = 2 * CHUNK

    @pl.when(pl.program_id(2) == 0)
    def _():
        pr_ref[...] = jnp.zeros_like(pr_ref)
        pk_ref[...] = jnp.zeros_like(pk_ref)
        pv_ref[...] = jnp.zeros_like(pv_ref)
        plo_ref[...] = jnp.zeros_like(plo_ref)
        s_ref[...] = jnp.zeros_like(s_ref)

    first_row = lax.broadcasted_iota(jnp.int32, (tb, 1), 0) == 0
    r = _shifted(r_ref, pr_ref, mur_ref, first_row)
    k = _shifted(k_ref, pk_ref, muk_ref, first_row)
    v = _shifted(v_ref, pv_ref, muv_ref, first_row)
    lo = _shifted(lo_ref, plo_ref, mul_ref, first_row)

    xw = jnp.tanh(lo[:, LORA_XW:LORA_XW + LANES])
    xa = lo[:, LORA_XA:LORA_XA + LANES]
    xg = jax.nn.sigmoid(lo[:, LORA_XG:LORA_PAD])
    z = w0_ref[...] + _bdot(xw, w2d_ref[...])
    w_logit = -(jnp.maximum(-z, 0.0) + jnp.log(1.0 + jnp.exp(-jnp.abs(z)))) - 0.5
    a_ic = jax.nn.sigmoid(a0_ref[...] + _bdot(xa, a2_ref[...]))
    rs_ref[...] = r
    lw_ref[...] = -jnp.exp(w_logit)
    ks_ref[...] = k * (1.0 + (a_ic - 1.0) * ka_ref[...])
    vs_ref[...] = v
    kks_ref[...] = k * kk_ref[...]
    as_ref[...] = a_ic
    gs_ref[...] = _bdot(xg, g2_ref[...])

    lane = lax.broadcasted_iota(jnp.int32, (CHUNK, LANES), 1)
    head0 = lane < HEAD
    row = lax.broadcasted_iota(jnp.int32, (c2, c2), 0)
    col = lax.broadcasted_iota(jnp.int32, (c2, c2), 1)
    strict = row > col
    incl = row >= col
    same_sub = (row // SUB) == (col // SUB)
    eye = (row == col).astype(F32)
    tri = (lax.broadcasted_iota(jnp.int32, (CHUNK, CHUNK), 0)
           >= lax.broadcasted_iota(jnp.int32, (CHUNK, CHUNK), 1)).astype(F32)
    def head_sum(x):
        s0 = jnp.sum(jnp.where(head0, x, 0.0), axis=-1, keepdims=True)
        s1 = jnp.sum(jnp.where(head0, 0.0, x), axis=-1, keepdims=True)
        return jnp.where(head0, s0, s1)

    def stack(x):
        return jnp.concatenate([jnp.where(head0, x, 0.0), jnp.where(head0, 0.0, x)], axis=0)

    def operands(rc, lw, lwi, kc, vc, kk, ac):
        kkn = kk * lax.rsqrt(jnp.maximum(head_sum(kk * kk), 1e-24))
        b_ = kkn * ac
        tot = lwi[CHUNK - 1:CHUNK, :]
        e_inv = jnp.exp(-lwi)
        e_out = jnp.exp(tot - lwi)
        at = stack(-kkn * jnp.exp(lwi - lw)).astype(BF16)
        rt = stack(rc * jnp.exp(lwi)).astype(BF16)
        bk = jnp.concatenate([stack(b_ * e_inv), stack(kc * e_inv)], axis=0).astype(BF16)
        bkh = jnp.concatenate([stack(b_ * e_out), stack(kc * e_out)], axis=0).astype(BF16)
        return at, rt, bk, bkh, stack(vc).astype(BF16), jnp.exp(tot)

    def finish(y_st, rc, kc, vc, gate, rk, lnw, lnb):
        y = y_st[:CHUNK, :] + y_st[CHUNK:, :]
        mu = head_sum(y) * (1.0 / HEAD)
        yc = y - mu
        var = head_sum(yc * yc) * (1.0 / HEAD)
        yn = yc * lax.rsqrt(var + GN_EPS) * lnw + lnb
        yn = yn + head_sum(rc * kc * rk) * vc
        return (yn * gate).astype(o_ref.dtype)

    def chunks(it, carry):
        sls = [pl.ds(pl.multiple_of((it * CHUNKS_PER_ITER + u) * CHUNK, CHUNK), CHUNK)
               for u in range(CHUNKS_PER_ITER)]
        lws = [lw_ref[sl, :] for sl in sls]
        lwis = [jnp.dot(tri, lw, preferred_element_type=F32, precision=lax.Precision.HIGHEST) for lw in lws]
        pairs = range(n_pairs)
        ls = [slice(p * LANES, (p + 1) * LANES) for p in pairs]
        items = [(u, p) for u in range(CHUNKS_PER_ITER) for p in pairs]
        n = range(len(items))
        ops = [operands(rs_ref[sls[u], ls[p]], lws[u][:, ls[p]], lwis[u][:, ls[p]], ks_ref[sls[u], ls[p]],
                        vs_ref[sls[u], ls[p]], kks_ref[sls[u], ls[p]], as_ref[sls[u], ls[p]]) for u, p in items]
        at, rt, bk, bkh, vst, dec = (list(z) for z in zip(*ops))
        a_all = [_bdot_nt(jnp.concatenate([at[i], rt[i]], axis=0), bk[i]) for i in n]
        a_ab = [jnp.where(strict, a[:c2, :c2], 0.0) for a in a_all]
        a_ak = [jnp.where(strict, a[:c2, c2:], 0.0) for a in a_all]
        a_r = [jnp.where(jnp.concatenate([incl, incl], axis=1), a[c2:, :], 0.0).astype(BF16) for a in a_all]
        akv = [_bdot(a_ak[i], vst[i]).astype(BF16) for i in n]

        d1 = [jnp.where(same_sub, a, 0.0) for a in a_ab]
        l1 = [a_ab[i] - d1[i] for i in n]
        d2 = [_bdot(d, d) for d in d1]
        d4 = [_bdot(d, d) for d in d2]
        p12 = [_bdot(eye + d1[i], eye + d2[i]) for i in n]
        d8 = [_bdot(d, d) for d in d4]
        p48 = [_bdot(eye + d4[i], eye + d8[i]) for i in n]
        t_d = [_bdot(p12[i], p48[i]).astype(BF16) for i in n]
        m1 = [_bdot(t_d[i], l1[i]) for i in n]
        m2 = [_bdot(m, m) for m in m1]
        q = [_bdot(eye + m1[i], eye + m2[i]) for i in n]
        t_inv = [_bdot(q[i], t_d[i]) for i in n]
        tx = [_bdot(t_inv[i], jnp.concatenate([at[i], akv[i]], axis=1)) for i in n]

        for u in range(CHUNKS_PER_ITER):
            idx = [u * n_pairs + p for p in pairs]
            s_bd = [s_ref[p] for p in pairs]
            wr = [_bdot_nt(jnp.concatenate([tx[i][:, :LANES].astype(BF16), rt[i]], axis=0), s_bd[p])
                  for p, i in zip(pairs, idx)]
            uv = [jnp.concatenate([(wr[p][:c2, :] + tx[i][:, LANES:]).astype(BF16), vst[i]], axis=0)
                  for p, i in zip(pairs, idx)]
            y_st = [wr[p][c2:, :] + _bdot(a_r[i], uv[p]) for p, i in zip(pairs, idx)]
            for p, i in zip(pairs, idx):
                s_ref[p] = s_bd[p] * dec[i] + _bdot_tn(uv[p], bkh[i])
            for p, l in zip(pairs, ls):
                sl = sls[u]
                o_ref[sl, l] = finish(y_st[p], rs_ref[sl, l], ks_ref[sl, l], vs_ref[sl, l], gs_ref[sl, l],
                                      rk_ref[:, l], lnw_ref[:, l], lnb_ref[:, l])
        return carry

    lax.fori_loop(0, tb // (CHUNK * CHUNKS_PER_ITER), chunks, 0)


RWKV_PAIRS_PER_STEP = 8


def _rwkv_group(proj, batch, seq, mu_rkv, mu_lora, w0, w2d, a0, a2, g2, k_k, k_a, r_k, lnx_w, lnx_b, tb=512,
                pairs=RWKV_PAIRS_PER_STEP):
    width = pairs * LANES
    n_g = RWKV_WIDTH // width
    n_tb = seq // tb
    row = lambda b, h, t: b * n_tb + t
    vec = lambda off: pl.BlockSpec((1, width), lambda b, h, t: (0, off + h))
    lora_blk = LORA_COL0 // LORA_PAD
    in_specs = [
        pl.BlockSpec((tb, width), lambda b, h, t: (row(b, h, t), h)),
        pl.BlockSpec((tb, width), lambda b, h, t: (row(b, h, t), n_g + h)),
        pl.BlockSpec((tb, width), lambda b, h, t: (row(b, h, t), 2 * n_g + h)),
        pl.BlockSpec((tb, LORA_PAD), lambda b, h, t: (row(b, h, t), lora_blk)),
        vec(0), vec(n_g), vec(2 * n_g),
        pl.BlockSpec((1, LORA_PAD), lambda b, h, t: (0, 0)),
        vec(0),
        pl.BlockSpec((LANES, width), lambda b, h, t: (0, h)),
        vec(0),
        pl.BlockSpec((LANES, width), lambda b, h, t: (0, h)),
        pl.BlockSpec((LORA_PAD - LORA_XG, width), lambda b, h, t: (0, h)),
        vec(0), vec(0), vec(0), vec(0), vec(0),
    ]
    blk = lambda: pltpu.VMEM((tb, width), F32)
    return pl.pallas_call(
        _rwkv_kernel,
        grid=(batch, n_g, n_tb),
        in_specs=in_specs,
        out_specs=pl.BlockSpec((tb, width), lambda b, h, t: (row(b, h, t), h)),
        out_shape=jax.ShapeDtypeStruct((batch * seq, RWKV_WIDTH), BF16),
        scratch_shapes=[pltpu.VMEM((1, width), F32), pltpu.VMEM((1, width), F32), pltpu.VMEM((1, width), F32),
                        pltpu.VMEM((1, LORA_PAD), F32), pltpu.VMEM((pairs, 2 * CHUNK, LANES), F32),
                        blk(), blk(), blk(), blk(), blk(), blk(), blk()],
        compiler_params=_cparams(("arbitrary", "arbitrary", "arbitrary")),
        name="rwkv7_group",
    )(proj, proj, proj, proj, mu_rkv, mu_rkv, mu_rkv, mu_lora, w0, w2d, a0, a2, g2, k_k, k_a, r_k, lnx_w, lnx_b)


HALO = 16


def _pool_kernel(p_ref, w_ref, sc_ref, o_ref, ext_ref):
    tb = p_ref.shape[0]
    t = pl.program_id(1)

    @pl.when(t == 0)
    def _():
        ext_ref[0:HALO, :] = jnp.zeros((HALO, ext_ref.shape[1]), F32)

    @pl.when(t > 0)
    def _():
        ext_ref[0:HALO, :] = ext_ref[tb:tb + HALO, :]

    ext_ref[HALO:HALO + tb, :] = p_ref[...]
    t_idx = t * tb + lax.broadcasted_iota(jnp.int32, (tb, 1), 0)
    for gi, win in enumerate(POOL_WINDOWS):
        cs = slice(gi * POOL_GROUP_WIDTH, (gi + 1) * POOL_GROUP_WIDTH)
        acc = ext_ref[HALO:HALO + tb, cs]
        for d in range(1, win):
            acc = acc + ext_ref[HALO - d:HALO - d + tb, cs]
        count = jnp.minimum(t_idx + 1, win).astype(F32)
        pooled = acc / count - ext_ref[HALO:HALO + tb, cs]
        mixed = _bdot(pooled, w_ref[gi])
        o_ref[:, cs] = (mixed * sc_ref[:, cs]).astype(o_ref.dtype)


def _pool_group(proj, batch, seq, w_pool_b, pool_scale, tb=512):
    n_tb = seq // tb
    col_blk = POOL_COL0 // POOL_WIDTH
    return pl.pallas_call(
        _pool_kernel,
        grid=(batch, n_tb),
        in_specs=[pl.BlockSpec((tb, POOL_WIDTH), lambda b, t: (b * n_tb + t, col_blk)),
                  pl.BlockSpec(w_pool_b.shape, lambda b, t: (0, 0, 0)),
                  pl.BlockSpec((1, POOL_WIDTH), lambda b, t: (0, 0))],
        out_specs=pl.BlockSpec((tb, POOL_WIDTH), lambda b, t: (b * n_tb + t, 0)),
        out_shape=jax.ShapeDtypeStruct((batch * seq, POOL_WIDTH), BF16),
        scratch_shapes=[pltpu.VMEM((tb + HALO, POOL_WIDTH), F32)],
        compiler_params=_cparams(("arbitrary", "arbitrary")),
        name="pool_group",
    )(proj, w_pool_b, pool_scale)


def _layer_norm(x, w, b, eps):
    mu = jnp.mean(x, axis=-1, keepdims=True)
    xc = x - mu
    var = jnp.mean(xc * xc, axis=-1, keepdims=True)
    return xc * lax.rsqrt(var + eps) * w + b


def _mix_kernel(yr_ref, yp_ref, x_ref, wo_ref, lw_ref, lb_ref, rwh_ref, rwl_ref, rb_ref,
                h_ref, hp_ref, ri_ref, g_ref, cnt_ref, carry_ref):
    tm = x_ref.shape[0]
    half = yr_ref.shape[1]

    @pl.when(pl.program_id(0) == 0)
    def _():
        carry_ref[...] = jnp.zeros_like(carry_ref)

    mix = (jnp.dot(yr_ref[...], wo_ref[0:half, :], preferred_element_type=F32)
           + jnp.dot(yp_ref[...], wo_ref[half:, :], preferred_element_type=F32))
    h = _layer_norm(DEEPNORM_ALPHA * x_ref[...] + mix, lw_ref[...], lb_ref[...], LN_EPS)
    h_ref[...] = h
    half_d = h.shape[1] // 2
    lo = pltpu.bitcast(h[:, :half_d].astype(BF16).astype(F32), jnp.uint32) >> 16
    hi = pltpu.bitcast(h[:, half_d:].astype(BF16).astype(F32), jnp.uint32)
    words = lo | hi
    chunks = jnp.stack([words[:, c * LANES:(c + 1) * LANES] for c in range(hp_ref.shape[1])], axis=0)
    hp_ref[...] = pltpu.einshape("ctl->tcl", chunks)

    h_hi = h.astype(BF16)
    h_lo = (h - h_hi.astype(F32)).astype(BF16)
    logits = (jnp.dot(h_hi, rwh_ref[...], preferred_element_type=F32)
              + jnp.dot(h_lo, rwh_ref[...], preferred_element_type=F32)
              + jnp.dot(h_hi, rwl_ref[...], preferred_element_type=F32)) + rb_ref[...]
    lane = lax.broadcasted_iota(jnp.int32, (tm, LANES), 1).astype(F32)
    idxs, vals = [], []
    left = logits
    for _ in range(TOP_K):
        m = jnp.max(left, axis=-1, keepdims=True)
        idx = jnp.min(jnp.where(left == m, lane, float(LANES)), axis=-1, keepdims=True)
        idxs.append(idx)
        vals.append(m)
        left = jnp.where(lane == idx, -jnp.inf, left)
    exps = [jnp.exp(v - vals[0]) for v in vals]
    denom = exps[0] + exps[1] + exps[2] + exps[3]

    onehot = jnp.zeros((tm, LANES), F32)
    for idx in idxs:
        onehot = onehot + (lane == idx).astype(F32)
    tri = (lax.broadcasted_iota(jnp.int32, (tm, tm), 0)
           > lax.broadcasted_iota(jnp.int32, (tm, tm), 1)).astype(BF16)
    before = jnp.dot(tri, onehot.astype(BF16), preferred_element_type=F32) + carry_ref[...]
    carry_ref[...] = carry_ref[...] + jnp.sum(onehot, axis=0, keepdims=True)
    cnt_ref[...] = carry_ref[...]

    ri = jnp.zeros((tm, LANES), F32)
    gt = jnp.zeros((tm, LANES), F32)
    for k in range(TOP_K):
        rank = jnp.sum(jnp.where(lane == idxs[k], before, 0.0), axis=-1, keepdims=True)
        ri = jnp.where(lane == float(k), idxs[k], ri)
        ri = jnp.where(lane == float(TOP_K + k), rank, ri)
        gt = jnp.where(lane == float(k), exps[k] / denom, gt)
    ri_ref[...] = ri.astype(jnp.int32)
    g_ref[...] = gt


def _mix_ln_router(y_rwkv, y_pool, x2d, w_out_b, ln_w, ln_b, router_w_p, router_b_p, tm=512):
    n, d = x2d.shape
    rw_hi = router_w_p.astype(BF16)
    rw_lo = (router_w_p - rw_hi.astype(F32)).astype(BF16)
    half = y_rwkv.shape[1]
    const = lambda shape: pl.BlockSpec(shape, lambda i: (0, 0))
    return pl.pallas_call(
        _mix_kernel,
        grid=(n // tm,),
        in_specs=[pl.BlockSpec((tm, half), lambda i: (i, 0)),
                  pl.BlockSpec((tm, y_pool.shape[1]), lambda i: (i, 0)),
                  pl.BlockSpec((tm, d), lambda i: (i, 0)),
                  const(w_out_b.shape), const((1, d)), const((1, d)),
                  const(router_w_p.shape), const(router_w_p.shape), const((1, LANES))],
        out_specs=[pl.BlockSpec((tm, d), lambda i: (i, 0)),
                   pl.BlockSpec((tm, d // 2 // LANES, LANES), lambda i: (i, 0, 0)),
                   pl.BlockSpec((tm, LANES), lambda i: (i, 0)),
                   pl.BlockSpec((tm, LANES), lambda i: (i, 0)),
                   const((1, LANES))],
        out_shape=[jax.ShapeDtypeStruct((n, d), F32),
                   jax.ShapeDtypeStruct((n, d // 2 // LANES, LANES), jnp.uint32),
                   jax.ShapeDtypeStruct((n, LANES), jnp.int32),
                   jax.ShapeDtypeStruct((n, LANES), F32),
                   jax.ShapeDtypeStruct((1, LANES), F32)],
        scratch_shapes=[pltpu.VMEM((1, LANES), F32)],
        compiler_params=_cparams(("arbitrary",)),
        name="mix_ln_router",
    )(y_rwkv, y_pool, x2d, w_out_b, ln_w, ln_b, rw_hi, rw_lo, router_b_p)


MOE_BM = 512
GATHER_UNROLL = 8


MXU_DIM = 256
OUT_CHUNK = 512


def _split_w1_kernel(w_ref, g_ref, l_ref):
    half = MXU_DIM // 2
    src = lax.broadcasted_iota(jnp.int32, (MXU_DIM, MXU_DIM), 0)
    dst = lax.broadcasted_iota(jnp.int32, (MXU_DIM, MXU_DIM), 1)
    perm = (src == jnp.where(dst < half, 2 * dst, 2 * (dst - half) + 1)).astype(BF16)
    for c in range(w_ref.shape[2] // MXU_DIM):
        w = w_ref[0, :, c * MXU_DIM:(c + 1) * MXU_DIM].astype(BF16)
        o = jnp.dot(w, perm, preferred_element_type=F32)
        g_ref[0, :, c * half:(c + 1) * half] = o[:, :half].astype(BF16)
        l_ref[0, :, c * half:(c + 1) * half] = o[:, half:].astype(BF16)


def _split_w1(w1, tr=512, tc=2048):
    n_exp, d, f2 = w1.shape
    out = jax.ShapeDtypeStruct((n_exp, d, f2 // 2), BF16)
    return pl.pallas_call(
        _split_w1_kernel,
        grid=(n_exp, d // tr, f2 // tc),
        in_specs=[pl.BlockSpec((1, tr, tc), lambda e, i, j: (e, i, j))],
        out_specs=[pl.BlockSpec((1, tr, tc // 2), lambda e, i, j: (e, i, j)),
                   pl.BlockSpec((1, tr, tc // 2), lambda e, i, j: (e, i, j))],
        out_shape=[out, out],
        compiler_params=_cparams(("arbitrary", "arbitrary", "arbitrary")),
        name="split_w1",
    )(w1)


def _cast_kernel(w_ref, o_ref):
    o_ref[...] = w_ref[...].astype(o_ref.dtype)


def _cast_bf16(w, tr=512):
    n_exp, r, c = w.shape
    return pl.pallas_call(
        _cast_kernel,
        grid=(n_exp, r // tr),
        in_specs=[pl.BlockSpec((1, tr, c), lambda e, i: (e, i, 0))],
        out_specs=pl.BlockSpec((1, tr, c), lambda e, i: (e, i, 0)),
        out_shape=jax.ShapeDtypeStruct(w.shape, BF16),
        compiler_params=_cparams(("arbitrary", "arbitrary")),
        name="cast_w2",
    )(w)


def _scatter_kernel(pos_ref, hp_ref, zero_hbm, x_hbm, sem):
    del zero_hbm
    i = pl.program_id(0)
    tm = hp_ref.shape[0]

    def group(gi, carry):
        n0 = pl.multiple_of(gi * GATHER_UNROLL, GATHER_UNROLL)
        for j in range(GATHER_UNROLL):
            for k in range(TOP_K):
                p = pos_ref[(i * tm + n0 + j) * TOP_K + k]
                pltpu.make_async_copy(hp_ref.at[n0 + j], x_hbm.at[p], sem).start()
        return carry

    lax.fori_loop(0, tm // GATHER_UNROLL, group, 0)
    for k in range(TOP_K):
        pltpu.make_async_copy(hp_ref, x_hbm.at[pl.ds(0, tm)], sem).wait()


def _dispatch(h_packed, pos_flat, n_slots, tm=512):
    n = h_packed.shape[0]
    tile = h_packed.shape[1:]
    return pl.pallas_call(
        _scatter_kernel,
        grid_spec=pltpu.PrefetchScalarGridSpec(
            num_scalar_prefetch=1,
            grid=(n // tm,),
            in_specs=[pl.BlockSpec((tm,) + tile, lambda i, pos: (i, 0, 0)),
                      pl.BlockSpec(memory_space=pl.ANY)],
            out_specs=pl.BlockSpec(memory_space=pl.ANY),
            scratch_shapes=[pltpu.SemaphoreType.DMA(())]),
        out_shape=jax.ShapeDtypeStruct((n_slots,) + tile, h_packed.dtype),
        input_output_aliases={2: 0},
        compiler_params=_cparams(("arbitrary",)),
        name="moe_dispatch",
    )(pos_flat, h_packed, jnp.zeros((n_slots,) + tile, h_packed.dtype))


ROW_GROUP = 128


def _ffn_hidden(words, w1g_ref, w1l_ref, b1g_ref, b1l_ref):
    packed = pltpu.einshape("tcl->ctl", words)
    chunks = [packed[c] for c in range(words.shape[1])]
    x = jnp.concatenate([pltpu.bitcast(w << 16, F32).astype(BF16) for w in chunks]
                        + [pltpu.bitcast(w & jnp.uint32(0xFFFF0000), F32).astype(BF16) for w in chunks], axis=1)
    hg = jnp.dot(x, w1g_ref[0], preferred_element_type=F32) + b1g_ref[0]
    hl = jnp.dot(x, w1l_ref[0], preferred_element_type=F32) + b1l_ref[0]
    x_glu = jnp.minimum(hg, SWIGLU_LIMIT)
    x_lin = jnp.clip(hl, -SWIGLU_LIMIT, SWIGLU_LIMIT)
    return (x_glu * jax.nn.sigmoid(SWIGLU_ALPHA * x_glu) * (x_lin + 1.0)).astype(BF16)


def _ffn_kernel(be_ref, nu_ref, bv_ref, x_ref, w1g_ref, w1l_ref, b1g_ref, b1l_ref, w2_ref, b2_ref, o_ref):
    m = pl.program_id(0)
    f = pl.program_id(1)
    bm = o_ref.shape[0]
    valid = bv_ref[m]

    @pl.when(f == 0)
    def _():
        o_ref[...] = jnp.broadcast_to(b2_ref[0], o_ref.shape)

    whole = valid > bm - ROW_GROUP

    @pl.when(whole)
    def _():
        act = _ffn_hidden(x_ref[...], w1g_ref, w1l_ref, b1g_ref, b1l_ref)
        for c in range(0, o_ref.shape[1], OUT_CHUNK):
            o_ref[:, c:c + OUT_CHUNK] += jnp.dot(act, w2_ref[0, :, c:c + OUT_CHUNK], preferred_element_type=F32)

    @pl.when((valid > 0) & jnp.logical_not(whole))
    def _():
        def group(g, carry):
            rows = pl.ds(pl.multiple_of(g * ROW_GROUP, ROW_GROUP), ROW_GROUP)
            act = _ffn_hidden(x_ref[rows], w1g_ref, w1l_ref, b1g_ref, b1l_ref)
            for c in range(0, o_ref.shape[1], OUT_CHUNK):
                o_ref[rows, c:c + OUT_CHUNK] += jnp.dot(act, w2_ref[0, :, c:c + OUT_CHUNK],
                                                        preferred_element_type=F32)
            return carry

        lax.fori_loop(0, (valid + ROW_GROUP - 1) // ROW_GROUP, group, 0)


def _experts(x_slots, block_expert, n_used, block_valid, w1g, w1l, b1g, b1l, w2, b2, bm=MOE_BM, tf=1024):
    d = w1g.shape[1]
    n_slots = x_slots.shape[0]
    n_blocks = n_slots // bm
    ff = w1g.shape[2]
    n_f = ff // tf

    def m_eff(m, nu):
        return jnp.minimum(m, jnp.maximum(nu[0] - 1, 0))

    def f_eff(m, f, nu):
        return jnp.where(m < nu[0], f, n_f - 1)

    return pl.pallas_call(
        _ffn_kernel,
        grid_spec=pltpu.PrefetchScalarGridSpec(
            num_scalar_prefetch=3,
            grid=(n_blocks, n_f),
            in_specs=[
                pl.BlockSpec((bm,) + x_slots.shape[1:], lambda m, f, be, nu, bv: (m_eff(m, nu), 0, 0)),
                pl.BlockSpec((1, d, tf), lambda m, f, be, nu, bv: (be[m], 0, f_eff(m, f, nu))),
                pl.BlockSpec((1, d, tf), lambda m, f, be, nu, bv: (be[m], 0, f_eff(m, f, nu))),
                pl.BlockSpec((1, 1, tf), lambda m, f, be, nu, bv: (be[m], 0, f_eff(m, f, nu))),
                pl.BlockSpec((1, 1, tf), lambda m, f, be, nu, bv: (be[m], 0, f_eff(m, f, nu))),
                pl.BlockSpec((1, tf, d), lambda m, f, be, nu, bv: (be[m], f_eff(m, f, nu), 0)),
                pl.BlockSpec((1, 1, d), lambda m, f, be, nu, bv: (be[m], 0, 0)),
            ],
            out_specs=pl.BlockSpec((bm, d), lambda m, f, be, nu, bv: (m, 0))),
        out_shape=jax.ShapeDtypeStruct((n_slots, d), F32),
        compiler_params=_cparams(("arbitrary", "arbitrary")),
        name="moe_experts",
    )(block_expert, n_used, block_valid, x_slots, w1g, w1l, b1g, b1l, w2, b2)


def _combine_kernel(pos_ref, h_ref, g_ref, lw_ref, lb_ref, y_hbm, o_ref, buf, sem):
    i = pl.program_id(0)
    tm = h_ref.shape[0]

    def issue_step(step, slot):
        def group(gi, carry):
            n0 = pl.multiple_of(gi * GATHER_UNROLL, GATHER_UNROLL)
            for j in range(GATHER_UNROLL):
                for k in range(TOP_K):
                    p = pos_ref[(step * tm + n0 + j) * TOP_K + k]
                    pltpu.make_async_copy(y_hbm.at[pl.ds(p, 1), :], buf.at[slot, pl.ds(k * tm + n0 + j, 1), :],
                                          sem.at[slot]).start()
            return carry

        lax.fori_loop(0, tm // GATHER_UNROLL, group, 0)

    @pl.when(i == 0)
    def _():
        issue_step(0, 0)

    @pl.when(i + 1 < pl.num_programs(0))
    def _():
        issue_step(i + 1, (i + 1) % 2)

    slot = i % 2
    pltpu.make_async_copy(y_hbm.at[pl.ds(0, TOP_K * tm), :], buf.at[slot], sem.at[slot]).wait()
    g = g_ref[...]
    ffn = g[:, 0:1] * buf[slot, 0:tm, :]
    for k in range(1, TOP_K):
        ffn = ffn + g[:, k:k + 1] * buf[slot, k * tm:(k + 1) * tm, :]
    o_ref[...] = _layer_norm(DEEPNORM_ALPHA * h_ref[...] + ffn, lw_ref[...], lb_ref[...], LN_EPS)


def _combine(pos_flat, h, gates, ln_w, ln_b, y_slots, tm=256):
    n, d = h.shape
    return pl.pallas_call(
        _combine_kernel,
        grid_spec=pltpu.PrefetchScalarGridSpec(
            num_scalar_prefetch=1,
            grid=(n // tm,),
            in_specs=[pl.BlockSpec((tm, d), lambda i, pos: (i, 0)),
                      pl.BlockSpec((tm, LANES), lambda i, pos: (i, 0)),
                      pl.BlockSpec((1, d), lambda i, pos: (0, 0)),
                      pl.BlockSpec((1, d), lambda i, pos: (0, 0)),
                      pl.BlockSpec(memory_space=pl.ANY)],
            out_specs=pl.BlockSpec((tm, d), lambda i, pos: (i, 0)),
            scratch_shapes=[pltpu.VMEM((2, TOP_K * tm, d), F32), pltpu.SemaphoreType.DMA((2,))]),
        out_shape=jax.ShapeDtypeStruct((n, d), F32),
        compiler_params=_cparams(("arbitrary",)),
        name="moe_combine_ln",
    )(pos_flat, h, gates, ln_w, ln_b, y_slots)


def _moe(h, h_packed, route_i, gates, counts_f, w1, b1, w2, b2, ln_w, ln_b):
    n_tok, d = h.shape
    n_exp = w1.shape[0]
    n_asg = n_tok * TOP_K
    n_blocks = -(-n_asg // MOE_BM) + n_exp

    counts = counts_f[0, :n_exp].astype(jnp.int32)
    padded = (counts + MOE_BM - 1) // MOE_BM * MOE_BM
    pend = jnp.cumsum(padded)
    pstart = pend - padded
    pos = pstart[route_i[:, 0:TOP_K]] + route_i[:, TOP_K:2 * TOP_K]
    pos_flat = pos.reshape(n_asg)
    n_used = (pend[-1] // MOE_BM).astype(jnp.int32).reshape(1)
    blk = jnp.minimum(jnp.arange(n_blocks, dtype=jnp.int32), n_used[0] - 1)
    block_expert = jnp.minimum(jnp.sum(pend[None, :] <= (blk * MOE_BM)[:, None], axis=1), n_exp - 1).astype(jnp.int32)
    block_valid = jnp.clip((pstart + counts)[block_expert] - jnp.arange(n_blocks, dtype=jnp.int32) * MOE_BM,
                           0, MOE_BM).astype(jnp.int32)

    w1g, w1l = _split_w1(w1)
    b1g = b1[:, None, 0::2]
    b1l = b1[:, None, 1::2]
    w2b = _cast_bf16(w2)

    x_slots = _dispatch(h_packed, pos_flat, n_blocks * MOE_BM)
    y_slots = _experts(x_slots, block_expert, n_used, block_valid, w1g, w1l, b1g, b1l, w2b, b2[:, None, :])
    return _combine(pos_flat, h, gates, ln_w, ln_b, y_slots)


def _pad_to(a, axis, size):
    pad = [(0, 0)] * a.ndim
    pad[axis] = (0, size - a.shape[axis])
    return jnp.pad(a, pad)


def kernel(x, w_in, mu_shift, w0, w2_decay, a0, a2_iclr, g2_gate, k_k, k_a, r_k, lnx_w, lnx_b, w_pool, pool_scale,
           w_out, ln1_w, ln1_b, router_w, router_b, w1_exp, b1_exp, w2_exp, b2_exp, ln2_w, ln2_b):
    batch, seq, d = x.shape
    assert w_in.shape[0] == 1, "one layer"
    x2d = x.reshape(batch * seq, d)
    c_xw = RKV_COLS
    c_xa = c_xw + DECAY_LORA
    c_xg = c_xa + ICLR_LORA
    c_pool = c_xg + GATE_LORA

    def lora_layout(a):
        return jnp.concatenate([_pad_to(a[:, c_xw:c_xa], 1, LORA_XA - LORA_XW),
                                _pad_to(a[:, c_xa:c_xg], 1, LORA_XG - LORA_XA),
                                _pad_to(a[:, c_xg:c_pool], 1, LORA_PAD - LORA_XG)], axis=1)

    wi = w_in[0]
    w_p = jnp.concatenate([wi[:, :RKV_COLS], wi[:, c_pool:], lora_layout(wi)], axis=1).astype(BF16)
    mu = mu_shift[0][None, :]
    row = lambda a: a.reshape(1, -1)

    proj = _in_proj(x2d, w_p)
    y_rwkv = _rwkv_group(
        proj, batch, seq, mu[:, :RKV_COLS], lora_layout(mu), row(w0[0]),
        _pad_to(w2_decay[0], 0, LANES).astype(BF16), row(a0[0]), _pad_to(a2_iclr[0], 0, LANES).astype(BF16),
        _pad_to(g2_gate[0], 0, LORA_PAD - LORA_XG).astype(BF16), row(k_k[0]), row(k_a[0]), row(r_k[0]),
        row(lnx_w[0]), row(lnx_b[0]))
    y_pool = _pool_group(proj, batch, seq, w_pool[0].astype(BF16), row(pool_scale[0]))

    n_exp = router_w.shape[2]
    router_w_p = _pad_to(router_w[0], 1, LANES)
    router_b_p = jnp.concatenate([router_b[0], jnp.full((LANES - n_exp,), -1e30, F32)])[None, :]
    h1, h1_packed, route_i, gates, counts = _mix_ln_router(y_rwkv, y_pool, x2d, w_out[0].astype(BF16), row(ln1_w[0]),
                                                row(ln1_b[0]), router_w_p, router_b_p)
    out = _moe(h1, h1_packed, route_i, gates, counts, w1_exp[0], b1_exp[0], w2_exp[0], b2_exp[0], row(ln2_w[0]), row(ln2_b[0]))
    return out.reshape(batch, seq, d)
```

```python
import jax
import jax.numpy as jnp
from jax import lax
from jax.experimental import pallas as pl
from jax.experimental.pallas import tpu as pltpu

F32 = jnp.float32
BF16 = jnp.bfloat16

RWKV_WIDTH = 1024
HEAD = 64
DECAY_LORA = 64
ICLR_LORA = 64
GATE_LORA = 160
POOL_WINDOWS = (2, 4, 8, 16)
POOL_GROUP_WIDTH = 256
TOP_K = 4
SWIGLU_LIMIT = 7.0
SWIGLU_ALPHA = 1.702
LN_EPS = 1e-5
GN_EPS = 64e-5
DEEPNORM_ALPHA = 2.0 ** 0.25

LANES = 128
LORA_PAD = 512
LORA_XW, LORA_XA, LORA_XG = 0, 128, 256
RKV_COLS = 3 * RWKV_WIDTH
POOL_WIDTH = len(POOL_WINDOWS) * POOL_GROUP_WIDTH
POOL_COL0 = RKV_COLS
LORA_COL0 = POOL_COL0 + POOL_WIDTH
PROJ_COLS = LORA_COL0 + LORA_PAD
CHUNK = 64
SUB = 16
CHUNKS_PER_ITER = 2
VMEM_LIMIT = 56 * 1024 * 1024


def _cparams(sem):
    return pltpu.CompilerParams(dimension_semantics=sem, vmem_limit_bytes=VMEM_LIMIT)


def _bdot(a, b):
    return jnp.dot(a.astype(BF16), b.astype(BF16), preferred_element_type=F32)


def _bdot_nt(a, b):
    return lax.dot_general(a.astype(BF16), b.astype(BF16), (((1,), (1,)), ((), ())),
                           preferred_element_type=F32)


def _bdot_tn(a, b):
    return lax.dot_general(a.astype(BF16), b.astype(BF16), (((0,), (0,)), ((), ())),
                           preferred_element_type=F32)


def _in_proj_kernel(x_ref, w_ref, o_ref, xb_ref):
    @pl.when(pl.program_id(1) == 0)
    def _():
        xb_ref[...] = x_ref[...].astype(BF16)

    o_ref[...] = jnp.dot(xb_ref[...], w_ref[...], preferred_element_type=F32)


def _in_proj(x2d, w_p, tm=1024, tn=1536):
    n, d = x2d.shape
    cols = w_p.shape[1]
    return pl.pallas_call(
        _in_proj_kernel,
        grid=(n // tm, cols // tn),
        in_specs=[pl.BlockSpec((tm, d), lambda i, j: (i, 0)),
                  pl.BlockSpec((d, tn), lambda i, j: (0, j))],
        out_specs=pl.BlockSpec((tm, tn), lambda i, j: (i, j)),
        out_shape=jax.ShapeDtypeStruct((n, cols), F32),
        scratch_shapes=[pltpu.VMEM((tm, d), BF16)],
        compiler_params=_cparams(("arbitrary", "arbitrary")),
        name="in_proj",
    )(x2d, w_p)


def _shifted(x_ref, prev_ref, mu_ref, first_row):
    x = x_ref[...]
    xp = jnp.where(first_row, prev_ref[...], pltpu.roll(x, 1, axis=0))
    prev_ref[...] = x_ref[x.shape[0] - 1:x.shape[0], :]
    return x + (xp - x) * mu_ref[...]


def _rwkv_kernel(r_ref, k_ref, v_ref, lo_ref, mur_ref, muk_ref, muv_ref, mul_ref,
                 w0_ref, w2d_ref, a0_ref, a2_ref, g2_ref, kk_ref, ka_ref, rk_ref, lnw_ref, lnb_ref,
                 o_ref,
                 pr_ref, pk_ref, pv_ref, plo_ref, s_ref,
                 rs_ref, lw_ref, ks_ref, vs_ref, kks_ref, as_ref, gs_ref):
    tb, width = r_ref.shape
    n_pairs = width // LANES
    c2 = 2 * CHUNK

    @pl.when(pl.program_id(2) == 0)
    def _():
        pr_ref[...] = jnp.zeros_like(pr_ref)
        pk_ref[...] = jnp.zeros_like(pk_ref)
        pv_ref[...] = jnp.zeros_like(pv_ref)
        plo_ref[...] = jnp.zeros_like(plo_ref)
        s_ref[...] = jnp.zeros_like(s_ref)

    first_row = lax.broadcasted_iota(jnp.int32, (tb, 1), 0) == 0
    r = _shifted(r_ref, pr_ref, mur_ref, first_row)
    k = _shifted(k_ref, pk_ref, muk_ref, first_row)
    v = _shifted(v_ref, pv_ref, muv_ref, first_row)
    lo = _shifted(lo_ref, plo_ref, mul_ref, first_row)

    xw = jnp.tanh(lo[:, LORA_XW:LORA_XW + LANES])
    xa = lo[:, LORA_XA:LORA_XA + LANES]
    xg = jax.nn.sigmoid(lo[:, LORA_XG:LORA_PAD])
    z = w0_ref[...] + _bdot(xw, w2d_ref[...])
    w_logit = -(jnp.maximum(-z, 0.0) + jnp.log(1.0 + jnp.exp(-jnp.abs(z)))) - 0.5
    a_ic = jax.nn.sigmoid(a0_ref[...] + _bdot(xa, a2_ref[...]))
    rs_ref[...] = r
    lw_ref[...] = -jnp.exp(w_logit)
    ks_ref[...] = k * (1.0 + (a_ic - 1.0) * ka_ref[...])
    vs_ref[...] = v
    kks_ref[...] = k * kk_ref[...]
    as_ref[...] = a_ic
    gs_ref[...] = _bdot(xg, g2_ref[...])

    lane = lax.broadcasted_iota(jnp.int32, (CHUNK, LANES), 1)
    head0 = lane < HEAD
    row = lax.broadcasted_iota(jnp.int32, (c2, c2), 0)
    col = lax.broadcasted_iota(jnp.int32, (c2, c2), 1)
    strict = row > col
    incl = row >= col
    same_sub = (row // SUB) == (col // SUB)
    eye = (row == col).astype(F32)
    tri = (lax.broadcasted_iota(jnp.int32, (CHUNK, CHUNK), 0)
           >= lax.broadcasted_iota(jnp.int32, (CHUNK, CHUNK), 1)).astype(F32)
    def head_sum(x):
        s0 = jnp.sum(jnp.where(head0, x, 0.0), axis=-1, keepdims=True)
        s1 = jnp.sum(jnp.where(head0, 0.0, x), axis=-1, keepdims=True)
        return jnp.where(head0, s0, s1)

    def stack(x):
        return jnp.concatenate([jnp.where(head0, x, 0.0), jnp.where(head0, 0.0, x)], axis=0)

    def operands(rc, lw, lwi, kc, vc, kk, ac):
        kkn = kk * lax.rsqrt(jnp.maximum(head_sum(kk * kk), 1e-24))
        b_ = kkn * ac
        tot = lwi[CHUNK - 1:CHUNK, :]
        e_inv = jnp.exp(-lwi)
        e_out = jnp.exp(tot - lwi)
        at = stack(-kkn * jnp.exp(lwi - lw)).astype(BF16)
        rt = stack(rc * jnp.exp(lwi)).astype(BF16)
        bk = jnp.concatenate([stack(b_ * e_inv), stack(kc * e_inv)], axis=0).astype(BF16)
        bkh = jnp.concatenate([stack(b_ * e_out), stack(kc * e_out)], axis=0).astype(BF16)
        return at, rt, bk, bkh, stack(vc).astype(BF16), jnp.exp(tot)

    def finish(y_st, rc, kc, vc, gate, rk, lnw, lnb):
        y = y_st[:CHUNK, :] + y_st[CHUNK:, :]
        mu = head_sum(y) * (1.0 / HEAD)
        yc = y - mu
        var = head_sum(yc * yc) * (1.0 / HEAD)
        yn = yc * lax.rsqrt(var + GN_EPS) * lnw + lnb
        yn = yn + head_sum(rc * kc * rk) * vc
        return (yn * gate).astype(o_ref.dtype)

    def chunks(it, carry):
        sls = [pl.ds(pl.multiple_of((it * CHUNKS_PER_ITER + u) * CHUNK, CHUNK), CHUNK)
               for u in range(CHUNKS_PER_ITER)]
        lws = [lw_ref[sl, :] for sl in sls]
        lwis = [jnp.dot(tri, lw, preferred_element_type=F32, precision=lax.Precision.HIGHEST) for lw in lws]
        pairs = range(n_pairs)
        ls = [slice(p * LANES, (p + 1) * LANES) for p in pairs]
        items = [(u, p) for u in range(CHUNKS_PER_ITER) for p in pairs]
        n = range(len(items))
        ops = [operands(rs_ref[sls[u], ls[p]], lws[u][:, ls[p]], lwis[u][:, ls[p]], ks_ref[sls[u], ls[p]],
                        vs_ref[sls[u], ls[p]], kks_ref[sls[u], ls[p]], as_ref[sls[u], ls[p]]) for u, p in items]
        at, rt, bk, bkh, vst, dec = (list(z) for z in zip(*ops))
        a_all = [_bdot_nt(jnp.concatenate([at[i], rt[i]], axis=0), bk[i]) for i in n]
        a_ab = [jnp.where(strict, a[:c2, :c2], 0.0) for a in a_all]
        a_ak = [jnp.where(strict, a[:c2, c2:], 0.0) for a in a_all]
        a_r = [jnp.where(jnp.concatenate([incl, incl], axis=1), a[c2:, :], 0.0).astype(BF16) for a in a_all]
        akv = [_bdot(a_ak[i], vst[i]).astype(BF16) for i in n]

        d1 = [jnp.where(same_sub, a, 0.0) for a in a_ab]
        l1 = [a_ab[i] - d1[i] for i in n]
        d2 = [_bdot(d, d) for d in d1]
        d4 = [_bdot(d, d) for d in d2]
        p12 = [_bdot(eye + d1[i], eye + d2[i]) for i in n]
        d8 = [_bdot(d, d) for d in d4]
        p48 = [_bdot(eye + d4[i], eye + d8[i]) for i in n]
        t_d = [_bdot(p12[i], p48[i]).astype(BF16) for i in n]
        m1 = [_bdot(t_d[i], l1[i]) for i in n]
        m2 = [_bdot(m, m) for m in m1]
        q = [_bdot(eye + m1[i], eye + m2[i]) for i in n]
        t_inv = [_bdot(q[i], t_d[i]) for i in n]
        tx = [_bdot(t_inv[i], jnp.concatenate([at[i], akv[i]], axis=1)) for i in n]

        for u in range(CHUNKS_PER_ITER):
            idx = [u * n_pairs + p for p in pairs]
            s_bd = [s_ref[p] for p in pairs]
            wr = [_bdot_nt(jnp.concatenate([tx[i][:, :LANES].astype(BF16), rt[i]], axis=0), s_bd[p])
                  for p, i in zip(pairs, idx)]
            uv = [jnp.concatenate([(wr[p][:c2, :] + tx[i][:, LANES:]).astype(BF16), vst[i]], axis=0)
                  for p, i in zip(pairs, idx)]
            y_st = [wr[p][c2:, :] + _bdot(a_r[i], uv[p]) for p, i in zip(pairs, idx)]
            for p, i in zip(pairs, idx):
                s_ref[p] = s_bd[p] * dec[i] + _bdot_tn(uv[p], bkh[i])
            for p, l in zip(pairs, ls):
                sl = sls[u]
                o_ref[sl, l] = finish(y_st[p], rs_ref[sl, l], ks_ref[sl, l], vs_ref[sl, l], gs_ref[sl, l],
                                      rk_ref[:, l], lnw_ref[:, l], lnb_ref[:, l])
        return carry

    lax.fori_loop(0, tb // (CHUNK * CHUNKS_PER_ITER), chunks, 0)


RWKV_PAIRS_PER_STEP = 8


def _rwkv_group(proj, batch, seq, mu_rkv, mu_lora, w0, w2d, a0, a2, g2, k_k, k_a, r_k, lnx_w, lnx_b, tb=512,
                pairs=RWKV_PAIRS_PER_STEP):
    width = pairs * LANES
    n_g = RWKV_WIDTH // width
    n_tb = seq // tb
    row = lambda b, h, t: b * n_tb + t
    vec = lambda off: pl.BlockSpec((1, width), lambda b, h, t: (0, off + h))
    lora_blk = LORA_COL0 // LORA_PAD
    in_specs = [
        pl.BlockSpec((tb, width), lambda b, h, t: (row(b, h, t), h)),
        pl.BlockSpec((tb, width), lambda b, h, t: (row(b, h, t), n_g + h)),
        pl.BlockSpec((tb, width), lambda b, h, t: (row(b, h, t), 2 * n_g + h)),
        pl.BlockSpec((tb, LORA_PAD), lambda b, h, t: (row(b, h, t), lora_blk)),
        vec(0), vec(n_g), vec(2 * n_g),
        pl.BlockSpec((1, LORA_PAD), lambda b, h, t: (0, 0)),
        vec(0),
        pl.BlockSpec((LANES, width), lambda b, h, t: (0, h)),
        vec(0),
        pl.BlockSpec((LANES, width), lambda b, h, t: (0, h)),
        pl.BlockSpec((LORA_PAD - LORA_XG, width), lambda b, h, t: (0, h)),
        vec(0), vec(0), vec(0), vec(0), vec(0),
    ]
    blk = lambda: pltpu.VMEM((tb, width), F32)
    return pl.pallas_call(
        _rwkv_kernel,
        grid=(batch, n_g, n_tb),
        in_specs=in_specs,
        out_specs=pl.BlockSpec((tb, width), lambda b, h, t: (row(b, h, t), h)),
        out_shape=jax.ShapeDtypeStruct((batch * seq, RWKV_WIDTH), BF16),
        scratch_shapes=[pltpu.VMEM((1, width), F32), pltpu.VMEM((1, width), F32), pltpu.VMEM((1, width), F32),
                        pltpu.VMEM((1, LORA_PAD), F32), pltpu.VMEM((pairs, 2 * CHUNK, LANES), F32),
                        blk(), blk(), blk(), blk(), blk(), blk(), blk()],
        compiler_params=_cparams(("arbitrary", "arbitrary", "arbitrary")),
        name="rwkv7_group",
    )(proj, proj, proj, proj, mu_rkv, mu_rkv, mu_rkv, mu_lora, w0, w2d, a0, a2, g2, k_k, k_a, r_k, lnx_w, lnx_b)


HALO = 16


def _pool_kernel(p_ref, w_ref, sc_ref, o_ref, ext_ref):
    tb = p_ref.shape[0]
    t = pl.program_id(1)

    @pl.when(t == 0)
    def _():
        ext_ref[0:HALO, :] = jnp.zeros((HALO, ext_ref.shape[1]), F32)

    @pl.when(t > 0)
    def _():
        ext_ref[0:HALO, :] = ext_ref[tb:tb + HALO, :]

    ext_ref[HALO:HALO + tb, :] = p_ref[...]
    t_idx = t * tb + lax.broadcasted_iota(jnp.int32, (tb, 1), 0)
    for gi, win in enumerate(POOL_WINDOWS):
        cs = slice(gi * POOL_GROUP_WIDTH, (gi + 1) * POOL_GROUP_WIDTH)
        acc = ext_ref[HALO:HALO + tb, cs]
        for d in range(1, win):
            acc = acc + ext_ref[HALO - d:HALO - d + tb, cs]
        count = jnp.minimum(t_idx + 1, win).astype(F32)
        pooled = acc / count - ext_ref[HALO:HALO + tb, cs]
        mixed = _bdot(pooled, w_ref[gi])
        o_ref[:, cs] = (mixed * sc_ref[:, cs]).astype(o_ref.dtype)


def _pool_group(proj, batch, seq, w_pool_b, pool_scale, tb=512):
    n_tb = seq // tb
    col_blk = POOL_COL0 // POOL_WIDTH
    return pl.pallas_call(
        _pool_kernel,
        grid=(batch, n_tb),
        in_specs=[pl.BlockSpec((tb, POOL_WIDTH), lambda b, t: (b * n_tb + t, col_blk)),
                  pl.BlockSpec(w_pool_b.shape, lambda b, t: (0, 0, 0)),
                  pl.BlockSpec((1, POOL_WIDTH), lambda b, t: (0, 0))],
        out_specs=pl.BlockSpec((tb, POOL_WIDTH), lambda b, t: (b * n_tb + t, 0)),
        out_shape=jax.ShapeDtypeStruct((batch * seq, POOL_WIDTH), BF16),
        scratch_shapes=[pltpu.VMEM((tb + HALO, POOL_WIDTH), F32)],
        compiler_params=_cparams(("arbitrary", "arbitrary")),
        name="pool_group",
    )(proj, w_pool_b, pool_scale)


def _layer_norm(x, w, b, eps):
    mu = jnp.mean(x, axis=-1, keepdims=True)
    xc = x - mu
    var = jnp.mean(xc * xc, axis=-1, keepdims=True)
    return xc * lax.rsqrt(var + eps) * w + b


def _mix_kernel(yr_ref, yp_ref, x_ref, wo_ref, lw_ref, lb_ref, rwh_ref, rwl_ref, rb_ref,
                h_ref, hp_ref, ri_ref, g_ref, cnt_ref, carry_ref):
    tm = x_ref.shape[0]
    half = yr_ref.shape[1]

    @pl.when(pl.program_id(0) == 0)
    def _():
        carry_ref[...] = jnp.zeros_like(carry_ref)

    mix = (jnp.dot(yr_ref[...], wo_ref[0:half, :], preferred_element_type=F32)
           + jnp.dot(yp_ref[...], wo_ref[half:, :], preferred_element_type=F32))
    h = _layer_norm(DEEPNORM_ALPHA * x_ref[...] + mix, lw_ref[...], lb_ref[...], LN_EPS)
    h_ref[...] = h
    half_d = h.shape[1] // 2
    lo = pltpu.bitcast(h[:, :half_d].astype(BF16).astype(F32), jnp.uint32) >> 16
    hi = pltpu.bitcast(h[:, half_d:].astype(BF16).astype(F32), jnp.uint32)
    words = lo | hi
    chunks = jnp.stack([words[:, c * LANES:(c + 1) * LANES] for c in range(hp_ref.shape[1])], axis=0)
    hp_ref[...] = pltpu.einshape("ctl->tcl", chunks)

    h_hi = h.astype(BF16)
    h_lo = (h - h_hi.astype(F32)).astype(BF16)
    logits = (jnp.dot(h_hi, rwh_ref[...], preferred_element_type=F32)
              + jnp.dot(h_lo, rwh_ref[...], preferred_element_type=F32)
              + jnp.dot(h_hi, rwl_ref[...], preferred_element_type=F32)) + rb_ref[...]
    lane = lax.broadcasted_iota(jnp.int32, (tm, LANES), 1).astype(F32)
    idxs, vals = [], []
    left = logits
    for _ in range(TOP_K):
        m = jnp.max(left, axis=-1, keepdims=True)
        idx = jnp.min(jnp.where(left == m, lane, float(LANES)), axis=-1, keepdims=True)
        idxs.append(idx)
        vals.append(m)
        left = jnp.where(lane == idx, -jnp.inf, left)
    exps = [jnp.exp(v - vals[0]) for v in vals]
    denom = exps[0] + exps[1] + exps[2] + exps[3]

    onehot = jnp.zeros((tm, LANES), F32)
    for idx in idxs:
        onehot = onehot + (lane == idx).astype(F32)
    tri = (lax.broadcasted_iota(jnp.int32, (tm, tm), 0)
           > lax.broadcasted_iota(jnp.int32, (tm, tm), 1)).astype(BF16)
    before = jnp.dot(tri, onehot.astype(BF16), preferred_element_type=F32) + carry_ref[...]
    carry_ref[...] = carry_ref[...] + jnp.sum(onehot, axis=0, keepdims=True)
    cnt_ref[...] = carry_ref[...]

    ri = jnp.zeros((tm, LANES), F32)
    gt = jnp.zeros((tm, LANES), F32)
    for k in range(TOP_K):
        rank = jnp.sum(jnp.where(lane == idxs[k], before, 0.0), axis=-1, keepdims=True)
        ri = jnp.where(lane == float(k), idxs[k], ri)
        ri = jnp.where(lane == float(TOP_K + k), rank, ri)
        gt = jnp.where(lane == float(k), exps[k] / denom, gt)
    ri_ref[...] = ri.astype(jnp.int32)
    g_ref[...] = gt


def _mix_ln_router(y_rwkv, y_pool, x2d, w_out_b, ln_w, ln_b, router_w_p, router_b_p, tm=512):
    n, d = x2d.shape
    rw_hi = router_w_p.astype(BF16)
    rw_lo = (router_w_p - rw_hi.astype(F32)).astype(BF16)
    half = y_rwkv.shape[1]
    const = lambda shape: pl.BlockSpec(shape, lambda i: (0, 0))
    return pl.pallas_call(
        _mix_kernel,
        grid=(n // tm,),
        in_specs=[pl.BlockSpec((tm, half), lambda i: (i, 0)),
                  pl.BlockSpec((tm, y_pool.shape[1]), lambda i: (i, 0)),
                  pl.BlockSpec((tm, d), lambda i: (i, 0)),
                  const(w_out_b.shape), const((1, d)), const((1, d)),
                  const(router_w_p.shape), const(router_w_p.shape), const((1, LANES))],
        out_specs=[pl.BlockSpec((tm, d), lambda i: (i, 0)),
                   pl.BlockSpec((tm, d // 2 // LANES, LANES), lambda i: (i, 0, 0)),
                   pl.BlockSpec((tm, LANES), lambda i: (i, 0)),
                   pl.BlockSpec((tm, LANES), lambda i: (i, 0)),
                   const((1, LANES))],
        out_shape=[jax.ShapeDtypeStruct((n, d), F32),
                   jax.ShapeDtypeStruct((n, d // 2 // LANES, LANES), jnp.uint32),
                   jax.ShapeDtypeStruct((n, LANES), jnp.int32),
                   jax.ShapeDtypeStruct((n, LANES), F32),
                   jax.ShapeDtypeStruct((1, LANES), F32)],
        scratch_shapes=[pltpu.VMEM((1, LANES), F32)],
        compiler_params=_cparams(("arbitrary",)),
        name="mix_ln_router",
    )(y_rwkv, y_pool, x2d, w_out_b, ln_w, ln_b, rw_hi, rw_lo, router_b_p)


MOE_BM = 512
GATHER_UNROLL = 8


MXU_DIM = 256
OUT_CHUNK = 512


def _split_w1_kernel(w_ref, g_ref, l_ref):
    half = MXU_DIM // 2
    src = lax.broadcasted_iota(jnp.int32, (MXU_DIM, MXU_DIM), 0)
    dst = lax.broadcasted_iota(jnp.int32, (MXU_DIM, MXU_DIM), 1)
    perm = (src == jnp.where(dst < half, 2 * dst, 2 * (dst - half) + 1)).astype(BF16)
    for c in range(w_ref.shape[2] // MXU_DIM):
        w = w_ref[0, :, c * MXU_DIM:(c + 1) * MXU_DIM].astype(BF16)
        o = jnp.dot(w, perm, preferred_element_type=F32)
        g_ref[0, :, c * half:(c + 1) * half] = o[:, :half].astype(BF16)
        l_ref[0, :, c * half:(c + 1) * half] = o[:, half:].astype(BF16)


def _split_w1(w1, tr=512, tc=2048):
    n_exp, d, f2 = w1.shape
    out = jax.ShapeDtypeStruct((n_exp, d, f2 // 2), BF16)
    return pl.pallas_call(
        _split_w1_kernel,
        grid=(n_exp, d // tr, f2 // tc),
        in_specs=[pl.BlockSpec((1, tr, tc), lambda e, i, j: (e, i, j))],
        out_specs=[pl.BlockSpec((1, tr, tc // 2), lambda e, i, j: (e, i, j)),
                   pl.BlockSpec((1, tr, tc // 2), lambda e, i, j: (e, i, j))],
        out_shape=[out, out],
        compiler_params=_cparams(("arbitrary", "arbitrary", "arbitrary")),
        name="split_w1",
    )(w1)


def _cast_kernel(w_ref, o_ref):
    o_ref[...] = w_ref[...].astype(o_ref.dtype)


def _cast_bf16(w, tr=512):
    n_exp, r, c = w.shape
    return pl.pallas_call(
        _cast_kernel,
        grid=(n_exp, r // tr),
        in_specs=[pl.BlockSpec((1, tr, c), lambda e, i: (e, i, 0))],
        out_specs=pl.BlockSpec((1, tr, c), lambda e, i: (e, i, 0)),
        out_shape=jax.ShapeDtypeStruct(w.shape, BF16),
        compiler_params=_cparams(("arbitrary", "arbitrary")),
        name="cast_w2",
    )(w)


def _scatter_kernel(pos_ref, hp_ref, zero_hbm, x_hbm, sem):
    del zero_hbm
    i = pl.program_id(0)
    tm = hp_ref.shape[0]

    def group(gi, carry):
        n0 = pl.multiple_of(gi * GATHER_UNROLL, GATHER_UNROLL)
        for j in range(GATHER_UNROLL):
            for k in range(TOP_K):
                p = pos_ref[(i * tm + n0 + j) * TOP_K + k]
                pltpu.make_async_copy(hp_ref.at[n0 + j], x_hbm.at[p], sem).start(priority=k % 2)
        return carry

    lax.fori_loop(0, tm // GATHER_UNROLL, group, 0)
    for k in range(TOP_K):
        pltpu.make_async_copy(hp_ref, x_hbm.at[pl.ds(0, tm)], sem).wait()


def _dispatch(h_packed, pos_flat, n_slots, tm=512):
    n = h_packed.shape[0]
    tile = h_packed.shape[1:]
    return pl.pallas_call(
        _scatter_kernel,
        grid_spec=pltpu.PrefetchScalarGridSpec(
            num_scalar_prefetch=1,
            grid=(n // tm,),
            in_specs=[pl.BlockSpec((tm,) + tile, lambda i, pos: (i, 0, 0)),
                      pl.BlockSpec(memory_space=pl.ANY)],
            out_specs=pl.BlockSpec(memory_space=pl.ANY),
            scratch_shapes=[pltpu.SemaphoreType.DMA(())]),
        out_shape=jax.ShapeDtypeStruct((n_slots,) + tile, h_packed.dtype),
        input_output_aliases={2: 0},
        compiler_params=_cparams(("arbitrary",)),
        name="moe_dispatch",
    )(pos_flat, h_packed, jnp.zeros((n_slots,) + tile, h_packed.dtype))


ROW_GROUP = 128


def _ffn_hidden(words, w1g_ref, w1l_ref, b1g_ref, b1l_ref):
    packed = pltpu.einshape("tcl->ctl", words)
    chunks = [packed[c] for c in range(words.shape[1])]
    x = jnp.concatenate([pltpu.bitcast(w << 16, F32).astype(BF16) for w in chunks]
                        + [pltpu.bitcast(w & jnp.uint32(0xFFFF0000), F32).astype(BF16) for w in chunks], axis=1)
    hg = jnp.dot(x, w1g_ref[0], preferred_element_type=F32) + b1g_ref[0]
    hl = jnp.dot(x, w1l_ref[0], preferred_element_type=F32) + b1l_ref[0]
    x_glu = jnp.minimum(hg, SWIGLU_LIMIT)
    x_lin = jnp.clip(hl, -SWIGLU_LIMIT, SWIGLU_LIMIT)
    return (x_glu * jax.nn.sigmoid(SWIGLU_ALPHA * x_glu) * (x_lin + 1.0)).astype(BF16)


def _ffn_kernel(be_ref, nu_ref, bv_ref, x_ref, w1g_ref, w1l_ref, b1g_ref, b1l_ref, w2_ref, b2_ref, o_ref):
    m = pl.program_id(0)
    f = pl.program_id(1)
    bm = o_ref.shape[0]
    valid = bv_ref[m]

    @pl.when(f == 0)
    def _():
        o_ref[...] = jnp.broadcast_to(b2_ref[0], o_ref.shape)

    whole = valid > bm - ROW_GROUP

    @pl.when(whole)
    def _():
        act = _ffn_hidden(x_ref[...], w1g_ref, w1l_ref, b1g_ref, b1l_ref)
        for c in range(0, o_ref.shape[1], OUT_CHUNK):
            o_ref[:, c:c + OUT_CHUNK] += jnp.dot(act, w2_ref[0, :, c:c + OUT_CHUNK], preferred_element_type=F32)

    @pl.when((valid > 0) & jnp.logical_not(whole))
    def _():
        def group(g, carry):
            rows = pl.ds(pl.multiple_of(g * ROW_GROUP, ROW_GROUP), ROW_GROUP)
            act = _ffn_hidden(x_ref[rows], w1g_ref, w1l_ref, b1g_ref, b1l_ref)
            for c in range(0, o_ref.shape[1], OUT_CHUNK):
                o_ref[rows, c:c + OUT_CHUNK] += jnp.dot(act, w2_ref[0, :, c:c + OUT_CHUNK],
                                                        preferred_element_type=F32)
            return carry

        lax.fori_loop(0, (valid + ROW_GROUP - 1) // ROW_GROUP, group, 0)


def _experts(x_slots, block_expert, n_used, block_valid, w1g, w1l, b1g, b1l, w2, b2, bm=MOE_BM, tf=1024):
    d = w1g.shape[1]
    n_slots = x_slots.shape[0]
    n_blocks = n_slots // bm
    ff = w1g.shape[2]
    n_f = ff // tf

    def m_eff(m, nu):
        return jnp.minimum(m, jnp.maximum(nu[0] - 1, 0))

    def f_eff(m, f, nu):
        return jnp.where(m < nu[0], f, n_f - 1)

    return pl.pallas_call(
        _ffn_kernel,
        grid_spec=pltpu.PrefetchScalarGridSpec(
            num_scalar_prefetch=3,
            grid=(n_blocks, n_f),
            in_specs=[
                pl.BlockSpec((bm,) + x_slots.shape[1:], lambda m, f, be, nu, bv: (m_eff(m, nu), 0, 0)),
                pl.BlockSpec((1, d, tf), lambda m, f, be, nu, bv: (be[m], 0, f_eff(m, f, nu))),
                pl.BlockSpec((1, d, tf), lambda m, f, be, nu, bv: (be[m], 0, f_eff(m, f, nu))),
                pl.BlockSpec((1, 1, tf), lambda m, f, be, nu, bv: (be[m], 0, f_eff(m, f, nu))),
                pl.BlockSpec((1, 1, tf), lambda m, f, be, nu, bv: (be[m], 0, f_eff(m, f, nu))),
                pl.BlockSpec((1, tf, d), lambda m, f, be, nu, bv: (be[m], f_eff(m, f, nu), 0)),
                pl.BlockSpec((1, 1, d), lambda m, f, be, nu, bv: (be[m], 0, 0)),
            ],
            out_specs=pl.BlockSpec((bm, d), lambda m, f, be, nu, bv: (m, 0))),
        out_shape=jax.ShapeDtypeStruct((n_slots, d), F32),
        compiler_params=_cparams(("arbitrary", "arbitrary")),
        name="moe_experts",
    )(block_expert, n_used, block_valid, x_slots, w1g, w1l, b1g, b1l, w2, b2)


def _combine_kernel(pos_ref, h_ref, g_ref, lw_ref, lb_ref, y_hbm, o_ref, buf, sem):
    i = pl.program_id(0)
    tm = h_ref.shape[0]

    def issue_step(step, slot):
        def group(gi, carry):
            n0 = pl.multiple_of(gi * GATHER_UNROLL, GATHER_UNROLL)
            for j in range(GATHER_UNROLL):
                for k in range(TOP_K):
                    p = pos_ref[(step * tm + n0 + j) * TOP_K + k]
                    pltpu.make_async_copy(y_hbm.at[pl.ds(p, 1), :], buf.at[slot, pl.ds(k * tm + n0 + j, 1), :],
                                          sem.at[slot]).start()
            return carry

        lax.fori_loop(0, tm // GATHER_UNROLL, group, 0)

    @pl.when(i == 0)
    def _():
        issue_step(0, 0)

    @pl.when(i + 1 < pl.num_programs(0))
    def _():
        issue_step(i + 1, (i + 1) % 2)

    slot = i % 2
    pltpu.make_async_copy(y_hbm.at[pl.ds(0, TOP_K * tm), :], buf.at[slot], sem.at[slot]).wait()
    g = g_ref[...]
    ffn = g[:, 0:1] * buf[slot, 0:tm, :]
    for k in range(1, TOP_K):
        ffn = ffn + g[:, k:k + 1] * buf[slot, k * tm:(k + 1) * tm, :]
    o_ref[...] = _layer_norm(DEEPNORM_ALPHA * h_ref[...] + ffn, lw_ref[...], lb_ref[...], LN_EPS)


def _combine(pos_flat, h, gates, ln_w, ln_b, y_slots, tm=256):
    n, d = h.shape
    return pl.pallas_call(
        _combine_kernel,
        grid_spec=pltpu.PrefetchScalarGridSpec(
            num_scalar_prefetch=1,
            grid=(n // tm,),
            in_specs=[pl.BlockSpec((tm, d), lambda i, pos: (i, 0)),
                      pl.BlockSpec((tm, LANES), lambda i, pos: (i, 0)),
                      pl.BlockSpec((1, d), lambda i, pos: (0, 0)),
                      pl.BlockSpec((1, d), lambda i, pos: (0, 0)),
                      pl.BlockSpec(memory_space=pl.ANY)],
            out_specs=pl.BlockSpec((tm, d), lambda i, pos: (i, 0)),
            scratch_shapes=[pltpu.VMEM((2, TOP_K * tm, d), F32), pltpu.SemaphoreType.DMA((2,))]),
        out_shape=jax.ShapeDtypeStruct((n, d), F32),
        compiler_params=_cparams(("arbitrary",)),
        name="moe_combine_ln",
    )(pos_flat, h, gates, ln_w, ln_b, y_slots)


def _moe(h, h_packed, route_i, gates, counts_f, w1, b1, w2, b2, ln_w, ln_b):
    n_tok, d = h.shape
    n_exp = w1.shape[0]
    n_asg = n_tok * TOP_K
    n_blocks = -(-n_asg // MOE_BM) + n_exp

    counts = counts_f[0, :n_exp].astype(jnp.int32)
    padded = (counts + MOE_BM - 1) // MOE_BM * MOE_BM
    pend = jnp.cumsum(padded)
    pstart = pend - padded
    pos = pstart[route_i[:, 0:TOP_K]] + route_i[:, TOP_K:2 * TOP_K]
    pos_flat = pos.reshape(n_asg)
    n_used = (pend[-1] // MOE_BM).astype(jnp.int32).reshape(1)
    blk = jnp.minimum(jnp.arange(n_blocks, dtype=jnp.int32), n_used[0] - 1)
    block_expert = jnp.minimum(jnp.sum(pend[None, :] <= (blk * MOE_BM)[:, None], axis=1), n_exp - 1).astype(jnp.int32)
    block_valid = jnp.clip((pstart + counts)[block_expert] - jnp.arange(n_blocks, dtype=jnp.int32) * MOE_BM,
                           0, MOE_BM).astype(jnp.int32)

    w1g, w1l = _split_w1(w1)
    b1g = b1[:, None, 0::2]
    b1l = b1[:, None, 1::2]
    w2b = _cast_bf16(w2)

    x_slots = _dispatch(h_packed, pos_flat, n_blocks * MOE_BM)
    y_slots = _experts(x_slots, block_expert, n_used, block_valid, w1g, w1l, b1g, b1l, w2b, b2[:, None, :])
    return _combine(pos_flat, h, gates, ln_w, ln_b, y_slots)


def _pad_to(a, axis, size):
    pad = [(0, 0)] * a.ndim
    pad[axis] = (0, size - a.shape[axis])
    return jnp.pad(a, pad)


def kernel(x, w_in, mu_shift, w0, w2_decay, a0, a2_iclr, g2_gate, k_k, k_a, r_k, lnx_w, lnx_b, w_pool, pool_scale,
           w_out, ln1_w, ln1_b, router_w, router_b, w1_exp, b1_exp, w2_exp, b2_exp, ln2_w, ln2_b):
    batch, seq, d = x.shape
    assert w_in.shape[0] == 1, "one layer"
    x2d = x.reshape(batch * seq, d)
    c_xw = RKV_COLS
    c_xa = c_xw + DECAY_LORA
    c_xg = c_xa + ICLR_LORA
    c_pool = c_xg + GATE_LORA

    def lora_layout(a):
        return jnp.concatenate([_pad_to(a[:, c_xw:c_xa], 1, LORA_XA - LORA_XW),
                                _pad_to(a[:, c_xa:c_xg], 1, LORA_XG - LORA_XA),
                                _pad_to(a[:, c_xg:c_pool], 1, LORA_PAD - LORA_XG)], axis=1)

    wi = w_in[0]
    w_p = jnp.concatenate([wi[:, :RKV_COLS], wi[:, c_pool:], lora_layout(wi)], axis=1).astype(BF16)
    mu = mu_shift[0][None, :]
    row = lambda a: a.reshape(1, -1)

    proj = _in_proj(x2d, w_p)
    y_rwkv = _rwkv_group(
        proj, batch, seq, mu[:, :RKV_COLS], lora_layout(mu), row(w0[0]),
        _pad_to(w2_decay[0], 0, LANES).astype(BF16), row(a0[0]), _pad_to(a2_iclr[0], 0, LANES).astype(BF16),
        _pad_to(g2_gate[0], 0, LORA_PAD - LORA_XG).astype(BF16), row(k_k[0]), row(k_a[0]), row(r_k[0]),
        row(lnx_w[0]), row(lnx_b[0]))
    y_pool = _pool_group(proj, batch, seq, w_pool[0].astype(BF16), row(pool_scale[0]))

    n_exp = router_w.shape[2]
    router_w_p = _pad_to(router_w[0], 1, LANES)
    router_b_p = jnp.concatenate([router_b[0], jnp.full((LANES - n_exp,), -1e30, F32)])[None, :]
    h1, h1_packed, route_i, gates, counts = _mix_ln_router(y_rwkv, y_pool, x2d, w_out[0].astype(BF16), row(ln1_w[0]),
                                                row(ln1_b[0]), router_w_p, router_b_p)
    out = _moe(h1, h1_packed, route_i, gates, counts, w1_exp[0], b1_exp[0], w2_exp[0], b2_exp[0], row(ln2_w[0]), row(ln2_b[0]))
    return out.reshape(batch, seq, d)
```
